```python
import math
import jax, jax.numpy as jnp
from jax import lax
import numpy as np

D_MODEL = 2048
BATCH = 4
SEQ = 4096
DEPTH = 1

A_HEADS = 8
A_HEAD_DIM = 128
A_WIDTH = A_HEADS * A_HEAD_DIM
MOBA_BLOCK = 256
MOBA_TOPK = 3
MOBA_Q_CHUNK = 32
B_Q_HEADS = 16
B_KV_HEADS = 2
B_HEAD_DIM = 64
B_WIDTH = B_Q_HEADS * B_HEAD_DIM
B_KV_WIDTH = B_KV_HEADS * B_HEAD_DIM
WINDOW = 128
NUM_BUCKETS = 32
MAX_DISTANCE = 128
MAX_EXACT = NUM_BUCKETS // 2
N_BIAS_HEADS = A_HEADS + B_Q_HEADS
D_FF = -(-(8 * D_MODEL) // (3 * 256)) * 256
IN_SPLITS = [A_WIDTH, 2 * A_WIDTH, 3 * A_WIDTH,
             3 * A_WIDTH + B_WIDTH,
             3 * A_WIDTH + B_WIDTH + B_KV_WIDTH,
             3 * A_WIDTH + B_WIDTH + 2 * B_KV_WIDTH,
             3 * A_WIDTH + B_WIDTH + 2 * B_KV_WIDTH + D_MODEL]
IN_WIDTH = 3 * A_WIDTH + B_WIDTH + 2 * B_KV_WIDTH + 2 * D_MODEL
EPS = 1e-6
NEG = -1e30

kernel_name = "hybrid_moba_swa_sink_t5bias_block"


def rmsnorm(x, g):
    xf = x.astype(jnp.float32)
    y = xf * lax.rsqrt(jnp.mean(xf * xf, axis=-1, keepdims=True) + EPS)
    return y.astype(x.dtype) * g.astype(x.dtype)


def t5_bucket(dist):
    n = jnp.maximum(dist, 0)
    nf = jnp.maximum(n, 1).astype(jnp.float32)
    large = MAX_EXACT + (jnp.log(nf / MAX_EXACT) / math.log(MAX_DISTANCE / MAX_EXACT)
                         * (NUM_BUCKETS - MAX_EXACT)).astype(jnp.int32)
    large = jnp.minimum(large, NUM_BUCKETS - 1)
    return jnp.where(n < MAX_EXACT, n, large)


def moba_attention(q, k, v, table_a):
    B_, H, S, Dh = q.shape
    L = MOBA_BLOCK
    nb = -(-S // L)
    s_pad = nb * L
    pad = ((0, 0), (0, 0), (0, s_pad - S), (0, 0))
    kb = jnp.pad(k, pad).reshape(B_, H, nb, L, Dh)
    vb = jnp.pad(v, pad).reshape(B_, H, nb, L, Dh)
    kmean = jnp.mean(kb.astype(jnp.float32), axis=3)
    gate = jnp.einsum('bhsd,bhnd->bhsn', q.astype(jnp.float32), kmean)
    q_blk = jnp.arange(S) // L
    past = jnp.arange(nb)[None, :] < q_blk[:, None]
    gate = jnp.where(past, gate, NEG)
    k_sel = min(MOBA_TOPK, nb)
    _, sel = lax.top_k(gate, k_sel)
    sel_valid = sel < q_blk[:, None]

    C = MOBA_Q_CHUNK
    n_chunks = S // C
    q_ch = q.reshape(B_, H, n_chunks, C, Dh).transpose(2, 0, 1, 3, 4)
    sel_ch = sel.reshape(B_, H, n_chunks, C, k_sel).transpose(2, 0, 1, 3, 4)
    val_ch = sel_valid.reshape(B_, H, n_chunks, C, k_sel).transpose(2, 0, 1, 3, 4)
    b_ix = jnp.arange(B_)[:, None, None, None]
    h_ix = jnp.arange(H)[None, :, None, None]
    offs = jnp.arange(L)
    scale = Dh ** -0.5

    def chunk(args):
        c, qc, selc, validc = args
        q0 = c * C
        qpos = q0 + jnp.arange(C)
        own = q0 // L
        k_g = kb[b_ix, h_ix, selc]
        v_g = vb[b_ix, h_ix, selc]
        k_own = lax.dynamic_index_in_dim(kb, own, axis=2, keepdims=False)
        v_own = lax.dynamic_index_in_dim(vb, own, axis=2, keepdims=False)
        s_sel = jnp.einsum('bhcd,bhcnld->bhcnl', qc, k_g).reshape(B_, H, C, k_sel * L)
        s_own = jnp.einsum('bhcd,bhld->bhcl', qc, k_own)
        kpos_sel = (selc[..., None] * L + offs).reshape(B_, H, C, k_sel * L)
        dist_sel = qpos[None, None, :, None] - kpos_sel
        bias_sel = table_a[t5_bucket(dist_sel), h_ix].astype(jnp.float32)
        dist_own = qpos[:, None] - (own * L + offs)[None, :]
        bias_own = table_a[t5_bucket(dist_own)].transpose(2, 0, 1).astype(jnp.float32)
        mask_sel = jnp.broadcast_to(validc[..., None], (B_, H, C, k_sel, L)).reshape(B_, H, C, k_sel * L)
        mask_own = dist_own >= 0
        l_sel = jnp.where(mask_sel, s_sel.astype(jnp.float32) * scale + bias_sel, NEG)
        l_own = jnp.where(mask_own, s_own.astype(jnp.float32) * scale + bias_own, NEG)
        p = jax.nn.softmax(jnp.concatenate([l_sel, l_own], axis=-1), axis=-1).astype(v.dtype)
        p_sel = p[..., :k_sel * L].reshape(B_, H, C, k_sel, L)
        p_own = p[..., k_sel * L:]
        return (jnp.einsum('bhcnl,bhcnld->bhcd', p_sel, v_g)
                + jnp.einsum('bhcl,bhld->bhcd', p_own, v_own))

    out = lax.map(chunk, (jnp.arange(n_chunks), q_ch, sel_ch, val_ch))
    return out.transpose(1, 2, 0, 3, 4).reshape(B_, H, S, Dh)


def swa_sink_attention(q, k, v, sinks, table_b):
    B_, Hq, S, Dh = q.shape
    Hkv = k.shape[1]
    G = Hq // Hkv
    W = WINDOW
    nq = S // W
    qb = q.reshape(B_, Hkv, G, nq, W, Dh)
    pad = ((0, 0), (0, 0), (W, 0), (0, 0))
    kp = jnp.pad(k, pad).reshape(B_, Hkv, nq + 1, W, Dh)
    vp = jnp.pad(v, pad).reshape(B_, Hkv, nq + 1, W, Dh)
    kband = jnp.concatenate([kp[:, :, :-1], kp[:, :, 1:]], axis=3)
    vband = jnp.concatenate([vp[:, :, :-1], vp[:, :, 1:]], axis=3)
    s = jnp.einsum('bkgnqd,bknld->bkgnql', qb, kband).astype(jnp.float32) * (Dh ** -0.5)
    qpos = jnp.arange(nq)[:, None] * W + jnp.arange(W)[None, :]
    kpos = jnp.arange(nq)[:, None] * W - W + jnp.arange(2 * W)[None, :]
    dist = qpos[:, :, None] - kpos[:, None, :]
    mask = (dist >= 0) & (dist < W) & (kpos[:, None, :] >= 0)
    bias = table_b[t5_bucket(dist)].astype(jnp.float32)
    bias = bias.transpose(3, 0, 1, 2).reshape(Hkv, G, nq, W, 2 * W)
    logits = jnp.where(mask, s + bias, NEG)
    sink = jnp.broadcast_to(sinks.astype(jnp.float32).reshape(1, Hkv, G, 1, 1, 1),
                            (B_, Hkv, G, nq, W, 1))
    p = jax.nn.softmax(jnp.concatenate([logits, sink], axis=-1), axis=-1)[..., :-1]
    out = jnp.einsum('bkgnql,bknld->bkgnqd', p.astype(v.dtype), vband)
    return out.reshape(B_, Hq, S, Dh)


def setup_inputs(seed: int = 0) -> dict:
    key = jax.random.key(seed)
    ks = jax.random.split(key, 16)

    def nrm(k, shape, scale):
        return jax.random.normal(k, shape, jnp.float32) * scale

    return {
        "x": nrm(ks[0], (BATCH, SEQ, D_MODEL), 1.0),
        "norm1_g": 1.0 + nrm(ks[1], (DEPTH, D_MODEL), 0.02),
        "w_in": nrm(ks[2], (DEPTH, D_MODEL, IN_WIDTH), D_MODEL ** -0.5),
        "q_norm_a": 1.0 + nrm(ks[3], (DEPTH, A_HEAD_DIM), 0.02),
        "k_norm_a": 1.0 + nrm(ks[4], (DEPTH, A_HEAD_DIM), 0.02),
        "q_norm_b": 1.0 + nrm(ks[5], (DEPTH, B_HEAD_DIM), 0.02),
        "k_norm_b": 1.0 + nrm(ks[6], (DEPTH, B_HEAD_DIM), 0.02),
        "rel_bias": nrm(ks[7], (NUM_BUCKETS, N_BIAS_HEADS), 0.5),
        "sinks": nrm(ks[8], (DEPTH, B_Q_HEADS), 0.5),
        "w_branch_a": nrm(ks[9], (DEPTH, A_WIDTH, D_MODEL), A_WIDTH ** -0.5),
        "w_branch_b": nrm(ks[10], (DEPTH, B_WIDTH, D_MODEL), B_WIDTH ** -0.5),
        "w_out": nrm(ks[11], (DEPTH, D_MODEL, D_MODEL), D_MODEL ** -0.5),
        "norm2_g": 1.0 + nrm(ks[12], (DEPTH, D_MODEL), 0.02),
        "w_gate_up": nrm(ks[13], (DEPTH, D_MODEL, 2 * D_FF), D_MODEL ** -0.5),
        "w_down": nrm(ks[14], (DEPTH, D_FF, D_MODEL), D_FF ** -0.5),
    }


def reference(x, norm1_g, w_in, q_norm_a, k_norm_a, q_norm_b, k_norm_b, rel_bias, sinks,
              w_branch_a, w_branch_b, w_out, norm2_g, w_gate_up, w_down):
    B_, S, _ = x.shape
    table_a = rel_bias[:, :A_HEADS]
    table_b = rel_bias[:, A_HEADS:]

    def heads(t, n, d):
        return t.reshape(B_, S, n, d).transpose(0, 2, 1, 3)

    for l in range(DEPTH):
        h = rmsnorm(x, norm1_g[l])
        proj = h @ w_in[l].astype(h.dtype)
        qa, ka, va, qb, kb, vb, ga, gb = jnp.split(proj, IN_SPLITS, axis=-1)
        qa = rmsnorm(heads(qa, A_HEADS, A_HEAD_DIM), q_norm_a[l])
        ka = rmsnorm(heads(ka, A_HEADS, A_HEAD_DIM), k_norm_a[l])
        va = heads(va, A_HEADS, A_HEAD_DIM)
        qb = rmsnorm(heads(qb, B_Q_HEADS, B_HEAD_DIM), q_norm_b[l])
        kb = rmsnorm(heads(kb, B_KV_HEADS, B_HEAD_DIM), k_norm_b[l])
        vb = heads(vb, B_KV_HEADS, B_HEAD_DIM)
        ya = moba_attention(qa, ka, va, table_a).transpose(0, 2, 1, 3).reshape(B_, S, A_WIDTH)
        yb = swa_sink_attention(qb, kb, vb, sinks[l], table_b).transpose(0, 2, 1, 3).reshape(B_, S, B_WIDTH)
        merged = (jax.nn.sigmoid(ga) * (ya @ w_branch_a[l].astype(ya.dtype))
                  + jax.nn.sigmoid(gb) * (yb @ w_branch_b[l].astype(yb.dtype)))
        x = x + merged @ w_out[l].astype(merged.dtype)
        h2 = rmsnorm(x, norm2_g[l])
        g, u = jnp.split(h2 @ w_gate_up[l].astype(h2.dtype), 2, axis=-1)
        x = x + (jax.nn.silu(g) * u) @ w_down[l].astype(h2.dtype)
    return x
```

```python
import functools
import math

import jax
import jax.numpy as jnp
import numpy as np
from jax import lax
from jax.experimental import pallas as pl
from jax.experimental.pallas import tpu as pltpu

F32 = jnp.float32
BF16 = jnp.bfloat16

D_MODEL = 2048
A_HEADS = 8
A_HEAD_DIM = 128
A_WIDTH = A_HEADS * A_HEAD_DIM
MOBA_BLOCK = 256
MOBA_TOPK = 3
B_Q_HEADS = 16
B_KV_HEADS = 2
B_HEAD_DIM = 64
B_WIDTH = B_Q_HEADS * B_HEAD_DIM
B_KV_WIDTH = B_KV_HEADS * B_HEAD_DIM
WINDOW = 128
NUM_BUCKETS = 32
MAX_DISTANCE = 128
MAX_EXACT = NUM_BUCKETS // 2
D_FF = -(-(8 * D_MODEL) // (3 * 256)) * 256
IN_WIDTH = 3 * A_WIDTH + B_WIDTH + 2 * B_KV_WIDTH + 2 * D_MODEL
EPS = 1e-6
NEG = -1e30

LANES = 128
VMEM_LIMIT_BYTES = 56 * 1024 * 1024

IN_TN = 512
IN_QKV_TILES = (3 * A_WIDTH + B_WIDTH) // IN_TN
IN_GATE_TILES = (2 * D_MODEL) // IN_TN
IN_TILES = IN_QKV_TILES + IN_GATE_TILES + 1
IN_TM = 1024
KVX_WIDTH = 8 * LANES

SWA_TQ = 512
MERGE_TM = 512
MERGE_CH = 512
FFN_TM = 512
FFN_TF = 512


def _dot(a, b):
    return jnp.dot(a, b, preferred_element_type=F32)


def _dot_nt(a, b):
    return lax.dot_general(a, b, (((1,), (1,)), ((), ())), preferred_element_type=F32)


def _t5_bucket_np(dist):
    n = np.maximum(dist, 0)
    nf = np.maximum(n, 1).astype(np.float32)
    large = MAX_EXACT + (np.log(nf / np.float32(MAX_EXACT))
                         / np.float32(math.log(MAX_DISTANCE / MAX_EXACT))
                         * np.float32(NUM_BUCKETS - MAX_EXACT)).astype(np.int32)
    large = np.minimum(large, NUM_BUCKETS - 1)
    return np.where(n < MAX_EXACT, n, large).astype(np.int32)


def _bias_tile_kernel(tab_ref, bucket_ref, valid_ref, o_ref, *, head0, shift_bucket):
    h = pl.program_id(0) + head0
    n_tiles = bucket_ref.shape[0]
    shift = tab_ref[shift_bucket, h] if shift_bucket is not None else 0.0
    for t in range(n_tiles):
        bm = bucket_ref[t]
        acc = jnp.zeros(bm.shape, F32)
        for b in range(NUM_BUCKETS):
            acc = jnp.where(bm == b, tab_ref[b, h], acc)
        o_ref[t, 0] = jnp.where(valid_ref[t] != 0, acc - shift, NEG)


def _bias_tiles(rel_bias, buckets, valid, n_heads, head0, shift_bucket):
    n_tiles, r, c = buckets.shape
    return pl.pallas_call(
        functools.partial(_bias_tile_kernel, head0=head0, shift_bucket=shift_bucket),
        grid=(n_heads,),
        in_specs=[
            pl.BlockSpec(memory_space=pltpu.SMEM),
            pl.BlockSpec((n_tiles, r, c), lambda h: (0, 0, 0)),
            pl.BlockSpec((n_tiles, r, c), lambda h: (0, 0, 0)),
        ],
        out_specs=pl.BlockSpec((n_tiles, 1, r, c), lambda h: (0, h, 0, 0)),
        out_shape=jax.ShapeDtypeStruct((n_tiles, n_heads, r, c), F32),
        name="bias_tiles",
    )(rel_bias, buckets, valid)


def _inproj_kernel(x_ref, g1_ref, w_ref, cg_ref, bd128_ref, bd64_ref,
                   qkv_ref, gates_ref, kvx_ref, h_ref):
    j = pl.program_id(1)

    @pl.when(j == 0)
    def _():
        x = x_ref[...]
        ms = jnp.mean(x * x, axis=-1, keepdims=True)
        h_ref[...] = (x * lax.rsqrt(ms + EPS) * g1_ref[...]).astype(BF16)

    y = _dot(h_ref[...], w_ref[...])
    cg = cg_ref[pl.ds(j, 1), :]

    def head_norm(v, bd, dh, gain):
        ss = _dot((v * v).astype(BF16), bd)
        return v * lax.rsqrt(ss * (1.0 / dh) + EPS) * gain

    n_a = 2 * A_WIDTH // IN_TN
    n_va = 3 * A_WIDTH // IN_TN

    @pl.when(j < n_a)
    def _():
        qkv_ref[...] = head_norm(y, bd128_ref[...], A_HEAD_DIM, cg).astype(BF16)

    @pl.when((j >= n_a) & (j < n_va))
    def _():
        qkv_ref[...] = y.astype(BF16)

    @pl.when((j >= n_va) & (j < IN_QKV_TILES))
    def _():
        qkv_ref[...] = head_norm(y, bd64_ref[...], B_HEAD_DIM, cg).astype(BF16)

    @pl.when((j >= IN_QKV_TILES) & (j < IN_QKV_TILES + IN_GATE_TILES))
    def _():
        gates_ref[...] = jax.nn.sigmoid(y).astype(BF16)

    @pl.when(j == IN_TILES - 1)
    def _():
        kn = head_norm(y[:, :LANES], bd64_ref[:LANES, :LANES], B_HEAD_DIM, cg[:, :LANES])
        vv = y[:, LANES:2 * LANES]
        lo = lax.broadcasted_iota(jnp.int32, kn.shape, 1) < B_HEAD_DIM
        for base, t in ((0, kn), (4, vv)):
            e0 = jnp.where(lo, t, 0.0)
            o1 = jnp.where(lo, 0.0, t)
            o0 = pltpu.roll(e0, B_HEAD_DIM, 1)
            e1 = pltpu.roll(o1, B_HEAD_DIM, 1)
            for k, piece in enumerate((e0, o0, e1, o1)):
                kvx_ref[:, (base + k) * LANES:(base + k + 1) * LANES] = piece.astype(BF16)


def _inproj(x2, g1, w_perm, colgain, bd128, bd64):
    n = x2.shape[0]
    grid = (n // IN_TM, IN_TILES)
    return pl.pallas_call(
        _inproj_kernel,
        grid=grid,
        in_specs=[
            pl.BlockSpec((IN_TM, D_MODEL), lambda i, j: (i, 0)),
            pl.BlockSpec((1, D_MODEL), lambda i, j: (0, 0)),
            pl.BlockSpec((D_MODEL, IN_TN), lambda i, j: (0, j)),
            pl.BlockSpec(colgain.shape, lambda i, j: (0, 0)),
            pl.BlockSpec((IN_TN, IN_TN), lambda i, j: (0, 0)),
            pl.BlockSpec((IN_TN, IN_TN), lambda i, j: (0, 0)),
        ],
        out_specs=[
            pl.BlockSpec((IN_TM, IN_TN), lambda i, j: (i, jnp.minimum(j, IN_QKV_TILES - 1))),
            pl.BlockSpec((IN_TM, IN_TN),
                         lambda i, j: (i, jnp.clip(j - IN_QKV_TILES, 0, IN_GATE_TILES - 1))),
            pl.BlockSpec((IN_TM, KVX_WIDTH), lambda i, j: (i, 0)),
        ],
        out_shape=[
            jax.ShapeDtypeStruct((n, IN_QKV_TILES * IN_TN), BF16),
            jax.ShapeDtypeStruct((n, IN_GATE_TILES * IN_TN), BF16),
            jax.ShapeDtypeStruct((n, KVX_WIDTH), BF16),
        ],
        scratch_shapes=[pltpu.VMEM((IN_TM, D_MODEL), BF16)],
        compiler_params=pltpu.CompilerParams(
            dimension_semantics=("arbitrary", "arbitrary"),
            vmem_limit_bytes=VMEM_LIMIT_BYTES),
        name="inproj",
    )(x2, g1, w_perm, colgain, bd128, bd64)


def _moba_kernel(q_ref, k_ref, v_ref, bias_ref, o_ref,
                 kmh_ref, kml_ref, qa_ref, m_ref, l_ref, acc_ref):
    i = pl.program_id(2)
    L = MOBA_BLOCK
    seq = k_ref.shape[1]

    @pl.when(i == 0)
    def _():
        r = lax.broadcasted_iota(jnp.int32, (LANES, seq), 0)
        c = lax.broadcasted_iota(jnp.int32, (LANES, seq), 1)
        ind = jnp.where(lax.shift_right_logical(c, int(math.log2(L))) == r, 1.0, 0.0)
        km = _dot(ind.astype(BF16), k_ref[0]) * (1.0 / L)
        hi = km.astype(BF16)
        kmh_ref[...] = hi
        kml_ref[...] = (km - hi.astype(F32)).astype(BF16)

    q = q_ref[0]
    lane = lax.broadcasted_iota(jnp.int32, (L, LANES), 1)
    lane_f = lane.astype(F32)
    past = lane < i

    g = _dot_nt(q, kmh_ref[...]) + _dot_nt(q, kml_ref[...])
    g = jnp.where(past, g, NEG)
    sel_bias = jnp.full((L, LANES), NEG, F32)
    for _ in range(MOBA_TOPK):
        mx = jnp.max(g, axis=1, keepdims=True)
        first = jnp.min(jnp.where(g == mx, lane_f, float(LANES)), axis=1, keepdims=True)
        hit = lane_f == first
        sel_bias = jnp.where(hit & past, 0.0, sel_bias)
        g = jnp.where(hit, -jnp.inf, g)
    qa_ref[:, :A_HEAD_DIM] = q
    qa_ref[:, A_HEAD_DIM:] = sel_bias.astype(BF16)

    row0 = pl.multiple_of(i * L, L)
    s = _dot_nt(q, k_ref[0, pl.ds(row0, L), :]) + bias_ref[0, 0]
    m0 = jnp.max(s, axis=1, keepdims=True)
    p = jnp.exp(s - m0)
    m_ref[...] = m0
    l_ref[...] = jnp.sum(p, axis=1, keepdims=True)
    acc_ref[...] = _dot(p.astype(BF16), v_ref[0, pl.ds(row0, L), :])

    def block_scores(jb):
        r0 = pl.multiple_of(jb * L, L)
        onehot = jnp.where(lane == jb, 1.0, 0.0).astype(BF16)
        k_aug = jnp.concatenate([k_ref[0, pl.ds(r0, L), :], onehot], axis=1)
        return _dot_nt(qa_ref[...], k_aug), r0

    def update(s, r0):
        m_old = m_ref[...]
        m_new = jnp.maximum(m_old, jnp.max(s, axis=1, keepdims=True))
        alpha = jnp.exp(m_old - m_new)
        p = jnp.exp(s - m_new)
        l_ref[...] = alpha * l_ref[...] + jnp.sum(p, axis=1, keepdims=True)
        acc_ref[...] = alpha * acc_ref[...] + _dot(p.astype(BF16), v_ref[0, pl.ds(r0, L), :])
        m_ref[...] = m_new

    @pl.when(i >= 1)
    def _():
        s, r0 = block_scores(i - 1)
        update(s + bias_ref[1, 0], r0)

    def far_body(jb, carry):
        s, r0 = block_scores(jb)
        update(s, r0)
        return carry

    lax.fori_loop(0, jnp.maximum(i - 1, 0), far_body, 0)

    o_ref[0] = (acc_ref[...] / l_ref[...]).astype(BF16)


def _moba(qkv3, bias_a):
    b, s, _ = qkv3.shape
    L = MOBA_BLOCK
    nq = s // L
    return pl.pallas_call(
        _moba_kernel,
        grid=(b, A_HEADS, nq),
        in_specs=[
            pl.BlockSpec((1, L, A_HEAD_DIM), lambda bi, h, i: (bi, i, h)),
            pl.BlockSpec((1, s, A_HEAD_DIM), lambda bi, h, i: (bi, 0, A_HEADS + h)),
            pl.BlockSpec((1, s, A_HEAD_DIM), lambda bi, h, i: (bi, 0, 2 * A_HEADS + h)),
            pl.BlockSpec((2, 1, L, L), lambda bi, h, i: (0, h, 0, 0)),
        ],
        out_specs=pl.BlockSpec((1, L, A_HEAD_DIM), lambda bi, h, i: (bi, i, h)),
        out_shape=jax.ShapeDtypeStruct((b, s, A_WIDTH), BF16),
        scratch_shapes=[
            pltpu.VMEM((LANES, A_HEAD_DIM), BF16),
            pltpu.VMEM((LANES, A_HEAD_DIM), BF16),
            pltpu.VMEM((L, A_HEAD_DIM + LANES), BF16),
            pltpu.VMEM((L, 1), F32),
            pltpu.VMEM((L, 1), F32),
            pltpu.VMEM((L, A_HEAD_DIM), F32),
        ],
        compiler_params=pltpu.CompilerParams(
            dimension_semantics=("arbitrary", "arbitrary", "arbitrary"),
            vmem_limit_bytes=VMEM_LIMIT_BYTES),
        name="moba",
    )(qkv3, qkv3, qkv3, bias_a)


def _swa_kernel(q_ref, kvx_ref, bias_ref, sink_ref, o_ref):
    t = pl.program_id(1)
    W = WINDOW
    n_win = q_ref.shape[1] // W
    pairs = B_Q_HEADS // 2
    pairs_per_kv = pairs // B_KV_HEADS

    def win(w, carry):
        n = t * n_win + w
        q0 = pl.multiple_of(w * W, W)
        start = pl.multiple_of(jnp.maximum(n - 1, 0) * W, W)
        first = (n == 0).astype(jnp.int32)
        for p in range(pairs):
            kv = p // pairs_per_kv
            q2 = q_ref[0, pl.ds(q0, W), p * LANES:(p + 1) * LANES]
            out = None
            for par in range(2):
                h = 2 * p + par
                kcol = (2 * kv + par) * LANES
                vcol = (4 + 2 * kv + par) * LANES
                kk = kvx_ref[0, pl.ds(start, 2 * W), kcol:kcol + LANES]
                vv = kvx_ref[0, pl.ds(start, 2 * W), vcol:vcol + LANES]
                s = _dot_nt(q2, kk) + bias_ref[first, h]
                sink = sink_ref[h]
                m = jnp.maximum(jnp.max(s, axis=1, keepdims=True), sink)
                pe = jnp.exp(s - m)
                l = jnp.sum(pe, axis=1, keepdims=True) + jnp.exp(sink - m)
                o = _dot(pe.astype(BF16), vv) / l
                out = o if out is None else out + o
            o_ref[0, pl.ds(q0, W), p * LANES:(p + 1) * LANES] = out.astype(BF16)
        return carry

    lax.fori_loop(0, n_win, win, 0)


def _swa(qkv3, kvx3, bias_b, sinks):
    b, s, _ = qkv3.shape
    qb_blk = 3 * A_WIDTH // B_WIDTH
    return pl.pallas_call(
        _swa_kernel,
        grid=(b, s // SWA_TQ),
        in_specs=[
            pl.BlockSpec((1, SWA_TQ, B_WIDTH), lambda bi, t: (bi, t, qb_blk)),
            pl.BlockSpec((1, s, KVX_WIDTH), lambda bi, t: (bi, 0, 0)),
            pl.BlockSpec(bias_b.shape, lambda bi, t: (0, 0, 0, 0)),
            pl.BlockSpec(memory_space=pltpu.SMEM),
        ],
        out_specs=pl.BlockSpec((1, SWA_TQ, B_WIDTH), lambda bi, t: (bi, t, 0)),
        out_shape=jax.ShapeDtypeStruct((b, s, B_WIDTH), BF16),
        compiler_params=pltpu.CompilerParams(
            dimension_semantics=("arbitrary", "arbitrary"),
            vmem_limit_bytes=VMEM_LIMIT_BYTES),
        name="swa",
    )(qkv3, kvx3, bias_b, sinks)


def _merge_kernel(ya_ref, yb_ref, gt_ref, x_ref, wa_ref, wb_ref, wo_ref, o_ref, mg_ref):
    ch = MERGE_CH
    for c in range(D_MODEL // ch):
        cols = slice(c * ch, (c + 1) * ch)
        ta = _dot(ya_ref[...], wa_ref[:, cols])
        tb = _dot(yb_ref[...], wb_ref[:, cols])
        ga = gt_ref[:, cols].astype(F32)
        gb = gt_ref[:, D_MODEL + c * ch:D_MODEL + (c + 1) * ch].astype(F32)
        mg_ref[:, cols] = (ga * ta + gb * tb).astype(BF16)
    for c in range(D_MODEL // ch):
        cols = slice(c * ch, (c + 1) * ch)
        o_ref[:, cols] = x_ref[:, cols] + _dot(mg_ref[...], wo_ref[:, cols])


def _merge(ya, yb, gates, x2, wa, wb, wo):
    n = x2.shape[0]
    tm = MERGE_TM
    resident = functools.partial(pl.BlockSpec, pipeline_mode=pl.Buffered(1))
    return pl.pallas_call(
        _merge_kernel,
        grid=(n // tm,),
        in_specs=[
            pl.BlockSpec((tm, A_WIDTH), lambda i: (i, 0)),
            pl.BlockSpec((tm, B_WIDTH), lambda i: (i, 0)),
            pl.BlockSpec((tm, 2 * D_MODEL), lambda i: (i, 0)),
            pl.BlockSpec((tm, D_MODEL), lambda i: (i, 0)),
            resident((A_WIDTH, D_MODEL), lambda i: (0, 0)),
            resident((B_WIDTH, D_MODEL), lambda i: (0, 0)),
            resident((D_MODEL, D_MODEL), lambda i: (0, 0)),
        ],
        out_specs=pl.BlockSpec((tm, D_MODEL), lambda i: (i, 0)),
        out_shape=jax.ShapeDtypeStruct((n, D_MODEL), F32),
        scratch_shapes=[pltpu.VMEM((tm, D_MODEL), BF16)],
        compiler_params=pltpu.CompilerParams(
            dimension_semantics=("arbitrary",),
            vmem_limit_bytes=VMEM_LIMIT_BYTES),
        name="merge",
    )(ya, yb, gates, x2, wa, wb, wo)


def _ffn_kernel(x_ref, g2_ref, wg_ref, wu_ref, wd_ref, o_ref, h_ref):
    f = pl.program_id(1)

    @pl.when(f == 0)
    def _():
        x = x_ref[...]
        ms = jnp.mean(x * x, axis=-1, keepdims=True)
        h_ref[...] = (x * lax.rsqrt(ms + EPS) * g2_ref[...]).astype(BF16)
        o_ref[...] = x

    h = h_ref[...]
    g = _dot(h, wg_ref[...])
    u = _dot(h, wu_ref[...])
    a = (g * jax.nn.sigmoid(g) * u).astype(BF16)
    o_ref[...] += _dot(a, wd_ref[...])


def _ffn(x1, g2, wgu, wd):
    n = x1.shape[0]
    tm, tf = FFN_TM, FFN_TF
    nf = D_FF // tf
    return pl.pallas_call(
        _ffn_kernel,
        grid=(n // tm, nf),
        in_specs=[
            pl.BlockSpec((tm, D_MODEL), lambda i, f: (i, 0)),
            pl.BlockSpec((1, D_MODEL), lambda i, f: (0, 0)),
            pl.BlockSpec((D_MODEL, tf), lambda i, f: (0, f)),
            pl.BlockSpec((D_MODEL, tf), lambda i, f: (0, nf + f)),
            pl.BlockSpec((tf, D_MODEL), lambda i, f: (f, 0)),
        ],
        out_specs=pl.BlockSpec((tm, D_MODEL), lambda i, f: (i, 0)),
        out_shape=jax.ShapeDtypeStruct((n, D_MODEL), F32),
        scratch_shapes=[pltpu.VMEM((tm, D_MODEL), BF16)],
        compiler_params=pltpu.CompilerParams(
            dimension_semantics=("arbitrary", "arbitrary"),
            vmem_limit_bytes=VMEM_LIMIT_BYTES),
        name="ffn",
    )(x1, g2, wgu, wgu, wd)


def _moba_bias_tables():
    L = MOBA_BLOCK
    qi = np.arange(L)[:, None]
    kj = np.arange(L)[None, :]
    d_own = qi - kj
    d_prev = L + qi - kj
    buckets = np.stack([_t5_bucket_np(d_own), _t5_bucket_np(d_prev)])
    valid = np.stack([d_own >= 0, np.ones_like(d_prev, bool)]).astype(np.int32)
    return buckets, valid


def _swa_bias_tables():
    W = WINDOW
    qi = np.arange(W)[:, None]
    c = np.arange(2 * W)[None, :]
    d = W + qi - c
    v = (d >= 0) & (d < W)
    d0 = qi - c
    v0 = (d0 >= 0) & (d0 < W)
    buckets = np.stack([_t5_bucket_np(d), _t5_bucket_np(d0)])
    valid = np.stack([v, v0]).astype(np.int32)
    return buckets, valid


def _block_diag_ones(n, blk):
    r = np.arange(n)
    return (r[:, None] // blk == r[None, :] // blk).astype(np.float32)


def kernel(x, norm1_g, w_in, q_norm_a, k_norm_a, q_norm_b, k_norm_b, rel_bias, sinks,
           w_branch_a, w_branch_b, w_out, norm2_g, w_gate_up, w_down):
    b, s, d = x.shape
    depth = w_in.shape[0]
    assert d == D_MODEL and s % MOBA_BLOCK == 0 and s % SWA_TQ == 0
    assert (b * s) % IN_TM == 0 and (b * s) % MERGE_TM == 0 and (b * s) % FFN_TM == 0
    far = _t5_bucket_np(np.arange(MOBA_BLOCK + 1, s + MOBA_BLOCK))
    assert (far == far[0]).all()
    far_bucket = int(far[0])

    bk_a, va_a = _moba_bias_tables()
    bk_b, va_b = _swa_bias_tables()
    bias_a = _bias_tiles(rel_bias, jnp.asarray(bk_a), jnp.asarray(va_a), A_HEADS, 0, far_bucket)
    bias_b = _bias_tiles(rel_bias, jnp.asarray(bk_b), jnp.asarray(va_b), B_Q_HEADS, A_HEADS, None)
    bd128 = jnp.asarray(_block_diag_ones(IN_TN, A_HEAD_DIM), BF16)
    bd64 = jnp.asarray(_block_diag_ones(IN_TN, B_HEAD_DIM), BF16)

    o_qb = 3 * A_WIDTH
    o_kb = o_qb + B_WIDTH
    o_g = o_kb + 2 * B_KV_WIDTH
    x2 = x.reshape(b * s, d)
    for l in range(depth):
        w = w_in[l]
        w_perm = jnp.concatenate(
            [w[:, :o_kb], w[:, o_g:], w[:, o_kb:o_g],
             jnp.zeros((d, IN_TN - 2 * B_KV_WIDTH), w.dtype)], axis=1).astype(BF16)
        ones = jnp.ones((IN_TN,), F32)
        colgain = jnp.stack(
            [jnp.tile(q_norm_a[l] * (A_HEAD_DIM ** -0.5), IN_TN // A_HEAD_DIM)] * 2
            + [jnp.tile(k_norm_a[l], IN_TN // A_HEAD_DIM)] * 2
            + [ones] * 2
            + [jnp.tile(q_norm_b[l] * (B_HEAD_DIM ** -0.5), IN_TN // B_HEAD_DIM)] * 2
            + [ones] * IN_GATE_TILES
            + [jnp.concatenate([jnp.tile(k_norm_b[l], B_KV_HEADS),
                                jnp.ones((IN_TN - B_KV_WIDTH,), F32)])]
            + [ones] * (24 - IN_TILES))
        qkv, gates, kvx = _inproj(x2, norm1_g[l][None, :], w_perm, colgain, bd128, bd64)
        qkv3 = qkv.reshape(b, s, -1)
        ya = _moba(qkv3, bias_a)
        yb = _swa(qkv3, kvx.reshape(b, s, -1), bias_b, sinks[l])
        x1 = _merge(ya.reshape(b * s, -1), yb.reshape(b * s, -1), gates, x2,
                    w_branch_a[l].astype(BF16), w_branch_b[l].astype(BF16),
                    w_out[l].astype(BF16))
        x2 = _ffn(x1, norm2_g[l][None, :], w_gate_up[l].astype(BF16), w_down[l].astype(BF16))
    return x2.reshape(b, s, d)
```

```python
import functools
import math

import jax
import jax.numpy as jnp
import numpy as np
from jax import lax
from jax.experimental import pallas as pl
from jax.experimental.pallas import tpu as pltpu

F32 = jnp.float32
BF16 = jnp.bfloat16

D_MODEL = 2048
A_HEADS = 8
A_HEAD_DIM = 128
A_WIDTH = A_HEADS * A_HEAD_DIM
MOBA_BLOCK = 256
MOBA_TOPK = 3
B_Q_HEADS = 16
B_KV_HEADS = 2
B_HEAD_DIM = 64
B_WIDTH = B_Q_HEADS * B_HEAD_DIM
B_KV_WIDTH = B_KV_HEADS * B_HEAD_DIM
WINDOW = 128
NUM_BUCKETS = 32
MAX_DISTANCE = 128
MAX_EXACT = NUM_BUCKETS // 2
D_FF = -(-(8 * D_MODEL) // (3 * 256)) * 256
IN_WIDTH = 3 * A_WIDTH + B_WIDTH + 2 * B_KV_WIDTH + 2 * D_MODEL
EPS = 1e-6
NEG = -1e30
LOG2E = math.log2(math.e)

LANES = 128
VMEM_LIMIT_BYTES = 56 * 1024 * 1024

IN_TN = 512
IN_QKV_TILES = (3 * A_WIDTH + B_WIDTH) // IN_TN
IN_GATE_TILES = (2 * D_MODEL) // IN_TN
IN_TILES = IN_QKV_TILES + IN_GATE_TILES + 1
IN_TM = 1024
KVX_WIDTH = 8 * LANES

MOBA_PAIR = 2 * MOBA_BLOCK
SWA_TQ = 512
MERGE_TM = 512
MERGE_CH = 512
FFN_TM = 512
FFN_TF = 512


def _dot(a, b):
    return jnp.dot(a, b, preferred_element_type=F32)


def _dot_nt(a, b):
    return lax.dot_general(a, b, (((1,), (1,)), ((), ())), preferred_element_type=F32)


def _t5_bucket_np(dist):
    n = np.maximum(dist, 0)
    nf = np.maximum(n, 1).astype(np.float32)
    large = MAX_EXACT + (np.log(nf / np.float32(MAX_EXACT))
                         / np.float32(math.log(MAX_DISTANCE / MAX_EXACT))
                         * np.float32(NUM_BUCKETS - MAX_EXACT)).astype(np.int32)
    large = np.minimum(large, NUM_BUCKETS - 1)
    return np.where(n < MAX_EXACT, n, large).astype(np.int32)


def _bucket_lookup(tab_ref, bm, h):
    acc = jnp.zeros(bm.shape, F32)
    for b in range(NUM_BUCKETS):
        acc = jnp.where(bm == b, tab_ref[b, h], acc)
    return acc


def _swa_bias_kernel(tab_ref, bucket_ref, valid_ref, o_ref):
    h = pl.program_id(0) + A_HEADS
    for t in range(bucket_ref.shape[0]):
        o_ref[t, 0] = jnp.where(valid_ref[t] != 0, _bucket_lookup(tab_ref, bucket_ref[t], h), NEG)


def _swa_bias(rel_bias, buckets, valid):
    n_tiles, r, c = buckets.shape
    return pl.pallas_call(
        _swa_bias_kernel,
        grid=(B_Q_HEADS,),
        in_specs=[
            pl.BlockSpec(memory_space=pltpu.SMEM),
            pl.BlockSpec((n_tiles, r, c), lambda h: (0, 0, 0)),
            pl.BlockSpec((n_tiles, r, c), lambda h: (0, 0, 0)),
        ],
        out_specs=pl.BlockSpec((n_tiles, 1, r, c), lambda h: (0, h, 0, 0)),
        out_shape=jax.ShapeDtypeStruct((n_tiles, B_Q_HEADS, r, c), F32),
        name="swa_bias",
    )(rel_bias, buckets, valid)


def _moba_bias_kernel(tab_ref, bucket_ref, o_ref, *, shift_bucket):
    h = pl.program_id(0)
    L = MOBA_BLOCK
    shift = tab_ref[shift_bucket, h]
    tiles = [(_bucket_lookup(tab_ref, bucket_ref[t], h) - shift) * LOG2E for t in range(2)]
    kk = lax.broadcasted_iota(jnp.int32, (L, L), 0)
    qq = lax.broadcasted_iota(jnp.int32, (L, L), 1)
    own = jnp.where(qq >= kk, tiles[0], NEG)
    o_ref[0, :L, :L] = own
    o_ref[0, L:, L:] = own
    o_ref[0, :L, L:] = tiles[1]
    o_ref[0, L:, :L] = jnp.full((L, L), NEG, F32)


def _moba_bias(rel_bias, buckets, shift_bucket):
    L = MOBA_BLOCK
    return pl.pallas_call(
        functools.partial(_moba_bias_kernel, shift_bucket=shift_bucket),
        grid=(A_HEADS,),
        in_specs=[
            pl.BlockSpec(memory_space=pltpu.SMEM),
            pl.BlockSpec((2, L, L), lambda h: (0, 0, 0)),
        ],
        out_specs=pl.BlockSpec((1, 2 * L, 2 * L), lambda h: (h, 0, 0)),
        out_shape=jax.ShapeDtypeStruct((A_HEADS, 2 * L, 2 * L), F32),
        name="moba_bias",
    )(rel_bias, buckets)


def _inproj_kernel(x_ref, g1_ref, w_ref, cg_ref, bd128_ref, bd64_ref,
                   qkv_ref, gates_ref, kvx_ref, h_ref):
    j = pl.program_id(1)

    @pl.when(j == 0)
    def _():
        x = x_ref[...]
        ms = jnp.mean(x * x, axis=-1, keepdims=True)
        h_ref[...] = (x * lax.rsqrt(ms + EPS) * g1_ref[...]).astype(BF16)

    y = _dot(h_ref[...], w_ref[...])
    cg = cg_ref[pl.ds(j, 1), :]

    def head_norm(v, bd, dh, gain):
        ss = _dot((v * v).astype(BF16), bd)
        return v * lax.rsqrt(ss * (1.0 / dh) + EPS) * gain

    n_a = 2 * A_WIDTH // IN_TN
    n_va = 3 * A_WIDTH // IN_TN

    @pl.when(j < n_a)
    def _():
        qkv_ref[...] = head_norm(y, bd128_ref[...], A_HEAD_DIM, cg).astype(BF16)

    @pl.when((j >= n_a) & (j < n_va))
    def _():
        qkv_ref[...] = y.astype(BF16)

    @pl.when((j >= n_va) & (j < IN_QKV_TILES))
    def _():
        qkv_ref[...] = head_norm(y, bd64_ref[...], B_HEAD_DIM, cg).astype(BF16)

    @pl.when((j >= IN_QKV_TILES) & (j < IN_QKV_TILES + IN_GATE_TILES))
    def _():
        gates_ref[...] = jax.nn.sigmoid(y).astype(BF16)

    @pl.when(j == IN_TILES - 1)
    def _():
        kn = head_norm(y[:, :LANES], bd64_ref[:LANES, :LANES], B_HEAD_DIM, cg[:, :LANES])
        vv = y[:, LANES:2 * LANES]
        lo = lax.broadcasted_iota(jnp.int32, kn.shape, 1) < B_HEAD_DIM
        for base, t in ((0, kn), (4, vv)):
            e0 = jnp.where(lo, t, 0.0)
            o1 = jnp.where(lo, 0.0, t)
            o0 = pltpu.roll(e0, B_HEAD_DIM, 1)
            e1 = pltpu.roll(o1, B_HEAD_DIM, 1)
            for k, piece in enumerate((e0, o0, e1, o1)):
                kvx_ref[:, (base + k) * LANES:(base + k + 1) * LANES] = piece.astype(BF16)


def _inproj(x2, g1, w_perm, colgain, bd128, bd64):
    n = x2.shape[0]
    grid = (n // IN_TM, IN_TILES)
    return pl.pallas_call(
        _inproj_kernel,
        grid=grid,
        in_specs=[
            pl.BlockSpec((IN_TM, D_MODEL), lambda i, j: (i, 0)),
            pl.BlockSpec((1, D_MODEL), lambda i, j: (0, 0)),
            pl.BlockSpec((D_MODEL, IN_TN), lambda i, j: (0, j)),
            pl.BlockSpec(colgain.shape, lambda i, j: (0, 0)),
            pl.BlockSpec((IN_TN, IN_TN), lambda i, j: (0, 0)),
            pl.BlockSpec((IN_TN, IN_TN), lambda i, j: (0, 0)),
        ],
        out_specs=[
            pl.BlockSpec((IN_TM, IN_TN), lambda i, j: (i, jnp.minimum(j, IN_QKV_TILES - 1))),
            pl.BlockSpec((IN_TM, IN_TN),
                         lambda i, j: (i, jnp.clip(j - IN_QKV_TILES, 0, IN_GATE_TILES - 1))),
            pl.BlockSpec((IN_TM, KVX_WIDTH), lambda i, j: (i, 0)),
        ],
        out_shape=[
            jax.ShapeDtypeStruct((n, IN_QKV_TILES * IN_TN), BF16),
            jax.ShapeDtypeStruct((n, IN_GATE_TILES * IN_TN), BF16),
            jax.ShapeDtypeStruct((n, KVX_WIDTH), BF16),
        ],
        scratch_shapes=[pltpu.VMEM((IN_TM, D_MODEL), BF16)],
        compiler_params=pltpu.CompilerParams(
            dimension_semantics=("arbitrary", "arbitrary"),
            vmem_limit_bytes=VMEM_LIMIT_BYTES),
        name="inproj",
    )(x2, g1, w_perm, colgain, bd128, bd64)


def _moba_kernel(q_ref, k_ref, v_ref, bd_ref, bc_ref, o_ref,
                 kmh_ref, kml_ref, vt_ref, qa_ref, s_ref, m_ref, l_ref, acc_ref):
    t = pl.program_id(2)
    L = MOBA_BLOCK
    P = MOBA_PAIR
    seq = k_ref.shape[1]

    @pl.when(t == 0)
    def _():
        r = lax.broadcasted_iota(jnp.int32, (LANES, seq), 0)
        c = lax.broadcasted_iota(jnp.int32, (LANES, seq), 1)
        ind = jnp.where(lax.shift_right_logical(c, int(math.log2(L))) == r, 1.0, 0.0)
        km = _dot(ind.astype(BF16), k_ref[0]) * (1.0 / L)
        hi = km.astype(BF16)
        kmh_ref[...] = hi
        kml_ref[...] = (km - hi.astype(F32)).astype(BF16)
        er = lax.broadcasted_iota(jnp.int32, (A_HEAD_DIM, A_HEAD_DIM), 0)
        ec = lax.broadcasted_iota(jnp.int32, (A_HEAD_DIM, A_HEAD_DIM), 1)
        eye = jnp.where(er == ec, 1.0, 0.0).astype(BF16)
        for g in range(seq // P):
            vt_ref[g] = _dot_nt(eye, v_ref[0, g * P:(g + 1) * P, :]).astype(BF16)

    q = q_ref[0]
    lane = lax.broadcasted_iota(jnp.int32, (P, LANES), 1)
    row = lax.broadcasted_iota(jnp.int32, (P, LANES), 0)
    lane_f = lane.astype(F32)
    own_blk = 2 * t + (row >= L).astype(jnp.int32)
    past = lane < own_blk

    g = _dot_nt(q, kmh_ref[...]) + _dot_nt(q, kml_ref[...])
    g = jnp.where(past, g, NEG)
    sel_bias = jnp.full((P, LANES), NEG, F32)
    for _ in range(MOBA_TOPK):
        mx = jnp.max(g, axis=1, keepdims=True)
        first = jnp.min(jnp.where(g == mx, lane_f, float(LANES)), axis=1, keepdims=True)
        hit = lane_f == first
        sel_bias = jnp.where(hit & past, 0.0, sel_bias)
        g = jnp.where(hit, -jnp.inf, g)
    sel_bias = jnp.where(lane == own_blk, 0.0, sel_bias)
    qa_ref[:, :A_HEAD_DIM] = q
    qa_ref[:, A_HEAD_DIM:] = sel_bias.astype(BF16)

    def scores(pair):
        r0 = pl.multiple_of(pair * P, P)
        blk = 2 * pair + (row >= L).astype(jnp.int32)
        ind = jnp.where(lane == blk, 1.0, 0.0).astype(BF16)
        k_aug = jnp.concatenate([k_ref[0, pl.ds(r0, P), :], ind], axis=1)
        return _dot_nt(k_aug, qa_ref[...])

    s = scores(t) + bd_ref[0]
    m0 = jnp.max(s, axis=0, keepdims=True)
    p = jnp.exp2(s - m0)
    m_ref[...] = m0
    l_ref[...] = jnp.sum(p, axis=0, keepdims=True)
    acc_ref[...] = _dot(vt_ref[t], p.astype(BF16))

    def accumulate(pair):
        s = s_ref[...]
        m_old = m_ref[...]
        m_new = jnp.maximum(m_old, jnp.max(s, axis=0, keepdims=True))
        alpha = jnp.exp2(m_old - m_new)
        p = jnp.exp2(s - m_new)
        l_ref[...] = alpha * l_ref[...] + jnp.sum(p, axis=0, keepdims=True)
        acc_ref[...] = alpha * acc_ref[...] + _dot(vt_ref[pair], p.astype(BF16))
        m_ref[...] = m_new

    @pl.when(t >= 1)
    def _():
        s_ref[...] = scores(t - 1)
        s_ref[P - LANES:, :LANES] += bc_ref[0]
        accumulate(t - 1)

    def far_body(pair, carry):
        s_ref[...] = scores(pair)
        accumulate(pair)
        return carry

    lax.fori_loop(0, jnp.maximum(t - 1, 0), far_body, 0)

    o_ref[0] = (acc_ref[...] / l_ref[...]).T.astype(BF16)


def _moba(qkv3, bias_a):
    b, s, _ = qkv3.shape
    P = MOBA_PAIR
    return pl.pallas_call(
        _moba_kernel,
        grid=(b, A_HEADS, s // P),
        in_specs=[
            pl.BlockSpec((1, P, A_HEAD_DIM), lambda bi, h, t: (bi, t, h)),
            pl.BlockSpec((1, s, A_HEAD_DIM), lambda bi, h, t: (bi, 0, A_HEADS + h)),
            pl.BlockSpec((1, s, A_HEAD_DIM), lambda bi, h, t: (bi, 0, 2 * A_HEADS + h)),
            pl.BlockSpec((1, P, P), lambda bi, h, t: (h, 0, 0)),
            pl.BlockSpec((1, LANES, LANES), lambda bi, h, t: (h, 1, 2)),
        ],
        out_specs=pl.BlockSpec((1, P, A_HEAD_DIM), lambda bi, h, t: (bi, t, h)),
        out_shape=jax.ShapeDtypeStruct((b, s, A_WIDTH), BF16),
        scratch_shapes=[
            pltpu.VMEM((LANES, A_HEAD_DIM), BF16),
            pltpu.VMEM((LANES, A_HEAD_DIM), BF16),
            pltpu.VMEM((s // P, A_HEAD_DIM, P), BF16),
            pltpu.VMEM((P, A_HEAD_DIM + LANES), BF16),
            pltpu.VMEM((P, P), F32),
            pltpu.VMEM((1, P), F32),
            pltpu.VMEM((1, P), F32),
            pltpu.VMEM((A_HEAD_DIM, P), F32),
        ],
        compiler_params=pltpu.CompilerParams(
            dimension_semantics=("arbitrary", "arbitrary", "arbitrary"),
            vmem_limit_bytes=VMEM_LIMIT_BYTES),
        name="moba",
    )(qkv3, qkv3, qkv3, bias_a, bias_a)


def _swa_kernel(q_ref, kvx_ref, bias_ref, sink_ref, o_ref):
    t = pl.program_id(1)
    W = WINDOW
    n_win = q_ref.shape[1] // W
    pairs = B_Q_HEADS // 2
    pairs_per_kv = pairs // B_KV_HEADS

    def win(w, carry):
        n = t * n_win + w
        q0 = pl.multiple_of(w * W, W)
        start = pl.multiple_of(jnp.maximum(n - 1, 0) * W, W)
        first = (n == 0).astype(jnp.int32)
        for p in range(pairs):
            kv = p // pairs_per_kv
            q2 = q_ref[0, pl.ds(q0, W), p * LANES:(p + 1) * LANES]
            out = None
            for par in range(2):
                h = 2 * p + par
                kcol = (2 * kv + par) * LANES
                vcol = (4 + 2 * kv + par) * LANES
                kk = kvx_ref[0, pl.ds(start, 2 * W), kcol:kcol + LANES]
                vv = kvx_ref[0, pl.ds(start, 2 * W), vcol:vcol + LANES]
                s = _dot_nt(q2, kk) + bias_ref[first, h]
                sink = sink_ref[h]
                m = jnp.maximum(jnp.max(s, axis=1, keepdims=True), sink)
                pe = jnp.exp(s - m)
                l = jnp.sum(pe, axis=1, keepdims=True) + jnp.exp(sink - m)
                o = _dot(pe.astype(BF16), vv) / l
                out = o if out is None else out + o
            o_ref[0, pl.ds(q0, W), p * LANES:(p + 1) * LANES] = out.astype(BF16)
        return carry

    lax.fori_loop(0, n_win, win, 0)


def _swa(qkv3, kvx3, bias_b, sinks):
    b, s, _ = qkv3.shape
    qb_blk = 3 * A_WIDTH // B_WIDTH
    return pl.pallas_call(
        _swa_kernel,
        grid=(b, s // SWA_TQ),
        in_specs=[
            pl.BlockSpec((1, SWA_TQ, B_WIDTH), lambda bi, t: (bi, t, qb_blk)),
            pl.BlockSpec((1, s, KVX_WIDTH), lambda bi, t: (bi, 0, 0)),
            pl.BlockSpec(bias_b.shape, lambda bi, t: (0, 0, 0, 0)),
            pl.BlockSpec(memory_space=pltpu.SMEM),
        ],
        out_specs=pl.BlockSpec((1, SWA_TQ, B_WIDTH), lambda bi, t: (bi, t, 0)),
        out_shape=jax.ShapeDtypeStruct((b, s, B_WIDTH), BF16),
        compiler_params=pltpu.CompilerParams(
            dimension_semantics=("arbitrary", "arbitrary"),
            vmem_limit_bytes=VMEM_LIMIT_BYTES),
        name="swa",
    )(qkv3, kvx3, bias_b, sinks)


def _merge_kernel(ya_ref, yb_ref, gt_ref, x_ref, wa_ref, wb_ref, wo_ref, o_ref, mg_ref):
    ch = MERGE_CH
    for c in range(D_MODEL // ch):
        cols = slice(c * ch, (c + 1) * ch)
        ta = _dot(ya_ref[...], wa_ref[:, cols])
        tb = _dot(yb_ref[...], wb_ref[:, cols])
        ga = gt_ref[:, cols].astype(F32)
        gb = gt_ref[:, D_MODEL + c * ch:D_MODEL + (c + 1) * ch].astype(F32)
        mg_ref[:, cols] = (ga * ta + gb * tb).astype(BF16)
    for c in range(D_MODEL // ch):
        cols = slice(c * ch, (c + 1) * ch)
        o_ref[:, cols] = x_ref[:, cols] + _dot(mg_ref[...], wo_ref[:, cols])


def _merge(ya, yb, gates, x2, wa, wb, wo):
    n = x2.shape[0]
    tm = MERGE_TM
    resident = functools.partial(pl.BlockSpec, pipeline_mode=pl.Buffered(1))
    return pl.pallas_call(
        _merge_kernel,
        grid=(n // tm,),
        in_specs=[
            pl.BlockSpec((tm, A_WIDTH), lambda i: (i, 0)),
            pl.BlockSpec((tm, B_WIDTH), lambda i: (i, 0)),
            pl.BlockSpec((tm, 2 * D_MODEL), lambda i: (i, 0)),
            pl.BlockSpec((tm, D_MODEL), lambda i: (i, 0)),
            resident((A_WIDTH, D_MODEL), lambda i: (0, 0)),
            resident((B_WIDTH, D_MODEL), lambda i: (0, 0)),
            resident((D_MODEL, D_MODEL), lambda i: (0, 0)),
        ],
        out_specs=pl.BlockSpec((tm, D_MODEL), lambda i: (i, 0)),
        out_shape=jax.ShapeDtypeStruct((n, D_MODEL), F32),
        scratch_shapes=[pltpu.VMEM((tm, D_MODEL), BF16)],
        compiler_params=pltpu.CompilerParams(
            dimension_semantics=("arbitrary",),
            vmem_limit_bytes=VMEM_LIMIT_BYTES),
        name="merge",
    )(ya, yb, gates, x2, wa, wb, wo)


def _ffn_kernel(x_ref, g2_ref, wg_ref, wu_ref, wd_ref, o_ref, h_ref):
    f = pl.program_id(1)

    @pl.when(f == 0)
    def _():
        x = x_ref[...]
        ms = jnp.mean(x * x, axis=-1, keepdims=True)
        h_ref[...] = (x * lax.rsqrt(ms + EPS) * g2_ref[...]).astype(BF16)
        o_ref[...] = x

    h = h_ref[...]
    g = _dot(h, wg_ref[...])
    u = _dot(h, wu_ref[...])
    a = (g * jax.nn.sigmoid(g) * u).astype(BF16)
    o_ref[...] += _dot(a, wd_ref[...])


def _ffn(x1, g2, wgu, wd):
    n = x1.shape[0]
    tm, tf = FFN_TM, FFN_TF
    nf = D_FF // tf
    return pl.pallas_call(
        _ffn_kernel,
        grid=(n // tm, nf),
        in_specs=[
            pl.BlockSpec((tm, D_MODEL), lambda i, f: (i, 0)),
            pl.BlockSpec((1, D_MODEL), lambda i, f: (0, 0)),
            pl.BlockSpec((D_MODEL, tf), lambda i, f: (0, f)),
            pl.BlockSpec((D_MODEL, tf), lambda i, f: (0, nf + f)),
            pl.BlockSpec((tf, D_MODEL), lambda i, f: (f, 0)),
        ],
        out_specs=pl.BlockSpec((tm, D_MODEL), lambda i, f: (i, 0)),
        out_shape=jax.ShapeDtypeStruct((n, D_MODEL), F32),
        scratch_shapes=[pltpu.VMEM((tm, D_MODEL), BF16)],
        compiler_params=pltpu.CompilerParams(
            dimension_semantics=("arbitrary", "arbitrary"),
            vmem_limit_bytes=VMEM_LIMIT_BYTES),
        name="ffn",
    )(x1, g2, wgu, wgu, wd)


def _moba_bucket_tables():
    L = MOBA_BLOCK
    kj = np.arange(L)[:, None]
    qi = np.arange(L)[None, :]
    return np.stack([_t5_bucket_np(qi - kj), _t5_bucket_np(L + qi - kj)])


def _swa_bias_tables():
    W = WINDOW
    qi = np.arange(W)[:, None]
    c = np.arange(2 * W)[None, :]
    d = W + qi - c
    v = (d >= 0) & (d < W)
    d0 = qi - c
    v0 = (d0 >= 0) & (d0 < W)
    buckets = np.stack([_t5_bucket_np(d), _t5_bucket_np(d0)])
    valid = np.stack([v, v0]).astype(np.int32)
    return buckets, valid


def _block_diag_ones(n, blk):
    r = np.arange(n)
    return (r[:, None] // blk == r[None, :] // blk).astype(np.float32)


def kernel(x, norm1_g, w_in, q_norm_a, k_norm_a, q_norm_b, k_norm_b, rel_bias, sinks,
           w_branch_a, w_branch_b, w_out, norm2_g, w_gate_up, w_down):
    b, s, d = x.shape
    depth = w_in.shape[0]
    assert d == D_MODEL and s % MOBA_PAIR == 0 and s % SWA_TQ == 0
    assert (b * s) % IN_TM == 0 and (b * s) % MERGE_TM == 0 and (b * s) % FFN_TM == 0
    far = _t5_bucket_np(np.arange(LANES + 1, s + MOBA_PAIR))
    assert (far == far[0]).all()
    far_bucket = int(far[0])

    bias_a = _moba_bias(rel_bias, jnp.asarray(_moba_bucket_tables()), far_bucket)
    bk_b, va_b = _swa_bias_tables()
    bias_b = _swa_bias(rel_bias, jnp.asarray(bk_b), jnp.asarray(va_b))
    bd128 = jnp.asarray(_block_diag_ones(IN_TN, A_HEAD_DIM), BF16)
    bd64 = jnp.asarray(_block_diag_ones(IN_TN, B_HEAD_DIM), BF16)

    o_qb = 3 * A_WIDTH
    o_kb = o_qb + B_WIDTH
    o_g = o_kb + 2 * B_KV_WIDTH
    x2 = x.reshape(b * s, d)
    for l in range(depth):
        w = w_in[l]
        w_perm = jnp.concatenate(
            [w[:, :o_kb], w[:, o_g:], w[:, o_kb:o_g],
             jnp.zeros((d, IN_TN - 2 * B_KV_WIDTH), w.dtype)], axis=1).astype(BF16)
        ones = jnp.ones((IN_TN,), F32)
        colgain = jnp.stack(
            [jnp.tile(q_norm_a[l] * (A_HEAD_DIM ** -0.5 * LOG2E), IN_TN // A_HEAD_DIM)] * 2
            + [jnp.tile(k_norm_a[l], IN_TN // A_HEAD_DIM)] * 2
            + [ones] * 2
            + [jnp.tile(q_norm_b[l] * (B_HEAD_DIM ** -0.5), IN_TN // B_HEAD_DIM)] * 2
            + [ones] * IN_GATE_TILES
            + [jnp.concatenate([jnp.tile(k_norm_b[l], B_KV_HEADS),
                                jnp.ones((IN_TN - B_KV_WIDTH,), F32)])]
            + [ones] * (24 - IN_TILES))
        qkv, gates, kvx = _inproj(x2, norm1_g[l][None, :], w_perm, colgain, bd128, bd64)
        qkv3 = qkv.reshape(b, s, -1)
        ya = _moba(qkv3, bias_a)
        yb = _swa(qkv3, kvx.reshape(b, s, -1), bias_b, sinks[l])
        x1 = _merge(ya.reshape(b * s, -1), yb.reshape(b * s, -1), gates, x2,
                    w_branch_a[l].astype(BF16), w_branch_b[l].astype(BF16),
                    w_out[l].astype(BF16))
        x2 = _ffn(x1, norm2_g[l][None, :], w_gate_up[l].astype(BF16), w_down[l].astype(BF16))
    return x2.reshape(b, s, d)
```

```python
import functools
import math

import jax
import jax.numpy as jnp
import numpy as np
from jax import lax
from jax.experimental import pallas as pl
from jax.experimental.pallas import tpu as pltpu

F32 = jnp.float32
BF16 = jnp.bfloat16

D_MODEL = 2048
A_HEADS = 8
A_HEAD_DIM = 128
A_WIDTH = A_HEADS * A_HEAD_DIM
MOBA_BLOCK = 256
MOBA_TOPK = 3
B_Q_HEADS = 16
B_KV_HEADS = 2
B_HEAD_DIM = 64
B_WIDTH = B_Q_HEADS * B_HEAD_DIM
B_KV_WIDTH = B_KV_HEADS * B_HEAD_DIM
WINDOW = 128
NUM_BUCKETS = 32
MAX_DISTANCE = 128
MAX_EXACT = NUM_BUCKETS // 2
D_FF = -(-(8 * D_MODEL) // (3 * 256)) * 256
IN_WIDTH = 3 * A_WIDTH + B_WIDTH + 2 * B_KV_WIDTH + 2 * D_MODEL
EPS = 1e-6
NEG = -1e30
LOG2E = math.log2(math.e)

LANES = 128
VMEM_LIMIT_BYTES = 56 * 1024 * 1024

IN_TN = 512
IN_QKV_TILES = (3 * A_WIDTH + B_WIDTH) // IN_TN
IN_GATE_TILES = (2 * D_MODEL) // IN_TN
IN_TILES = IN_QKV_TILES + IN_GATE_TILES + 1
IN_TM = 1024
KVX_WIDTH = 8 * LANES

MOBA_PAIR = 2 * MOBA_BLOCK
MOBA_HEADS_PER_STEP = 4
SWA_TQ = 512
MERGE_TM = 512
MERGE_CH = 512
FFN_TM = 512
FFN_TF = 512


def _dot(a, b):
    return jnp.dot(a, b, preferred_element_type=F32)


def _dot_nt(a, b):
    return lax.dot_general(a, b, (((1,), (1,)), ((), ())), preferred_element_type=F32)


def _sigmoid(v):
    return 0.5 * jnp.tanh(0.5 * v) + 0.5


def _t5_bucket_np(dist):
    n = np.maximum(dist, 0)
    nf = np.maximum(n, 1).astype(np.float32)
    large = MAX_EXACT + (np.log(nf / np.float32(MAX_EXACT))
                         / np.float32(math.log(MAX_DISTANCE / MAX_EXACT))
                         * np.float32(NUM_BUCKETS - MAX_EXACT)).astype(np.int32)
    large = np.minimum(large, NUM_BUCKETS - 1)
    return np.where(n < MAX_EXACT, n, large).astype(np.int32)


def _bucket_lookup(tab_ref, bm, h):
    acc = jnp.zeros(bm.shape, F32)
    for b in range(NUM_BUCKETS):
        acc = jnp.where(bm == b, tab_ref[b, h], acc)
    return acc


def _swa_bias_kernel(tab_ref, bucket_ref, valid_ref, o_ref):
    h = pl.program_id(0) + A_HEADS
    for t in range(bucket_ref.shape[0]):
        o_ref[t, 0] = jnp.where(valid_ref[t] != 0, _bucket_lookup(tab_ref, bucket_ref[t], h), NEG)


def _swa_bias(rel_bias, buckets, valid):
    n_tiles, r, c = buckets.shape
    return pl.pallas_call(
        _swa_bias_kernel,
        grid=(B_Q_HEADS,),
        in_specs=[
            pl.BlockSpec(memory_space=pltpu.SMEM),
            pl.BlockSpec((n_tiles, r, c), lambda h: (0, 0, 0)),
            pl.BlockSpec((n_tiles, r, c), lambda h: (0, 0, 0)),
        ],
        out_specs=pl.BlockSpec((n_tiles, 1, r, c), lambda h: (0, h, 0, 0)),
        out_shape=jax.ShapeDtypeStruct((n_tiles, B_Q_HEADS, r, c), F32),
        name="swa_bias",
    )(rel_bias, buckets, valid)


def _moba_bias_kernel(tab_ref, bucket_ref, o_ref, *, shift_bucket):
    h = pl.program_id(0)
    L = MOBA_BLOCK
    shift = tab_ref[shift_bucket, h]
    tiles = [(_bucket_lookup(tab_ref, bucket_ref[t], h) - shift) * LOG2E for t in range(2)]
    kk = lax.broadcasted_iota(jnp.int32, (L, L), 0)
    qq = lax.broadcasted_iota(jnp.int32, (L, L), 1)
    own = jnp.where(qq >= kk, tiles[0], NEG)
    o_ref[0, :L, :L] = own
    o_ref[0, L:, L:] = own
    o_ref[0, :L, L:] = tiles[1]
    o_ref[0, L:, :L] = jnp.full((L, L), NEG, F32)


def _moba_bias(rel_bias, buckets, shift_bucket):
    L = MOBA_BLOCK
    return pl.pallas_call(
        functools.partial(_moba_bias_kernel, shift_bucket=shift_bucket),
        grid=(A_HEADS,),
        in_specs=[
            pl.BlockSpec(memory_space=pltpu.SMEM),
            pl.BlockSpec((2, L, L), lambda h: (0, 0, 0)),
        ],
        out_specs=pl.BlockSpec((1, 2 * L, 2 * L), lambda h: (h, 0, 0)),
        out_shape=jax.ShapeDtypeStruct((A_HEADS, 2 * L, 2 * L), F32),
        name="moba_bias",
    )(rel_bias, buckets)


def _inproj_kernel(x_ref, g1_ref, w_ref, cg_ref, bd128_ref, bd64_ref,
                   qkv_ref, gates_ref, kvx_ref, h_ref):
    j = pl.program_id(1)

    @pl.when(j == 0)
    def _():
        x = x_ref[...]
        ms = jnp.mean(x * x, axis=-1, keepdims=True)
        h_ref[...] = (x * lax.rsqrt(ms + EPS) * g1_ref[...]).astype(BF16)

    cg = cg_ref[pl.ds(j, 1), :]

    def project(cols=slice(None)):
        return _dot(h_ref[...], w_ref[:, cols])

    def head_norm(v, bd, dh, gain):
        ss = _dot((v * v).astype(BF16), bd)
        return v * lax.rsqrt(ss * (1.0 / dh) + EPS) * gain

    n_a = 2 * A_WIDTH // IN_TN
    n_va = 3 * A_WIDTH // IN_TN

    @pl.when(j < n_a)
    def _():
        qkv_ref[...] = head_norm(project(), bd128_ref[...], A_HEAD_DIM, cg).astype(BF16)

    @pl.when((j >= n_a) & (j < n_va))
    def _():
        qkv_ref[...] = project().astype(BF16)

    @pl.when((j >= n_va) & (j < IN_QKV_TILES))
    def _():
        qkv_ref[...] = head_norm(project(), bd64_ref[...], B_HEAD_DIM, cg).astype(BF16)

    @pl.when((j >= IN_QKV_TILES) & (j < IN_QKV_TILES + IN_GATE_TILES))
    def _():
        gates_ref[...] = project().astype(BF16)

    @pl.when(j == IN_TILES - 1)
    def _():
        y = project(slice(0, 2 * LANES))
        kn = head_norm(y[:, :LANES], bd64_ref[:LANES, :LANES], B_HEAD_DIM, cg[:, :LANES])
        vv = y[:, LANES:2 * LANES]
        lo = lax.broadcasted_iota(jnp.int32, kn.shape, 1) < B_HEAD_DIM
        for base, t in ((0, kn), (4, vv)):
            e0 = jnp.where(lo, t, 0.0)
            o1 = jnp.where(lo, 0.0, t)
            o0 = pltpu.roll(e0, B_HEAD_DIM, 1)
            e1 = pltpu.roll(o1, B_HEAD_DIM, 1)
            for k, piece in enumerate((e0, o0, e1, o1)):
                kvx_ref[:, (base + k) * LANES:(base + k + 1) * LANES] = piece.astype(BF16)


def _inproj(x2, g1, w_perm, colgain, bd128, bd64):
    n = x2.shape[0]
    grid = (n // IN_TM, IN_TILES)
    return pl.pallas_call(
        _inproj_kernel,
        grid=grid,
        in_specs=[
            pl.BlockSpec((IN_TM, D_MODEL), lambda i, j: (i, 0)),
            pl.BlockSpec((1, D_MODEL), lambda i, j: (0, 0)),
            pl.BlockSpec((D_MODEL, IN_TN), lambda i, j: (0, j)),
            pl.BlockSpec(colgain.shape, lambda i, j: (0, 0)),
            pl.BlockSpec((IN_TN, IN_TN), lambda i, j: (0, 0)),
            pl.BlockSpec((IN_TN, IN_TN), lambda i, j: (0, 0)),
        ],
        out_specs=[
            pl.BlockSpec((IN_TM, IN_TN), lambda i, j: (i, jnp.minimum(j, IN_QKV_TILES - 1))),
            pl.BlockSpec((IN_TM, IN_TN),
                         lambda i, j: (i, jnp.clip(j - IN_QKV_TILES, 0, IN_GATE_TILES - 1))),
            pl.BlockSpec((IN_TM, KVX_WIDTH), lambda i, j: (i, 0)),
        ],
        out_shape=[
            jax.ShapeDtypeStruct((n, IN_QKV_TILES * IN_TN), BF16),
            jax.ShapeDtypeStruct((n, IN_GATE_TILES * IN_TN), BF16),
            jax.ShapeDtypeStruct((n, KVX_WIDTH), BF16),
        ],
        scratch_shapes=[pltpu.VMEM((IN_TM, D_MODEL), BF16)],
        compiler_params=pltpu.CompilerParams(
            dimension_semantics=("arbitrary", "arbitrary"),
            vmem_limit_bytes=VMEM_LIMIT_BYTES),
        name="inproj",
    )(x2, g1, w_perm, colgain, bd128, bd64)


def _moba_kernel(q_ref, k_ref, v_ref, bd_ref, bc_ref, o_ref,
                 kmh_ref, kml_ref, vt_ref, qa_ref, s_ref, m_ref, l_ref, acc_ref):
    t = pl.program_id(2)
    L = MOBA_BLOCK
    P = MOBA_PAIR
    Dh = A_HEAD_DIM
    NH = MOBA_HEADS_PER_STEP
    seq = k_ref.shape[1]
    nb = seq // L

    @pl.when(t == 0)
    def _():
        r = lax.broadcasted_iota(jnp.int32, (nb, seq), 0)
        c = lax.broadcasted_iota(jnp.int32, (nb, seq), 1)
        ind = jnp.where(lax.shift_right_logical(c, int(math.log2(L))) == r, 1.0, 0.0).astype(BF16)
        er = lax.broadcasted_iota(jnp.int32, (Dh, Dh), 0)
        ec = lax.broadcasted_iota(jnp.int32, (Dh, Dh), 1)
        eye = jnp.where(er == ec, 1.0, 0.0).astype(BF16)
        for hh in range(NH):
            km = _dot(ind, k_ref[0, :, hh * Dh:(hh + 1) * Dh]) * (1.0 / L)
            hi = km.astype(BF16)
            kmh_ref[hh] = hi
            kml_ref[hh] = (km - hi.astype(F32)).astype(BF16)
            for g in range(seq // P):
                vt_ref[hh, g] = _dot_nt(
                    eye, v_ref[0, g * P:(g + 1) * P, hh * Dh:(hh + 1) * Dh]).astype(BF16)

    lane = lax.broadcasted_iota(jnp.int32, (P, LANES), 1)
    row = lax.broadcasted_iota(jnp.int32, (P, LANES), 0)
    upper = (row >= L).astype(jnp.int32)

    def select(hh):
        q = q_ref[0, :, hh * Dh:(hh + 1) * Dh]
        g = _dot_nt(kmh_ref[hh], q) + _dot_nt(kml_ref[hh], q)
        blk = lax.broadcasted_iota(jnp.int32, g.shape, 0)
        blk_f = blk.astype(F32)
        qcol = lax.broadcasted_iota(jnp.int32, g.shape, 1)
        own_blk = 2 * t + (qcol >= L).astype(jnp.int32)
        past = blk < own_blk
        g = jnp.where(past, g, NEG)
        sel = jnp.full(g.shape, NEG, F32)
        for _ in range(MOBA_TOPK):
            mx = jnp.max(g, axis=0, keepdims=True)
            first = jnp.min(jnp.where(g == mx, blk_f, float(LANES)), axis=0, keepdims=True)
            hit = blk_f == first
            sel = jnp.where(hit & past, 0.0, sel)
            g = jnp.where(hit, -jnp.inf, g)
        sel = jnp.where(blk == own_blk, 0.0, sel)
        sel = jnp.concatenate([sel, jnp.zeros((LANES - nb, P), F32)], axis=0)
        qa_ref[hh, :, :Dh] = q
        qa_ref[hh, :, Dh:] = sel.T.astype(BF16)

    def scores(hh, pair):
        r0 = pl.multiple_of(pair * P, P)
        ind = jnp.where(lane == 2 * pair + upper, 1.0, 0.0).astype(BF16)
        k_aug = jnp.concatenate([k_ref[0, pl.ds(r0, P), hh * Dh:(hh + 1) * Dh], ind], axis=1)
        return _dot_nt(k_aug, qa_ref[hh])

    def accumulate(hh, pair):
        s = s_ref[hh]
        m_old = m_ref[hh]
        m_new = jnp.maximum(m_old, jnp.max(s, axis=0, keepdims=True))
        alpha = jnp.exp2(m_old - m_new)
        p = jnp.exp2(s - m_new)
        l_ref[hh] = alpha * l_ref[hh] + jnp.sum(p, axis=0, keepdims=True)
        acc_ref[hh] = alpha * acc_ref[hh] + _dot(vt_ref[hh, pair], p.astype(BF16))
        m_ref[hh] = m_new

    for hh in range(NH):
        select(hh)
    for hh in range(NH):
        s = scores(hh, t) + bd_ref[hh]
        m0 = jnp.max(s, axis=0, keepdims=True)
        p = jnp.exp2(s - m0)
        m_ref[hh] = m0
        l_ref[hh] = jnp.sum(p, axis=0, keepdims=True)
        acc_ref[hh] = _dot(vt_ref[hh, t], p.astype(BF16))

    @pl.when(t >= 1)
    def _():
        for hh in range(NH):
            s_ref[hh] = scores(hh, t - 1)
            s_ref[hh, P - LANES:, :LANES] += bc_ref[hh]
        for hh in range(NH):
            accumulate(hh, t - 1)

    def far_body(pair, carry):
        for hh in range(NH):
            s_ref[hh] = scores(hh, pair)
        for hh in range(NH):
            accumulate(hh, pair)
        return carry

    lax.fori_loop(0, jnp.maximum(t - 1, 0), far_body, 0)

    for hh in range(NH):
        o_ref[0, :, hh * Dh:(hh + 1) * Dh] = (acc_ref[hh] / l_ref[hh]).T.astype(BF16)


def _moba(qkv3, bias_a):
    b, s, _ = qkv3.shape
    P = MOBA_PAIR
    Dh = A_HEAD_DIM
    NH = MOBA_HEADS_PER_STEP
    G = A_HEADS // NH
    nb = s // MOBA_BLOCK
    assert nb % 16 == 0 and nb <= LANES
    return pl.pallas_call(
        _moba_kernel,
        grid=(b, G, s // P),
        in_specs=[
            pl.BlockSpec((1, P, NH * Dh), lambda bi, h, t: (bi, t, h)),
            pl.BlockSpec((1, s, NH * Dh), lambda bi, h, t: (bi, 0, G + h)),
            pl.BlockSpec((1, s, NH * Dh), lambda bi, h, t: (bi, 0, 2 * G + h)),
            pl.BlockSpec((NH, P, P), lambda bi, h, t: (h, 0, 0)),
            pl.BlockSpec((NH, LANES, LANES), lambda bi, h, t: (h, 1, 2)),
        ],
        out_specs=pl.BlockSpec((1, P, NH * Dh), lambda bi, h, t: (bi, t, h)),
        out_shape=jax.ShapeDtypeStruct((b, s, A_WIDTH), BF16),
        scratch_shapes=[
            pltpu.VMEM((NH, nb, Dh), BF16),
            pltpu.VMEM((NH, nb, Dh), BF16),
            pltpu.VMEM((NH, s // P, Dh, P), BF16),
            pltpu.VMEM((NH, P, Dh + LANES), BF16),
            pltpu.VMEM((NH, P, P), F32),
            pltpu.VMEM((NH, 1, P), F32),
            pltpu.VMEM((NH, 1, P), F32),
            pltpu.VMEM((NH, Dh, P), F32),
        ],
        compiler_params=pltpu.CompilerParams(
            dimension_semantics=("arbitrary", "arbitrary", "arbitrary"),
            vmem_limit_bytes=VMEM_LIMIT_BYTES),
        name="moba",
    )(qkv3, qkv3, qkv3, bias_a, bias_a)


def _swa_kernel(q_ref, kvx_ref, bias_ref, sink_ref, o_ref):
    t = pl.program_id(1)
    W = WINDOW
    n_win = q_ref.shape[1] // W
    pairs = B_Q_HEADS // 2
    pairs_per_kv = pairs // B_KV_HEADS

    def win(w, carry):
        n = t * n_win + w
        q0 = pl.multiple_of(w * W, W)
        start = pl.multiple_of(jnp.maximum(n - 1, 0) * W, W)
        first = (n == 0).astype(jnp.int32)
        for p in range(pairs):
            kv = p // pairs_per_kv
            q2 = q_ref[0, pl.ds(q0, W), p * LANES:(p + 1) * LANES]
            out = None
            for par in range(2):
                h = 2 * p + par
                kcol = (2 * kv + par) * LANES
                vcol = (4 + 2 * kv + par) * LANES
                kk = kvx_ref[0, pl.ds(start, 2 * W), kcol:kcol + LANES]
                vv = kvx_ref[0, pl.ds(start, 2 * W), vcol:vcol + LANES]
                s = _dot_nt(q2, kk) + bias_ref[first, h]
                sink = sink_ref[h]
                m = jnp.maximum(jnp.max(s, axis=1, keepdims=True), sink)
                pe = jnp.exp(s - m)
                l = jnp.sum(pe, axis=1, keepdims=True) + jnp.exp(sink - m)
                o = _dot(pe.astype(BF16), vv) / l
                out = o if out is None else out + o
            o_ref[0, pl.ds(q0, W), p * LANES:(p + 1) * LANES] = out.astype(BF16)
        return carry

    lax.fori_loop(0, n_win, win, 0)


def _swa(qkv3, kvx3, bias_b, sinks):
    b, s, _ = qkv3.shape
    qb_blk = 3 * A_WIDTH // B_WIDTH
    return pl.pallas_call(
        _swa_kernel,
        grid=(b, s // SWA_TQ),
        in_specs=[
            pl.BlockSpec((1, SWA_TQ, B_WIDTH), lambda bi, t: (bi, t, qb_blk)),
            pl.BlockSpec((1, s, KVX_WIDTH), lambda bi, t: (bi, 0, 0)),
            pl.BlockSpec(bias_b.shape, lambda bi, t: (0, 0, 0, 0)),
            pl.BlockSpec(memory_space=pltpu.SMEM),
        ],
        out_specs=pl.BlockSpec((1, SWA_TQ, B_WIDTH), lambda bi, t: (bi, t, 0)),
        out_shape=jax.ShapeDtypeStruct((b, s, B_WIDTH), BF16),
        compiler_params=pltpu.CompilerParams(
            dimension_semantics=("arbitrary", "arbitrary"),
            vmem_limit_bytes=VMEM_LIMIT_BYTES),
        name="swa",
    )(qkv3, kvx3, bias_b, sinks)


def _merge_kernel(ya_ref, yb_ref, gt_ref, x_ref, wa_ref, wb_ref, wo_ref, o_ref, mg_ref):
    ch = MERGE_CH
    for c in range(D_MODEL // ch):
        cols = slice(c * ch, (c + 1) * ch)
        ta = _dot(ya_ref[...], wa_ref[:, cols])
        tb = _dot(yb_ref[...], wb_ref[:, cols])
        ga = _sigmoid(gt_ref[:, cols].astype(F32))
        gb = _sigmoid(gt_ref[:, D_MODEL + c * ch:D_MODEL + (c + 1) * ch].astype(F32))
        mg_ref[:, cols] = (ga * ta + gb * tb).astype(BF16)
    for c in range(D_MODEL // ch):
        cols = slice(c * ch, (c + 1) * ch)
        o_ref[:, cols] = x_ref[:, cols] + _dot(mg_ref[...], wo_ref[:, cols])


def _merge(ya, yb, gates, x2, wa, wb, wo):
    n = x2.shape[0]
    tm = MERGE_TM
    resident = functools.partial(pl.BlockSpec, pipeline_mode=pl.Buffered(1))
    return pl.pallas_call(
        _merge_kernel,
        grid=(n // tm,),
        in_specs=[
            pl.BlockSpec((tm, A_WIDTH), lambda i: (i, 0)),
            pl.BlockSpec((tm, B_WIDTH), lambda i: (i, 0)),
            pl.BlockSpec((tm, 2 * D_MODEL), lambda i: (i, 0)),
            pl.BlockSpec((tm, D_MODEL), lambda i: (i, 0)),
            resident((A_WIDTH, D_MODEL), lambda i: (0, 0)),
            resident((B_WIDTH, D_MODEL), lambda i: (0, 0)),
            resident((D_MODEL, D_MODEL), lambda i: (0, 0)),
        ],
        out_specs=pl.BlockSpec((tm, D_MODEL), lambda i: (i, 0)),
        out_shape=jax.ShapeDtypeStruct((n, D_MODEL), F32),
        scratch_shapes=[pltpu.VMEM((tm, D_MODEL), BF16)],
        compiler_params=pltpu.CompilerParams(
            dimension_semantics=("arbitrary",),
            vmem_limit_bytes=VMEM_LIMIT_BYTES),
        name="merge",
    )(ya, yb, gates, x2, wa, wb, wo)


def _ffn_kernel(x_ref, g2_ref, wg_ref, wu_ref, wd_ref, o_ref, h_ref):
    f = pl.program_id(1)

    @pl.when(f == 0)
    def _():
        x = x_ref[...]
        ms = jnp.mean(x * x, axis=-1, keepdims=True)
        h_ref[...] = (x * lax.rsqrt(ms + EPS) * g2_ref[...]).astype(BF16)
        o_ref[...] = x

    h = h_ref[...]
    g = _dot(h, wg_ref[...])
    u = _dot(h, wu_ref[...])
    a = (g * _sigmoid(g) * u).astype(BF16)
    o_ref[...] += _dot(a, wd_ref[...])


def _ffn(x1, g2, wgu, wd):
    n = x1.shape[0]
    tm, tf = FFN_TM, FFN_TF
    nf = D_FF // tf
    return pl.pallas_call(
        _ffn_kernel,
        grid=(n // tm, nf),
        in_specs=[
            pl.BlockSpec((tm, D_MODEL), lambda i, f: (i, 0)),
            pl.BlockSpec((1, D_MODEL), lambda i, f: (0, 0)),
            pl.BlockSpec((D_MODEL, tf), lambda i, f: (0, f)),
            pl.BlockSpec((D_MODEL, tf), lambda i, f: (0, nf + f)),
            pl.BlockSpec((tf, D_MODEL), lambda i, f: (f, 0)),
        ],
        out_specs=pl.BlockSpec((tm, D_MODEL), lambda i, f: (i, 0)),
        out_shape=jax.ShapeDtypeStruct((n, D_MODEL), F32),
        scratch_shapes=[pltpu.VMEM((tm, D_MODEL), BF16)],
        compiler_params=pltpu.CompilerParams(
            dimension_semantics=("arbitrary", "arbitrary"),
            vmem_limit_bytes=VMEM_LIMIT_BYTES),
        name="ffn",
    )(x1, g2, wgu, wgu, wd)


def _moba_bucket_tables():
    L = MOBA_BLOCK
    kj = np.arange(L)[:, None]
    qi = np.arange(L)[None, :]
    return np.stack([_t5_bucket_np(qi - kj), _t5_bucket_np(L + qi - kj)])


def _swa_bias_tables():
    W = WINDOW
    qi = np.arange(W)[:, None]
    c = np.arange(2 * W)[None, :]
    d = W + qi - c
    v = (d >= 0) & (d < W)
    d0 = qi - c
    v0 = (d0 >= 0) & (d0 < W)
    buckets = np.stack([_t5_bucket_np(d), _t5_bucket_np(d0)])
    valid = np.stack([v, v0]).astype(np.int32)
    return buckets, valid


def _block_diag_ones(n, blk):
    r = np.arange(n)
    return (r[:, None] // blk == r[None, :] // blk).astype(np.float32)


def kernel(x, norm1_g, w_in, q_norm_a, k_norm_a, q_norm_b, k_norm_b, rel_bias, sinks,
           w_branch_a, w_branch_b, w_out, norm2_g, w_gate_up, w_down):
    b, s, d = x.shape
    depth = w_in.shape[0]
    assert d == D_MODEL and s % MOBA_PAIR == 0 and s % SWA_TQ == 0
    assert (b * s) % IN_TM == 0 and (b * s) % MERGE_TM == 0 and (b * s) % FFN_TM == 0
    far = _t5_bucket_np(np.arange(LANES + 1, s + MOBA_PAIR))
    assert (far == far[0]).all()
    far_bucket = int(far[0])

    bias_a = _moba_bias(rel_bias, jnp.asarray(_moba_bucket_tables()), far_bucket)
    bk_b, va_b = _swa_bias_tables()
    bias_b = _swa_bias(rel_bias, jnp.asarray(bk_b), jnp.asarray(va_b))
    bd128 = jnp.asarray(_block_diag_ones(IN_TN, A_HEAD_DIM), BF16)
    bd64 = jnp.asarray(_block_diag_ones(IN_TN, B_HEAD_DIM), BF16)

    o_qb = 3 * A_WIDTH
    o_kb = o_qb + B_WIDTH
    o_g = o_kb + 2 * B_KV_WIDTH
    x2 = x.reshape(b * s, d)
    for l in range(depth):
        w = w_in[l]
        w_perm = jnp.concatenate(
            [w[:, :o_kb], w[:, o_g:], w[:, o_kb:o_g],
             jnp.zeros((d, IN_TN - 2 * B_KV_WIDTH), w.dtype)], axis=1).astype(BF16)
        ones = jnp.ones((IN_TN,), F32)
        colgain = jnp.stack(
            [jnp.tile(q_norm_a[l] * (A_HEAD_DIM ** -0.5 * LOG2E), IN_TN // A_HEAD_DIM)] * 2
            + [jnp.tile(k_norm_a[l], IN_TN // A_HEAD_DIM)] * 2
            + [ones] * 2
            + [jnp.tile(q_norm_b[l] * (B_HEAD_DIM ** -0.5), IN_TN // B_HEAD_DIM)] * 2
            + [ones] * IN_GATE_TILES
            + [jnp.concatenate([jnp.tile(k_norm_b[l], B_KV_HEADS),
                                jnp.ones((IN_TN - B_KV_WIDTH,), F32)])]
            + [ones] * (24 - IN_TILES))
        qkv, gates, kvx = _inproj(x2, norm1_g[l][None, :], w_perm, colgain, bd128, bd64)
        qkv3 = qkv.reshape(b, s, -1)
        ya = _moba(qkv3, bias_a)
        yb = _swa(qkv3, kvx.reshape(b, s, -1), bias_b, sinks[l])
        x1 = _merge(ya.reshape(b * s, -1), yb.reshape(b * s, -1), gates, x2,
                    w_branch_a[l].astype(BF16), w_branch_b[l].astype(BF16),
                    w_out[l].astype(BF16))
        x2 = _ffn(x1, norm2_g[l][None, :], w_gate_up[l].astype(BF16), w_down[l].astype(BF16))
    return x2.reshape(b, s, d)
```

```python
import functools
import math

import jax
import jax.numpy as jnp
import numpy as np
from jax import lax
from jax.experimental import pallas as pl
from jax.experimental.pallas import tpu as pltpu

F32 = jnp.float32
BF16 = jnp.bfloat16

D_MODEL = 2048
A_HEADS = 8
A_HEAD_DIM = 128
A_WIDTH = A_HEADS * A_HEAD_DIM
MOBA_BLOCK = 256
MOBA_TOPK = 3
B_Q_HEADS = 16
B_KV_HEADS = 2
B_HEAD_DIM = 64
B_WIDTH = B_Q_HEADS * B_HEAD_DIM
B_KV_WIDTH = B_KV_HEADS * B_HEAD_DIM
WINDOW = 128
NUM_BUCKETS = 32
MAX_DISTANCE = 128
MAX_EXACT = NUM_BUCKETS // 2
D_FF = -(-(8 * D_MODEL) // (3 * 256)) * 256
IN_WIDTH = 3 * A_WIDTH + B_WIDTH + 2 * B_KV_WIDTH + 2 * D_MODEL
EPS = 1e-6
NEG = -1e30
LOG2E = math.log2(math.e)

LANES = 128
VMEM_LIMIT_BYTES = 56 * 1024 * 1024

IN_TN = 512
IN_QKV_TILES = (3 * A_WIDTH + B_WIDTH) // IN_TN
IN_GATE_TILES = (2 * D_MODEL) // IN_TN
IN_TILES = IN_QKV_TILES + IN_GATE_TILES + 1
IN_KV_COL = 3 * A_WIDTH + B_WIDTH
IN_GATE_COL = IN_KV_COL + 2 * B_KV_WIDTH
IN_TM = 1024
KVX_WIDTH = 8 * LANES

MOBA_PAIR = 2 * MOBA_BLOCK
MOBA_HEADS_PER_STEP = 4
SWA_TQ = 512
MERGE_TM = 512
MERGE_CH = 512
FFN_TM = 1024
FFN_TF = 512


def _dot(a, b):
    return jnp.dot(a, b, preferred_element_type=F32)


def _dot_nt(a, b):
    return lax.dot_general(a, b, (((1,), (1,)), ((), ())), preferred_element_type=F32)


def _sigmoid(v):
    return 0.5 * jnp.tanh(0.5 * v) + 0.5


def _t5_bucket_np(dist):
    n = np.maximum(dist, 0)
    nf = np.maximum(n, 1).astype(np.float32)
    large = MAX_EXACT + (np.log(nf / np.float32(MAX_EXACT))
                         / np.float32(math.log(MAX_DISTANCE / MAX_EXACT))
                         * np.float32(NUM_BUCKETS - MAX_EXACT)).astype(np.int32)
    large = np.minimum(large, NUM_BUCKETS - 1)
    return np.where(n < MAX_EXACT, n, large).astype(np.int32)


def _bucket_lookup(tab_ref, bm, h):
    acc = jnp.zeros(bm.shape, F32)
    for b in range(NUM_BUCKETS):
        acc = jnp.where(bm == b, tab_ref[b, h], acc)
    return acc


def _swa_bias_kernel(tab_ref, bucket_ref, valid_ref, o_ref):
    h = pl.program_id(0) + A_HEADS
    for t in range(bucket_ref.shape[0]):
        o_ref[t, 0, 0] = jnp.where(valid_ref[t] != 0,
                                   _bucket_lookup(tab_ref, bucket_ref[t], h) * LOG2E, NEG)


def _swa_bias(rel_bias, buckets, valid):
    n_tiles, r, c = buckets.shape
    per_kv = B_Q_HEADS // B_KV_HEADS
    return pl.pallas_call(
        _swa_bias_kernel,
        grid=(B_Q_HEADS,),
        in_specs=[
            pl.BlockSpec(memory_space=pltpu.SMEM),
            pl.BlockSpec((n_tiles, r, c), lambda h: (0, 0, 0)),
            pl.BlockSpec((n_tiles, r, c), lambda h: (0, 0, 0)),
        ],
        out_specs=pl.BlockSpec(
            (n_tiles, 1, 1, r, c),
            lambda h: (0, h // per_kv, h % 2, 0, (h % per_kv) // 2)),
        out_shape=jax.ShapeDtypeStruct((n_tiles, B_KV_HEADS, 2, r, c * per_kv // 2), F32),
        name="swa_bias",
    )(rel_bias, buckets, valid)


def _moba_bias_kernel(tab_ref, bucket_ref, o_ref, *, shift_bucket):
    h = pl.program_id(0)
    L = MOBA_BLOCK
    shift = tab_ref[shift_bucket, h]
    tiles = [(_bucket_lookup(tab_ref, bucket_ref[t], h) - shift) * LOG2E for t in range(2)]
    kk = lax.broadcasted_iota(jnp.int32, (L, L), 0)
    qq = lax.broadcasted_iota(jnp.int32, (L, L), 1)
    own = jnp.where(qq >= kk, tiles[0], NEG)
    o_ref[0, :L, :L] = own
    o_ref[0, L:, L:] = own
    o_ref[0, :L, L:] = tiles[1]
    o_ref[0, L:, :L] = jnp.full((L, L), NEG, F32)


def _moba_bias(rel_bias, buckets, shift_bucket):
    L = MOBA_BLOCK
    return pl.pallas_call(
        functools.partial(_moba_bias_kernel, shift_bucket=shift_bucket),
        grid=(A_HEADS,),
        in_specs=[
            pl.BlockSpec(memory_space=pltpu.SMEM),
            pl.BlockSpec((2, L, L), lambda h: (0, 0, 0)),
        ],
        out_specs=pl.BlockSpec((1, 2 * L, 2 * L), lambda h: (h, 0, 0)),
        out_shape=jax.ShapeDtypeStruct((A_HEADS, 2 * L, 2 * L), F32),
        name="moba_bias",
    )(rel_bias, buckets)


def _inproj_kernel(x_ref, g1_ref, w_ref, cg_ref, bd128_ref, bd64_ref,
                   qkv_ref, gates_ref, kvx_ref, h_ref):
    j = pl.program_id(1)

    @pl.when(j == 0)
    def _():
        x = x_ref[...]
        ms = jnp.mean(x * x, axis=-1, keepdims=True)
        h_ref[...] = (x * lax.rsqrt(ms + EPS) * g1_ref[...]).astype(BF16)

    cg = cg_ref[pl.ds(j, 1), :]

    def project(cols=slice(None)):
        return _dot(h_ref[...], w_ref[:, cols].astype(BF16))

    def head_norm(v, bd, dh, gain):
        ss = _dot((v * v).astype(BF16), bd)
        return v * lax.rsqrt(ss * (1.0 / dh) + EPS) * gain

    n_a = 2 * A_WIDTH // IN_TN
    n_va = 3 * A_WIDTH // IN_TN

    @pl.when(j < n_a)
    def _():
        qkv_ref[...] = head_norm(project(), bd128_ref[...], A_HEAD_DIM, cg).astype(BF16)

    @pl.when((j >= n_a) & (j < n_va))
    def _():
        qkv_ref[...] = project().astype(BF16)

    @pl.when((j >= n_va) & (j < IN_QKV_TILES))
    def _():
        qkv_ref[...] = head_norm(project(), bd64_ref[...], B_HEAD_DIM, cg).astype(BF16)

    @pl.when((j >= IN_QKV_TILES) & (j < IN_QKV_TILES + IN_GATE_TILES))
    def _():
        gates_ref[...] = project().astype(BF16)

    @pl.when(j == IN_TILES - 1)
    def _():
        y = project(slice(0, 2 * LANES))
        kn = head_norm(y[:, :LANES], bd64_ref[:LANES, :LANES], B_HEAD_DIM, cg[:, :LANES])
        vv = y[:, LANES:2 * LANES]
        lo = lax.broadcasted_iota(jnp.int32, kn.shape, 1) < B_HEAD_DIM
        for base, t in ((0, kn), (4, vv)):
            e0 = jnp.where(lo, t, 0.0)
            o1 = jnp.where(lo, 0.0, t)
            o0 = pltpu.roll(e0, B_HEAD_DIM, 1)
            e1 = pltpu.roll(o1, B_HEAD_DIM, 1)
            for k, piece in enumerate((e0, o0, e1, o1)):
                kvx_ref[:, (base + k) * LANES:(base + k + 1) * LANES] = piece.astype(BF16)


def _inproj_w_col(j):
    t = IN_TN // LANES
    lane_tile = jnp.where(j < IN_QKV_TILES, j * t,
                          jnp.where(j < IN_QKV_TILES + IN_GATE_TILES,
                                    IN_GATE_COL // LANES + (j - IN_QKV_TILES) * t,
                                    IN_KV_COL // LANES))
    return lane_tile * LANES


def _inproj(x2, g1, w, colgain, bd128, bd64):
    n = x2.shape[0]
    grid = (n // IN_TM, IN_TILES)
    return pl.pallas_call(
        _inproj_kernel,
        grid=grid,
        in_specs=[
            pl.BlockSpec((IN_TM, D_MODEL), lambda i, j: (i, 0)),
            pl.BlockSpec((1, D_MODEL), lambda i, j: (0, 0)),
            pl.BlockSpec((pl.Element(D_MODEL), pl.Element(IN_TN)),
                         lambda i, j: (0, _inproj_w_col(j))),
            pl.BlockSpec(colgain.shape, lambda i, j: (0, 0)),
            pl.BlockSpec((IN_TN, IN_TN), lambda i, j: (0, 0)),
            pl.BlockSpec((IN_TN, IN_TN), lambda i, j: (0, 0)),
        ],
        out_specs=[
            pl.BlockSpec((IN_TM, IN_TN), lambda i, j: (i, jnp.minimum(j, IN_QKV_TILES - 1))),
            pl.BlockSpec((IN_TM, IN_TN),
                         lambda i, j: (i, jnp.clip(j - IN_QKV_TILES, 0, IN_GATE_TILES - 1))),
            pl.BlockSpec((IN_TM, KVX_WIDTH), lambda i, j: (i, 0)),
        ],
        out_shape=[
            jax.ShapeDtypeStruct((n, IN_QKV_TILES * IN_TN), BF16),
            jax.ShapeDtypeStruct((n, IN_GATE_TILES * IN_TN), BF16),
            jax.ShapeDtypeStruct((n, KVX_WIDTH), BF16),
        ],
        scratch_shapes=[pltpu.VMEM((IN_TM, D_MODEL), BF16)],
        compiler_params=pltpu.CompilerParams(
            dimension_semantics=("arbitrary", "arbitrary"),
            vmem_limit_bytes=VMEM_LIMIT_BYTES),
        name="inproj",
    )(x2, g1, w, colgain, bd128, bd64)


def _moba_kernel(q_ref, k_ref, v_ref, bd_ref, bc_ref, o_ref,
                 kmh_ref, kml_ref, vt_ref, qa_ref, s_ref, m_ref, l_ref, acc_ref):
    t = pl.program_id(2)
    L = MOBA_BLOCK
    P = MOBA_PAIR
    Dh = A_HEAD_DIM
    NH = MOBA_HEADS_PER_STEP
    seq = k_ref.shape[1]
    nb = seq // L

    @pl.when(t == 0)
    def _():
        r = lax.broadcasted_iota(jnp.int32, (nb, seq), 0)
        c = lax.broadcasted_iota(jnp.int32, (nb, seq), 1)
        ind = jnp.where(lax.shift_right_logical(c, int(math.log2(L))) == r, 1.0, 0.0).astype(BF16)
        er = lax.broadcasted_iota(jnp.int32, (Dh, Dh), 0)
        ec = lax.broadcasted_iota(jnp.int32, (Dh, Dh), 1)
        eye = jnp.where(er == ec, 1.0, 0.0).astype(BF16)
        for hh in range(NH):
            km = _dot(ind, k_ref[0, :, hh * Dh:(hh + 1) * Dh]) * (1.0 / L)
            hi = km.astype(BF16)
            kmh_ref[hh] = hi
            kml_ref[hh] = (km - hi.astype(F32)).astype(BF16)
            for g in range(seq // P):
                vt_ref[hh, g] = _dot_nt(
                    eye, v_ref[0, g * P:(g + 1) * P, hh * Dh:(hh + 1) * Dh]).astype(BF16)

    lane = lax.broadcasted_iota(jnp.int32, (P, LANES), 1)
    row = lax.broadcasted_iota(jnp.int32, (P, LANES), 0)
    upper = (row >= L).astype(jnp.int32)

    def select(hh):
        q = q_ref[0, :, hh * Dh:(hh + 1) * Dh]
        g = _dot_nt(kmh_ref[hh], q) + _dot_nt(kml_ref[hh], q)
        blk = lax.broadcasted_iota(jnp.int32, g.shape, 0)
        blk_f = blk.astype(F32)
        qcol = lax.broadcasted_iota(jnp.int32, g.shape, 1)
        own_blk = 2 * t + (qcol >= L).astype(jnp.int32)
        past = blk < own_blk
        g = jnp.where(past, g, NEG)
        sel = jnp.full(g.shape, NEG, F32)
        for _ in range(MOBA_TOPK):
            mx = jnp.max(g, axis=0, keepdims=True)
            first = jnp.min(jnp.where(g == mx, blk_f, float(LANES)), axis=0, keepdims=True)
            hit = blk_f == first
            sel = jnp.where(hit & past, 0.0, sel)
            g = jnp.where(hit, -jnp.inf, g)
        sel = jnp.where(blk == own_blk, 0.0, sel)
        sel = jnp.concatenate([sel, jnp.zeros((LANES - nb, P), F32)], axis=0)
        qa_ref[hh, :, :Dh] = q
        qa_ref[hh, :, Dh:] = sel.T.astype(BF16)

    def scores(hh, pair):
        r0 = pl.multiple_of(pair * P, P)
        ind = jnp.where(lane == 2 * pair + upper, 1.0, 0.0).astype(BF16)
        k_aug = jnp.concatenate([k_ref[0, pl.ds(r0, P), hh * Dh:(hh + 1) * Dh], ind], axis=1)
        return _dot_nt(k_aug, qa_ref[hh])

    def accumulate(hh, pair):
        s = s_ref[hh]
        m_old = m_ref[hh]
        m_new = jnp.maximum(m_old, jnp.max(s, axis=0, keepdims=True))
        alpha = jnp.exp2(m_old - m_new)
        p = jnp.exp2(s - m_new)
        l_ref[hh] = alpha * l_ref[hh] + jnp.sum(p, axis=0, keepdims=True)
        acc_ref[hh] = alpha * acc_ref[hh] + _dot(vt_ref[hh, pair], p.astype(BF16))
        m_ref[hh] = m_new

    for hh in range(NH):
        select(hh)
    for hh in range(NH):
        s = scores(hh, t) + bd_ref[hh]
        m0 = jnp.max(s, axis=0, keepdims=True)
        p = jnp.exp2(s - m0)
        m_ref[hh] = m0
        l_ref[hh] = jnp.sum(p, axis=0, keepdims=True)
        acc_ref[hh] = _dot(vt_ref[hh, t], p.astype(BF16))

    @pl.when(t >= 1)
    def _():
        for hh in range(NH):
            s_ref[hh] = scores(hh, t - 1)
            s_ref[hh, P - LANES:, :LANES] += bc_ref[hh]
        for hh in range(NH):
            accumulate(hh, t - 1)

    def far_body(pair, carry):
        for hh in range(NH):
            s_ref[hh] = scores(hh, pair)
        for hh in range(NH):
            accumulate(hh, pair)
        return carry

    lax.fori_loop(0, jnp.maximum(t - 1, 0), far_body, 0)

    for hh in range(NH):
        o_ref[0, :, hh * Dh:(hh + 1) * Dh] = (acc_ref[hh] / l_ref[hh]).T.astype(BF16)


def _moba(qkv3, bias_a):
    b, s, _ = qkv3.shape
    P = MOBA_PAIR
    Dh = A_HEAD_DIM
    NH = MOBA_HEADS_PER_STEP
    G = A_HEADS // NH
    nb = s // MOBA_BLOCK
    assert nb % 16 == 0 and nb <= LANES
    return pl.pallas_call(
        _moba_kernel,
        grid=(b, G, s // P),
        in_specs=[
            pl.BlockSpec((1, P, NH * Dh), lambda bi, h, t: (bi, t, h)),
            pl.BlockSpec((1, s, NH * Dh), lambda bi, h, t: (bi, 0, G + h)),
            pl.BlockSpec((1, s, NH * Dh), lambda bi, h, t: (bi, 0, 2 * G + h)),
            pl.BlockSpec((NH, P, P), lambda bi, h, t: (h, 0, 0)),
            pl.BlockSpec((NH, LANES, LANES), lambda bi, h, t: (h, 1, 2)),
        ],
        out_specs=pl.BlockSpec((1, P, NH * Dh), lambda bi, h, t: (bi, t, h)),
        out_shape=jax.ShapeDtypeStruct((b, s, A_WIDTH), BF16),
        scratch_shapes=[
            pltpu.VMEM((NH, nb, Dh), BF16),
            pltpu.VMEM((NH, nb, Dh), BF16),
            pltpu.VMEM((NH, s // P, Dh, P), BF16),
            pltpu.VMEM((NH, P, Dh + LANES), BF16),
            pltpu.VMEM((NH, P, P), F32),
            pltpu.VMEM((NH, 1, P), F32),
            pltpu.VMEM((NH, 1, P), F32),
            pltpu.VMEM((NH, Dh, P), F32),
        ],
        compiler_params=pltpu.CompilerParams(
            dimension_semantics=("arbitrary", "arbitrary", "arbitrary"),
            vmem_limit_bytes=VMEM_LIMIT_BYTES),
        name="moba",
    )(qkv3, qkv3, qkv3, bias_a, bias_a)


def _swa_kernel(q_ref, kvx_ref, bias_ref, sink_ref, o_ref, vt_ref):
    t = pl.program_id(1)
    W = WINDOW
    n_win = q_ref.shape[1] // W
    seq = kvx_ref.shape[1]
    pairs_per_kv = B_Q_HEADS // B_KV_HEADS // 2
    n_combo = 2 * B_KV_HEADS
    chunk = 4 * W

    @pl.when(t == 0)
    def _():
        er = lax.broadcasted_iota(jnp.int32, (LANES, LANES), 0)
        ec = lax.broadcasted_iota(jnp.int32, (LANES, LANES), 1)
        eye = jnp.where(er == ec, 1.0, 0.0).astype(BF16)
        for i in range(n_combo):
            col = (n_combo + i) * LANES
            for c in range(seq // chunk):
                vt = _dot_nt(eye, kvx_ref[0, c * chunk:(c + 1) * chunk, col:col + LANES])
                for k in range(chunk // W):
                    vt_ref[i, c * (chunk // W) + k] = vt[:, k * W:(k + 1) * W].astype(BF16)

    def win(w, carry):
        n = t * n_win + w
        q0 = pl.multiple_of(w * W, W)
        sb = jnp.maximum(n - 1, 0)
        start = pl.multiple_of(sb * W, W)
        first = (n == 0).astype(jnp.int32)
        scores = []
        for kv in range(B_KV_HEADS):
            width = pairs_per_kv * LANES
            qg = q_ref[0, pl.ds(q0, W), kv * width:(kv + 1) * width]
            qs = jnp.concatenate(
                [qg[:, k * LANES:(k + 1) * LANES] for k in range(pairs_per_kv)], axis=0)
            for par in range(2):
                kcol = (2 * kv + par) * LANES
                kk = kvx_ref[0, pl.ds(start, 2 * W), kcol:kcol + LANES]
                scores.append(_dot_nt(kk, qs) + bias_ref[first, kv, par])
        outs = []
        for i, s in enumerate(scores):
            sink = sink_ref[pl.ds(i, 1), :]
            m = jnp.maximum(jnp.max(s, axis=0, keepdims=True), sink)
            pe = jnp.exp2(s - m)
            l = jnp.sum(pe, axis=0, keepdims=True) + jnp.exp2(sink - m)
            vt = jnp.concatenate([vt_ref[i, sb], vt_ref[i, sb + 1]], axis=1)
            outs.append(_dot(vt, pe.astype(BF16)) / l)
        for kv in range(B_KV_HEADS):
            ot = outs[2 * kv] + outs[2 * kv + 1]
            for k in range(pairs_per_kv):
                p = kv * pairs_per_kv + k
                o_ref[0, pl.ds(q0, W), p * LANES:(p + 1) * LANES] = (
                    ot[:, k * W:(k + 1) * W].T.astype(BF16))
        return carry

    lax.fori_loop(0, n_win, win, 0, unroll=2)


def _swa(qkv3, kvx3, bias_b, sink_rows):
    b, s, _ = qkv3.shape
    qb_blk = 3 * A_WIDTH // B_WIDTH
    return pl.pallas_call(
        _swa_kernel,
        grid=(b, s // SWA_TQ),
        in_specs=[
            pl.BlockSpec((1, SWA_TQ, B_WIDTH), lambda bi, t: (bi, t, qb_blk)),
            pl.BlockSpec((1, s, KVX_WIDTH), lambda bi, t: (bi, 0, 0)),
            pl.BlockSpec(bias_b.shape, lambda bi, t: (0, 0, 0, 0, 0)),
            pl.BlockSpec(sink_rows.shape, lambda bi, t: (0, 0)),
        ],
        out_specs=pl.BlockSpec((1, SWA_TQ, B_WIDTH), lambda bi, t: (bi, t, 0)),
        out_shape=jax.ShapeDtypeStruct((b, s, B_WIDTH), BF16),
        scratch_shapes=[pltpu.VMEM((2 * B_KV_HEADS, s // WINDOW, LANES, WINDOW), BF16)],
        compiler_params=pltpu.CompilerParams(
            dimension_semantics=("arbitrary", "arbitrary"),
            vmem_limit_bytes=VMEM_LIMIT_BYTES),
        name="swa",
    )(qkv3, kvx3, bias_b, sink_rows)


def _merge_kernel(ya_ref, yb_ref, gt_ref, x_ref, wa_ref, wb_ref, wo_ref, o_ref, mg_ref):
    ch = MERGE_CH
    for c in range(D_MODEL // ch):
        cols = slice(c * ch, (c + 1) * ch)
        ta = _dot(ya_ref[...], wa_ref[:, cols])
        tb = _dot(yb_ref[...], wb_ref[:, cols])
        ga = _sigmoid(gt_ref[:, cols].astype(F32))
        gb = _sigmoid(gt_ref[:, D_MODEL + c * ch:D_MODEL + (c + 1) * ch].astype(F32))
        mg_ref[:, cols] = (ga * ta + gb * tb).astype(BF16)
    for c in range(D_MODEL // ch):
        cols = slice(c * ch, (c + 1) * ch)
        o_ref[:, cols] = x_ref[:, cols] + _dot(mg_ref[...], wo_ref[:, cols])


def _merge(ya, yb, gates, x2, wa, wb, wo):
    n = x2.shape[0]
    tm = MERGE_TM
    resident = functools.partial(pl.BlockSpec, pipeline_mode=pl.Buffered(1))
    return pl.pallas_call(
        _merge_kernel,
        grid=(n // tm,),
        in_specs=[
            pl.BlockSpec((tm, A_WIDTH), lambda i: (i, 0)),
            pl.BlockSpec((tm, B_WIDTH), lambda i: (i, 0)),
            pl.BlockSpec((tm, 2 * D_MODEL), lambda i: (i, 0)),
            pl.BlockSpec((tm, D_MODEL), lambda i: (i, 0)),
            resident((A_WIDTH, D_MODEL), lambda i: (0, 0)),
            resident((B_WIDTH, D_MODEL), lambda i: (0, 0)),
            resident((D_MODEL, D_MODEL), lambda i: (0, 0)),
        ],
        out_specs=pl.BlockSpec((tm, D_MODEL), lambda i: (i, 0)),
        out_shape=jax.ShapeDtypeStruct((n, D_MODEL), F32),
        scratch_shapes=[pltpu.VMEM((tm, D_MODEL), BF16)],
        compiler_params=pltpu.CompilerParams(
            dimension_semantics=("arbitrary",),
            vmem_limit_bytes=VMEM_LIMIT_BYTES),
        name="merge",
    )(ya, yb, gates, x2, wa, wb, wo)


def _ffn_kernel(x_ref, g2_ref, wg_ref, wu_ref, wd_ref, o_ref, h_ref):
    f = pl.program_id(1)

    @pl.when(f == 0)
    def _():
        x = x_ref[...]
        ms = jnp.mean(x * x, axis=-1, keepdims=True)
        h_ref[...] = (x * lax.rsqrt(ms + EPS) * g2_ref[...]).astype(BF16)
        o_ref[...] = x

    h = h_ref[...]
    g = _dot(h, wg_ref[...])
    u = _dot(h, wu_ref[...])
    a = (g * _sigmoid(g) * u).astype(BF16)
    o_ref[...] += _dot(a, wd_ref[...])


def _ffn(x1, g2, wgu, wd):
    n = x1.shape[0]
    tm, tf = FFN_TM, FFN_TF
    nf = D_FF // tf
    return pl.pallas_call(
        _ffn_kernel,
        grid=(n // tm, nf),
        in_specs=[
            pl.BlockSpec((tm, D_MODEL), lambda i, f: (i, 0)),
            pl.BlockSpec((1, D_MODEL), lambda i, f: (0, 0)),
            pl.BlockSpec((D_MODEL, tf), lambda i, f: (0, f)),
            pl.BlockSpec((D_MODEL, tf), lambda i, f: (0, nf + f)),
            pl.BlockSpec((tf, D_MODEL), lambda i, f: (f, 0)),
        ],
        out_specs=pl.BlockSpec((tm, D_MODEL), lambda i, f: (i, 0)),
        out_shape=jax.ShapeDtypeStruct((n, D_MODEL), F32),
        scratch_shapes=[pltpu.VMEM((tm, D_MODEL), BF16)],
        compiler_params=pltpu.CompilerParams(
            dimension_semantics=("arbitrary", "arbitrary"),
            vmem_limit_bytes=VMEM_LIMIT_BYTES),
        name="ffn",
    )(x1, g2, wgu, wgu, wd)


def _moba_bucket_tables():
    L = MOBA_BLOCK
    kj = np.arange(L)[:, None]
    qi = np.arange(L)[None, :]
    return np.stack([_t5_bucket_np(qi - kj), _t5_bucket_np(L + qi - kj)])


def _swa_bias_tables():
    W = WINDOW
    qi = np.arange(W)[None, :]
    c = np.arange(2 * W)[:, None]
    d = W + qi - c
    v = (d >= 0) & (d < W)
    d0 = qi - c
    v0 = (d0 >= 0) & (d0 < W)
    buckets = np.stack([_t5_bucket_np(d), _t5_bucket_np(d0)])
    valid = np.stack([v, v0]).astype(np.int32)
    return buckets, valid


def _block_diag_ones(n, blk):
    r = np.arange(n)
    return (r[:, None] // blk == r[None, :] // blk).astype(np.float32)


def kernel(x, norm1_g, w_in, q_norm_a, k_norm_a, q_norm_b, k_norm_b, rel_bias, sinks,
           w_branch_a, w_branch_b, w_out, norm2_g, w_gate_up, w_down):
    b, s, d = x.shape
    depth = w_in.shape[0]
    assert d == D_MODEL and s % MOBA_PAIR == 0 and s % SWA_TQ == 0
    assert (b * s) % IN_TM == 0 and (b * s) % MERGE_TM == 0 and (b * s) % FFN_TM == 0
    far = _t5_bucket_np(np.arange(LANES + 1, s + MOBA_PAIR))
    assert (far == far[0]).all()
    far_bucket = int(far[0])

    bias_a = _moba_bias(rel_bias, jnp.asarray(_moba_bucket_tables()), far_bucket)
    bk_b, va_b = _swa_bias_tables()
    bias_b = _swa_bias(rel_bias, jnp.asarray(bk_b), jnp.asarray(va_b))
    bd128 = jnp.asarray(_block_diag_ones(IN_TN, A_HEAD_DIM), BF16)
    bd64 = jnp.asarray(_block_diag_ones(IN_TN, B_HEAD_DIM), BF16)

    x2 = x.reshape(b * s, d)
    for l in range(depth):
        ones = jnp.ones((IN_TN,), F32)
        colgain = jnp.stack(
            [jnp.tile(q_norm_a[l] * (A_HEAD_DIM ** -0.5 * LOG2E), IN_TN // A_HEAD_DIM)] * 2
            + [jnp.tile(k_norm_a[l], IN_TN // A_HEAD_DIM)] * 2
            + [ones] * 2
            + [jnp.tile(q_norm_b[l] * (B_HEAD_DIM ** -0.5 * LOG2E), IN_TN // B_HEAD_DIM)] * 2
            + [ones] * IN_GATE_TILES
            + [jnp.concatenate([jnp.tile(k_norm_b[l], B_KV_HEADS),
                                jnp.ones((IN_TN - B_KV_WIDTH,), F32)])]
            + [ones] * (24 - IN_TILES))
        qkv, gates, kvx = _inproj(x2, norm1_g[l][None, :], w_in[l], colgain, bd128, bd64)
        qkv3 = qkv.reshape(b, s, -1)
        ya = _moba(qkv3, bias_a)
        per_kv = B_Q_HEADS // B_KV_HEADS
        sink_rows = jnp.repeat(
            sinks[l].reshape(B_KV_HEADS, per_kv // 2, 2).transpose(0, 2, 1).reshape(2 * B_KV_HEADS, -1),
            WINDOW, axis=1) * LOG2E
        sink_rows = jnp.pad(sink_rows, ((0, 8 - 2 * B_KV_HEADS), (0, 0)))
        yb = _swa(qkv3, kvx.reshape(b, s, -1), bias_b, sink_rows)
        x1 = _merge(ya.reshape(b * s, -1), yb.reshape(b * s, -1), gates, x2,
                    w_branch_a[l].astype(BF16), w_branch_b[l].astype(BF16),
                    w_out[l].astype(BF16))
        x2 = _ffn(x1, norm2_g[l][None, :], w_gate_up[l].astype(BF16), w_down[l].astype(BF16))
    return x2.reshape(b, s, d)
```

```python
import functools
import math

import jax
import jax.numpy as jnp
import numpy as np
from jax import lax
from jax.experimental import pallas as pl
from jax.experimental.pallas import tpu as pltpu

F32 = jnp.float32
BF16 = jnp.bfloat16

D_MODEL = 2048
A_HEADS = 8
A_HEAD_DIM = 128
A_WIDTH = A_HEADS * A_HEAD_DIM
MOBA_BLOCK = 256
MOBA_TOPK = 3
B_Q_HEADS = 16
B_KV_HEADS = 2
B_HEAD_DIM = 64
B_WIDTH = B_Q_HEADS * B_HEAD_DIM
B_KV_WIDTH = B_KV_HEADS * B_HEAD_DIM
WINDOW = 128
NUM_BUCKETS = 32
MAX_DISTANCE = 128
MAX_EXACT = NUM_BUCKETS // 2
D_FF = -(-(8 * D_MODEL) // (3 * 256)) * 256
IN_WIDTH = 3 * A_WIDTH + B_WIDTH + 2 * B_KV_WIDTH + 2 * D_MODEL
EPS = 1e-6
NEG = -1e30
LOG2E = math.log2(math.e)

LANES = 128
VMEM_LIMIT_BYTES = 56 * 1024 * 1024

IN_TN = 512
IN_QKV_TILES = (3 * A_WIDTH + B_WIDTH) // IN_TN
IN_GATE_TILES = (2 * D_MODEL) // IN_TN
IN_TILES = IN_QKV_TILES + IN_GATE_TILES + 1
IN_KV_COL = 3 * A_WIDTH + B_WIDTH
IN_GATE_COL = IN_KV_COL + 2 * B_KV_WIDTH
IN_TM = 1024
KVX_WIDTH = 8 * LANES

MOBA_PAIR = 2 * MOBA_BLOCK
MOBA_HEADS_PER_STEP = 4
SWA_TQ = 512
MERGE_TM = 512
MERGE_CH = 512
FFN_TM = 1024
FFN_TF = 512


def _dot(a, b):
    return jnp.dot(a, b, preferred_element_type=F32)


def _dot_nt(a, b):
    return lax.dot_general(a, b, (((1,), (1,)), ((), ())), preferred_element_type=F32)


def _sigmoid(v):
    return 0.5 * jnp.tanh(0.5 * v) + 0.5


def _t5_bucket_np(dist):
    n = np.maximum(dist, 0)
    nf = np.maximum(n, 1).astype(np.float32)
    large = MAX_EXACT + (np.log(nf / np.float32(MAX_EXACT))
                         / np.float32(math.log(MAX_DISTANCE / MAX_EXACT))
                         * np.float32(NUM_BUCKETS - MAX_EXACT)).astype(np.int32)
    large = np.minimum(large, NUM_BUCKETS - 1)
    return np.where(n < MAX_EXACT, n, large).astype(np.int32)


def _bucket_lookup(tab_ref, bm, h):
    acc = jnp.zeros(bm.shape, F32)
    for b in range(NUM_BUCKETS):
        acc = jnp.where(bm == b, tab_ref[b, h], acc)
    return acc


def _swa_bias_kernel(tab_ref, bucket_ref, valid_ref, o_ref):
    h = pl.program_id(0) + A_HEADS
    for t in range(bucket_ref.shape[0]):
        o_ref[t, 0, 0] = jnp.where(valid_ref[t] != 0,
                                   _bucket_lookup(tab_ref, bucket_ref[t], h) * LOG2E, NEG)


def _swa_bias(rel_bias, buckets, valid):
    n_tiles, r, c = buckets.shape
    per_kv = B_Q_HEADS // B_KV_HEADS
    return pl.pallas_call(
        _swa_bias_kernel,
        grid=(B_Q_HEADS,),
        in_specs=[
            pl.BlockSpec(memory_space=pltpu.SMEM),
            pl.BlockSpec((n_tiles, r, c), lambda h: (0, 0, 0)),
            pl.BlockSpec((n_tiles, r, c), lambda h: (0, 0, 0)),
        ],
        out_specs=pl.BlockSpec(
            (n_tiles, 1, 1, r, c),
            lambda h: (0, h // per_kv, h % 2, 0, (h % per_kv) // 2)),
        out_shape=jax.ShapeDtypeStruct((n_tiles, B_KV_HEADS, 2, r, c * per_kv // 2), F32),
        name="swa_bias",
    )(rel_bias, buckets, valid)


def _moba_bias_kernel(tab_ref, bucket_ref, o_ref, *, shift_bucket):
    h = pl.program_id(0)
    L = MOBA_BLOCK
    shift = tab_ref[shift_bucket, h]
    tiles = [(_bucket_lookup(tab_ref, bucket_ref[t], h) - shift) * LOG2E for t in range(2)]
    kk = lax.broadcasted_iota(jnp.int32, (L, L), 0)
    qq = lax.broadcasted_iota(jnp.int32, (L, L), 1)
    own = jnp.where(qq >= kk, tiles[0], NEG)
    o_ref[0, :L, :L] = own
    o_ref[0, L:, L:] = own
    o_ref[0, :L, L:] = tiles[1]
    o_ref[0, L:, :L] = jnp.full((L, L), NEG, F32)


def _moba_bias(rel_bias, buckets, shift_bucket):
    L = MOBA_BLOCK
    return pl.pallas_call(
        functools.partial(_moba_bias_kernel, shift_bucket=shift_bucket),
        grid=(A_HEADS,),
        in_specs=[
            pl.BlockSpec(memory_space=pltpu.SMEM),
            pl.BlockSpec((2, L, L), lambda h: (0, 0, 0)),
        ],
        out_specs=pl.BlockSpec((1, 2 * L, 2 * L), lambda h: (h, 0, 0)),
        out_shape=jax.ShapeDtypeStruct((A_HEADS, 2 * L, 2 * L), F32),
        name="moba_bias",
    )(rel_bias, buckets)


def _inproj_kernel(x_ref, g1_ref, w_ref, cg_ref,
                   qkv_ref, gates_ref, kvx_ref, h_ref):
    j = pl.program_id(1)

    @pl.when(j == 0)
    def _():
        x = x_ref[...]
        ms = jnp.mean(x * x, axis=-1, keepdims=True)
        h_ref[...] = (x * lax.rsqrt(ms + EPS) * g1_ref[...]).astype(BF16)

    cg = cg_ref[pl.ds(j, 1), :]

    def project(cols=slice(None), rows=slice(None)):
        return _dot(h_ref[rows, :], w_ref[:, cols].astype(BF16))

    def normed_tile(dh):
        n_chunks = 4
        rc = IN_TM // n_chunks
        for r in range(n_chunks):
            rows = slice(r * rc, (r + 1) * rc)
            qkv_ref[rows, :] = head_norm(project(rows=rows), dh, cg).astype(BF16)

    def head_norm(v, dh, gain):
        parts = []
        for k in range(v.shape[1] // LANES):
            vk = v[:, k * LANES:(k + 1) * LANES]
            sq = vk * vk
            if dh == LANES:
                r = lax.rsqrt(jnp.sum(sq, axis=-1, keepdims=True) * (1.0 / dh) + EPS)
            else:
                lo = lax.broadcasted_iota(jnp.int32, vk.shape, 1) < dh
                ss_lo = jnp.sum(jnp.where(lo, sq, 0.0), axis=-1, keepdims=True)
                ss_hi = jnp.sum(jnp.where(lo, 0.0, sq), axis=-1, keepdims=True)
                r = jnp.where(lo, lax.rsqrt(ss_lo * (1.0 / dh) + EPS),
                              lax.rsqrt(ss_hi * (1.0 / dh) + EPS))
            parts.append(vk * r)
        return jnp.concatenate(parts, axis=1) * gain

    n_a = 2 * A_WIDTH // IN_TN
    n_va = 3 * A_WIDTH // IN_TN

    @pl.when(j < n_a)
    def _():
        normed_tile(A_HEAD_DIM)

    @pl.when((j >= n_a) & (j < n_va))
    def _():
        qkv_ref[...] = project().astype(BF16)

    @pl.when((j >= n_va) & (j < IN_QKV_TILES))
    def _():
        normed_tile(B_HEAD_DIM)

    @pl.when((j >= IN_QKV_TILES) & (j < IN_QKV_TILES + IN_GATE_TILES))
    def _():
        gates_ref[...] = project().astype(BF16)

    @pl.when(j == IN_TILES - 1)
    def _():
        y = project(slice(0, 2 * LANES))
        yk = y[:, :LANES]
        br = lax.broadcasted_iota(jnp.int32, (LANES, LANES), 0) // B_HEAD_DIM
        bc = lax.broadcasted_iota(jnp.int32, (LANES, LANES), 1) // B_HEAD_DIM
        ss = _dot((yk * yk).astype(BF16), jnp.where(br == bc, 1.0, 0.0).astype(BF16))
        kn = yk * lax.rsqrt(ss * (1.0 / B_HEAD_DIM) + EPS) * cg[:, :LANES]
        vv = y[:, LANES:2 * LANES]
        lo = lax.broadcasted_iota(jnp.int32, kn.shape, 1) < B_HEAD_DIM
        for base, t in ((0, kn), (4, vv)):
            e0 = jnp.where(lo, t, 0.0)
            o1 = jnp.where(lo, 0.0, t)
            o0 = pltpu.roll(e0, B_HEAD_DIM, 1)
            e1 = pltpu.roll(o1, B_HEAD_DIM, 1)
            for k, piece in enumerate((e0, o0, e1, o1)):
                kvx_ref[:, (base + k) * LANES:(base + k + 1) * LANES] = piece.astype(BF16)


def _inproj_w_col(j):
    t = IN_TN // LANES
    lane_tile = jnp.where(j < IN_QKV_TILES, j * t,
                          jnp.where(j < IN_QKV_TILES + IN_GATE_TILES,
                                    IN_GATE_COL // LANES + (j - IN_QKV_TILES) * t,
                                    IN_KV_COL // LANES))
    return lane_tile * LANES


def _inproj(x2, g1, w, colgain):
    n = x2.shape[0]
    grid = (n // IN_TM, IN_TILES)
    return pl.pallas_call(
        _inproj_kernel,
        grid=grid,
        in_specs=[
            pl.BlockSpec((IN_TM, D_MODEL), lambda i, j: (i, 0)),
            pl.BlockSpec((1, D_MODEL), lambda i, j: (0, 0)),
            pl.BlockSpec((pl.Element(D_MODEL), pl.Element(IN_TN)),
                         lambda i, j: (0, _inproj_w_col(j))),
            pl.BlockSpec(colgain.shape, lambda i, j: (0, 0)),
        ],
        out_specs=[
            pl.BlockSpec((IN_TM, IN_TN), lambda i, j: (i, jnp.minimum(j, IN_QKV_TILES - 1))),
            pl.BlockSpec((IN_TM, IN_TN),
                         lambda i, j: (i, jnp.clip(j - IN_QKV_TILES, 0, IN_GATE_TILES - 1))),
            pl.BlockSpec((IN_TM, KVX_WIDTH), lambda i, j: (i, 0)),
        ],
        out_shape=[
            jax.ShapeDtypeStruct((n, IN_QKV_TILES * IN_TN), BF16),
            jax.ShapeDtypeStruct((n, IN_GATE_TILES * IN_TN), BF16),
            jax.ShapeDtypeStruct((n, KVX_WIDTH), BF16),
        ],
        scratch_shapes=[pltpu.VMEM((IN_TM, D_MODEL), BF16)],
        compiler_params=pltpu.CompilerParams(
            dimension_semantics=("arbitrary", "arbitrary"),
            vmem_limit_bytes=VMEM_LIMIT_BYTES),
        name="inproj",
    )(x2, g1, w, colgain)


def _moba_kernel(*refs, n_casts):
    q_ref, k_ref, v_ref, bd_ref, bc_ref = refs[:5]
    cast_src = refs[5:5 + n_casts]
    o_ref = refs[5 + n_casts]
    cast_dst = refs[6 + n_casts:6 + 2 * n_casts]
    kmh_ref, kml_ref, vt_ref, qa_ref, s_ref, m_ref, l_ref, acc_ref = refs[6 + 2 * n_casts:]

    t = pl.program_id(2)
    L = MOBA_BLOCK
    P = MOBA_PAIR
    Dh = A_HEAD_DIM
    NH = MOBA_HEADS_PER_STEP
    seq = k_ref.shape[1]
    nb = seq // L

    @pl.when(t == 0)
    def _():
        r = lax.broadcasted_iota(jnp.int32, (nb, seq), 0)
        c = lax.broadcasted_iota(jnp.int32, (nb, seq), 1)
        ind = jnp.where(lax.shift_right_logical(c, int(math.log2(L))) == r, 1.0, 0.0).astype(BF16)
        er = lax.broadcasted_iota(jnp.int32, (Dh, Dh), 0)
        ec = lax.broadcasted_iota(jnp.int32, (Dh, Dh), 1)
        eye = jnp.where(er == ec, 1.0, 0.0).astype(BF16)
        for hh in range(NH):
            km = _dot(ind, k_ref[0, :, hh * Dh:(hh + 1) * Dh]) * (1.0 / L)
            hi = km.astype(BF16)
            kmh_ref[hh] = hi
            kml_ref[hh] = (km - hi.astype(F32)).astype(BF16)
            for g in range(seq // P):
                vt_ref[hh, g] = _dot_nt(
                    eye, v_ref[0, g * P:(g + 1) * P, hh * Dh:(hh + 1) * Dh]).astype(BF16)

    lane = lax.broadcasted_iota(jnp.int32, (P, LANES), 1)
    row = lax.broadcasted_iota(jnp.int32, (P, LANES), 0)
    upper = (row >= L).astype(jnp.int32)

    def select(hh):
        q = q_ref[0, :, hh * Dh:(hh + 1) * Dh]
        g = _dot_nt(kmh_ref[hh], q) + _dot_nt(kml_ref[hh], q)
        blk = lax.broadcasted_iota(jnp.int32, g.shape, 0)
        blk_f = blk.astype(F32)
        qcol = lax.broadcasted_iota(jnp.int32, g.shape, 1)
        own_blk = 2 * t + (qcol >= L).astype(jnp.int32)
        past = blk < own_blk
        g = jnp.where(past, g, NEG)
        sel = jnp.full(g.shape, NEG, F32)
        for _ in range(MOBA_TOPK):
            mx = jnp.max(g, axis=0, keepdims=True)
            first = jnp.min(jnp.where(g == mx, blk_f, float(LANES)), axis=0, keepdims=True)
            hit = blk_f == first
            sel = jnp.where(hit & past, 0.0, sel)
            g = jnp.where(hit, -jnp.inf, g)
        sel = jnp.where(blk == own_blk, 0.0, sel)
        sel = jnp.concatenate([sel, jnp.zeros((LANES - nb, P), F32)], axis=0)
        qa_ref[hh, :, :Dh] = q
        qa_ref[hh, :, Dh:] = sel.T.astype(BF16)

    def scores(hh, pair):
        r0 = pl.multiple_of(pair * P, P)
        ind = jnp.where(lane == 2 * pair + upper, 1.0, 0.0).astype(BF16)
        k_aug = jnp.concatenate([k_ref[0, pl.ds(r0, P), hh * Dh:(hh + 1) * Dh], ind], axis=1)
        return _dot_nt(k_aug, qa_ref[hh])

    def accumulate(hh, pair):
        s = s_ref[hh]
        m_old = m_ref[hh]
        m_new = jnp.maximum(m_old, jnp.max(s, axis=0, keepdims=True))
        alpha = jnp.exp2(m_old - m_new)
        p = jnp.exp2(s - m_new)
        l_ref[hh] = alpha * l_ref[hh] + jnp.sum(p, axis=0, keepdims=True)
        acc_ref[hh] = alpha * acc_ref[hh] + _dot(vt_ref[hh, pair], p.astype(BF16))
        m_ref[hh] = m_new

    for hh in range(NH):
        select(hh)
    for src, dst in zip(cast_src, cast_dst):
        dst[...] = src[...].astype(BF16)
    for hh in range(NH):
        s = scores(hh, t) + bd_ref[hh]
        m0 = jnp.max(s, axis=0, keepdims=True)
        p = jnp.exp2(s - m0)
        m_ref[hh] = m0
        l_ref[hh] = jnp.sum(p, axis=0, keepdims=True)
        acc_ref[hh] = _dot(vt_ref[hh, t], p.astype(BF16))

    @pl.when(t >= 1)
    def _():
        for hh in range(NH):
            s_ref[hh] = scores(hh, t - 1)
            s_ref[hh, P - LANES:, :LANES] += bc_ref[hh]
        for hh in range(NH):
            accumulate(hh, t - 1)

    def far_body(pair, carry):
        for hh in range(NH):
            s_ref[hh] = scores(hh, pair)
        for hh in range(NH):
            accumulate(hh, pair)
        return carry

    lax.fori_loop(0, jnp.maximum(t - 1, 0), far_body, 0)

    for hh in range(NH):
        o_ref[0, :, hh * Dh:(hh + 1) * Dh] = (acc_ref[hh] / l_ref[hh]).T.astype(BF16)


def _cast_slab_shape(w, n_steps):
    cols = w.shape[-1]
    while cols % LANES == 0:
        rows = w.size // cols
        if rows % (n_steps * 16) == 0:
            return rows, cols
        cols //= 2
    raise ValueError(f"no slab view of {w.shape} for {n_steps} steps")


def _moba(qkv3, bias_a, weights):
    b, s, _ = qkv3.shape
    P = MOBA_PAIR
    Dh = A_HEAD_DIM
    NH = MOBA_HEADS_PER_STEP
    G = A_HEADS // NH
    T = s // P
    nb = s // MOBA_BLOCK
    assert nb % 16 == 0 and nb <= LANES
    n_steps = b * G * T
    views = [_cast_slab_shape(w, n_steps) for w in weights]
    slab_specs = [pl.BlockSpec((r // n_steps, c), lambda bi, h, t: ((bi * G + h) * T + t, 0))
                  for r, c in views]
    outs = pl.pallas_call(
        functools.partial(_moba_kernel, n_casts=len(weights)),
        grid=(b, G, T),
        in_specs=[
            pl.BlockSpec((1, P, NH * Dh), lambda bi, h, t: (bi, t, h)),
            pl.BlockSpec((1, s, NH * Dh), lambda bi, h, t: (bi, 0, G + h)),
            pl.BlockSpec((1, s, NH * Dh), lambda bi, h, t: (bi, 0, 2 * G + h)),
            pl.BlockSpec((NH, P, P), lambda bi, h, t: (h, 0, 0)),
            pl.BlockSpec((NH, LANES, LANES), lambda bi, h, t: (h, 1, 2)),
        ] + slab_specs,
        out_specs=[pl.BlockSpec((1, P, NH * Dh), lambda bi, h, t: (bi, t, h))] + slab_specs,
        out_shape=[jax.ShapeDtypeStruct((b, s, A_WIDTH), BF16)]
        + [jax.ShapeDtypeStruct(v, BF16) for v in views],
        scratch_shapes=[
            pltpu.VMEM((NH, nb, Dh), BF16),
            pltpu.VMEM((NH, nb, Dh), BF16),
            pltpu.VMEM((NH, s // P, Dh, P), BF16),
            pltpu.VMEM((NH, P, Dh + LANES), BF16),
            pltpu.VMEM((NH, P, P), F32),
            pltpu.VMEM((NH, 1, P), F32),
            pltpu.VMEM((NH, 1, P), F32),
            pltpu.VMEM((NH, Dh, P), F32),
        ],
        compiler_params=pltpu.CompilerParams(
            dimension_semantics=("arbitrary", "arbitrary", "arbitrary"),
            vmem_limit_bytes=VMEM_LIMIT_BYTES),
        name="moba",
    )(qkv3, qkv3, qkv3, bias_a, bias_a, *[w.reshape(v) for w, v in zip(weights, views)])
    return outs[0], [o.reshape(w.shape) for o, w in zip(outs[1:], weights)]


def _swa_kernel(q_ref, kvx_ref, bias_ref, sink_ref, o_ref, vt_ref):
    t = pl.program_id(1)
    W = WINDOW
    n_win = q_ref.shape[1] // W
    seq = kvx_ref.shape[1]
    pairs_per_kv = B_Q_HEADS // B_KV_HEADS // 2
    n_combo = 2 * B_KV_HEADS
    chunk = 4 * W

    @pl.when(t == 0)
    def _():
        er = lax.broadcasted_iota(jnp.int32, (LANES, LANES), 0)
        ec = lax.broadcasted_iota(jnp.int32, (LANES, LANES), 1)
        eye = jnp.where(er == ec, 1.0, 0.0).astype(BF16)
        for i in range(n_combo):
            col = (n_combo + i) * LANES
            for c in range(seq // chunk):
                vt = _dot_nt(eye, kvx_ref[0, c * chunk:(c + 1) * chunk, col:col + LANES])
                for k in range(chunk // W):
                    vt_ref[i, c * (chunk // W) + k] = vt[:, k * W:(k + 1) * W].astype(BF16)

    def win(w, carry):
        n = t * n_win + w
        q0 = pl.multiple_of(w * W, W)
        sb = jnp.maximum(n - 1, 0)
        start = pl.multiple_of(sb * W, W)
        first = (n == 0).astype(jnp.int32)
        scores = []
        for kv in range(B_KV_HEADS):
            width = pairs_per_kv * LANES
            qg = q_ref[0, pl.ds(q0, W), kv * width:(kv + 1) * width]
            qs = jnp.concatenate(
                [qg[:, k * LANES:(k + 1) * LANES] for k in range(pairs_per_kv)], axis=0)
            for par in range(2):
                kcol = (2 * kv + par) * LANES
                kk = kvx_ref[0, pl.ds(start, 2 * W), kcol:kcol + LANES]
                scores.append(_dot_nt(kk, qs) + bias_ref[first, kv, par])
        outs = []
        for i, s in enumerate(scores):
            sink = sink_ref[pl.ds(i, 1), :]
            m = jnp.maximum(jnp.max(s, axis=0, keepdims=True), sink)
            pe = jnp.exp2(s - m)
            l = jnp.sum(pe, axis=0, keepdims=True) + jnp.exp2(sink - m)
            vt = jnp.concatenate([vt_ref[i, sb], vt_ref[i, sb + 1]], axis=1)
            outs.append(_dot(vt, pe.astype(BF16)) / l)
        for kv in range(B_KV_HEADS):
            ot = outs[2 * kv] + outs[2 * kv + 1]
            for k in range(pairs_per_kv):
                p = kv * pairs_per_kv + k
                o_ref[0, pl.ds(q0, W), p * LANES:(p + 1) * LANES] = (
                    ot[:, k * W:(k + 1) * W].T.astype(BF16))
        return carry

    lax.fori_loop(0, n_win, win, 0, unroll=2)


def _swa(qkv3, kvx3, bias_b, sink_rows):
    b, s, _ = qkv3.shape
    qb_blk = 3 * A_WIDTH // B_WIDTH
    return pl.pallas_call(
        _swa_kernel,
        grid=(b, s // SWA_TQ),
        in_specs=[
            pl.BlockSpec((1, SWA_TQ, B_WIDTH), lambda bi, t: (bi, t, qb_blk)),
            pl.BlockSpec((1, s, KVX_WIDTH), lambda bi, t: (bi, 0, 0)),
            pl.BlockSpec(bias_b.shape, lambda bi, t: (0, 0, 0, 0, 0)),
            pl.BlockSpec(sink_rows.shape, lambda bi, t: (0, 0)),
        ],
        out_specs=pl.BlockSpec((1, SWA_TQ, B_WIDTH), lambda bi, t: (bi, t, 0)),
        out_shape=jax.ShapeDtypeStruct((b, s, B_WIDTH), BF16),
        scratch_shapes=[pltpu.VMEM((2 * B_KV_HEADS, s // WINDOW, LANES, WINDOW), BF16)],
        compiler_params=pltpu.CompilerParams(
            dimension_semantics=("arbitrary", "arbitrary"),
            vmem_limit_bytes=VMEM_LIMIT_BYTES),
        name="swa",
    )(qkv3, kvx3, bias_b, sink_rows)


def _merge_kernel(ya_ref, yb_ref, gt_ref, x_ref, wa_ref, wb_ref, wo_ref, o_ref, mg_ref):
    ch = MERGE_CH
    for c in range(D_MODEL // ch):
        cols = slice(c * ch, (c + 1) * ch)
        ta = _dot(ya_ref[...], wa_ref[:, cols])
        tb = _dot(yb_ref[...], wb_ref[:, cols])
        ga = _sigmoid(gt_ref[:, cols].astype(F32))
        gb = _sigmoid(gt_ref[:, D_MODEL + c * ch:D_MODEL + (c + 1) * ch].astype(F32))
        mg_ref[:, cols] = (ga * ta + gb * tb).astype(BF16)
    for c in range(D_MODEL // ch):
        cols = slice(c * ch, (c + 1) * ch)
        o_ref[:, cols] = x_ref[:, cols] + _dot(mg_ref[...], wo_ref[:, cols])


def _merge(ya, yb, gates, x2, wa, wb, wo):
    n = x2.shape[0]
    tm = MERGE_TM
    resident = functools.partial(pl.BlockSpec, pipeline_mode=pl.Buffered(1))
    return pl.pallas_call(
        _merge_kernel,
        grid=(n // tm,),
        in_specs=[
            pl.BlockSpec((tm, A_WIDTH), lambda i: (i, 0)),
            pl.BlockSpec((tm, B_WIDTH), lambda i: (i, 0)),
            pl.BlockSpec((tm, 2 * D_MODEL), lambda i: (i, 0)),
            pl.BlockSpec((tm, D_MODEL), lambda i: (i, 0)),
            resident((A_WIDTH, D_MODEL), lambda i: (0, 0)),
            resident((B_WIDTH, D_MODEL), lambda i: (0, 0)),
            resident((D_MODEL, D_MODEL), lambda i: (0, 0)),
        ],
        out_specs=pl.BlockSpec((tm, D_MODEL), lambda i: (i, 0)),
        out_shape=jax.ShapeDtypeStruct((n, D_MODEL), F32),
        scratch_shapes=[pltpu.VMEM((tm, D_MODEL), BF16)],
        compiler_params=pltpu.CompilerParams(
            dimension_semantics=("arbitrary",),
            vmem_limit_bytes=VMEM_LIMIT_BYTES),
        name="merge",
    )(ya, yb, gates, x2, wa, wb, wo)


def _ffn_kernel(x_ref, g2_ref, wg_ref, wu_ref, wd_ref, o_ref, h_ref):
    f = pl.program_id(1)

    @pl.when(f == 0)
    def _():
        x = x_ref[...]
        ms = jnp.mean(x * x, axis=-1, keepdims=True)
        h_ref[...] = (x * lax.rsqrt(ms + EPS) * g2_ref[...]).astype(BF16)
        o_ref[...] = x

    h = h_ref[...]
    g = _dot(h, wg_ref[...])
    u = _dot(h, wu_ref[...])
    a = (g * _sigmoid(g) * u).astype(BF16)
    o_ref[...] += _dot(a, wd_ref[...])


def _ffn(x1, g2, wgu, wd):
    n = x1.shape[0]
    tm, tf = FFN_TM, FFN_TF
    nf = D_FF // tf
    return pl.pallas_call(
        _ffn_kernel,
        grid=(n // tm, nf),
        in_specs=[
            pl.BlockSpec((tm, D_MODEL), lambda i, f: (i, 0)),
            pl.BlockSpec((1, D_MODEL), lambda i, f: (0, 0)),
            pl.BlockSpec((D_MODEL, tf), lambda i, f: (0, f)),
            pl.BlockSpec((D_MODEL, tf), lambda i, f: (0, nf + f)),
            pl.BlockSpec((tf, D_MODEL), lambda i, f: (f, 0)),
        ],
        out_specs=pl.BlockSpec((tm, D_MODEL), lambda i, f: (i, 0)),
        out_shape=jax.ShapeDtypeStruct((n, D_MODEL), F32),
        scratch_shapes=[pltpu.VMEM((tm, D_MODEL), BF16)],
        compiler_params=pltpu.CompilerParams(
            dimension_semantics=("arbitrary", "arbitrary"),
            vmem_limit_bytes=VMEM_LIMIT_BYTES),
        name="ffn",
    )(x1, g2, wgu, wgu, wd)


def _moba_bucket_tables():
    L = MOBA_BLOCK
    kj = np.arange(L)[:, None]
    qi = np.arange(L)[None, :]
    return np.stack([_t5_bucket_np(qi - kj), _t5_bucket_np(L + qi - kj)])


def _swa_bias_tables():
    W = WINDOW
    qi = np.arange(W)[None, :]
    c = np.arange(2 * W)[:, None]
    d = W + qi - c
    v = (d >= 0) & (d < W)
    d0 = qi - c
    v0 = (d0 >= 0) & (d0 < W)
    buckets = np.stack([_t5_bucket_np(d), _t5_bucket_np(d0)])
    valid = np.stack([v, v0]).astype(np.int32)
    return buckets, valid


def kernel(x, norm1_g, w_in, q_norm_a, k_norm_a, q_norm_b, k_norm_b, rel_bias, sinks,
           w_branch_a, w_branch_b, w_out, norm2_g, w_gate_up, w_down):
    b, s, d = x.shape
    depth = w_in.shape[0]
    assert d == D_MODEL and s % MOBA_PAIR == 0 and s % SWA_TQ == 0
    assert (b * s) % IN_TM == 0 and (b * s) % MERGE_TM == 0 and (b * s) % FFN_TM == 0
    far = _t5_bucket_np(np.arange(LANES + 1, s + MOBA_PAIR))
    assert (far == far[0]).all()
    far_bucket = int(far[0])

    bias_a = _moba_bias(rel_bias, jnp.asarray(_moba_bucket_tables()), far_bucket)
    bk_b, va_b = _swa_bias_tables()
    bias_b = _swa_bias(rel_bias, jnp.asarray(bk_b), jnp.asarray(va_b))
    x2 = x.reshape(b * s, d)
    for l in range(depth):
        ones = jnp.ones((IN_TN,), F32)
        colgain = jnp.stack(
            [jnp.tile(q_norm_a[l] * (A_HEAD_DIM ** -0.5 * LOG2E), IN_TN // A_HEAD_DIM)] * 2
            + [jnp.tile(k_norm_a[l], IN_TN // A_HEAD_DIM)] * 2
            + [ones] * 2
            + [jnp.tile(q_norm_b[l] * (B_HEAD_DIM ** -0.5 * LOG2E), IN_TN // B_HEAD_DIM)] * 2
            + [ones] * IN_GATE_TILES
            + [jnp.concatenate([jnp.tile(k_norm_b[l], B_KV_HEADS),
                                jnp.ones((IN_TN - B_KV_WIDTH,), F32)])]
            + [ones] * (24 - IN_TILES))
        qkv, gates, kvx = _inproj(x2, norm1_g[l][None, :], w_in[l], colgain)
        qkv3 = qkv.reshape(b, s, -1)
        ya, (wa, wb, wo, wgu, wd) = _moba(
            qkv3, bias_a, [w_branch_a[l], w_branch_b[l], w_out[l], w_gate_up[l], w_down[l]])
        per_kv = B_Q_HEADS // B_KV_HEADS
        sink_rows = jnp.repeat(
            sinks[l].reshape(B_KV_HEADS, per_kv // 2, 2).transpose(0, 2, 1).reshape(2 * B_KV_HEADS, -1),
            WINDOW, axis=1) * LOG2E
        sink_rows = jnp.pad(sink_rows, ((0, 8 - 2 * B_KV_HEADS), (0, 0)))
        yb = _swa(qkv3, kvx.reshape(b, s, -1), bias_b, sink_rows)
        x1 = _merge(ya.reshape(b * s, -1), yb.reshape(b * s, -1), gates, x2, wa, wb, wo)
        x2 = _ffn(x1, norm2_g[l][None, :], wgu, wd)
    return x2.reshape(b, s, d)
```

```python
import functools
import math

import jax
import jax.numpy as jnp
import numpy as np
from jax import lax
from jax.experimental import pallas as pl
from jax.experimental.pallas import tpu as pltpu

F32 = jnp.float32
BF16 = jnp.bfloat16

D_MODEL = 2048
A_HEADS = 8
A_HEAD_DIM = 128
A_WIDTH = A_HEADS * A_HEAD_DIM
MOBA_BLOCK = 256
MOBA_TOPK = 3
B_Q_HEADS = 16
B_KV_HEADS = 2
B_HEAD_DIM = 64
B_WIDTH = B_Q_HEADS * B_HEAD_DIM
B_KV_WIDTH = B_KV_HEADS * B_HEAD_DIM
WINDOW = 128
NUM_BUCKETS = 32
MAX_DISTANCE = 128
MAX_EXACT = NUM_BUCKETS // 2
D_FF = -(-(8 * D_MODEL) // (3 * 256)) * 256
IN_WIDTH = 3 * A_WIDTH + B_WIDTH + 2 * B_KV_WIDTH + 2 * D_MODEL
EPS = 1e-6
NEG = -1e30
LOG2E = math.log2(math.e)

LANES = 128
VMEM_LIMIT_BYTES = 56 * 1024 * 1024

IN_TN = 1024
IN_QKV_TILES = (3 * A_WIDTH + B_WIDTH) // IN_TN
IN_GATE_TILES = (2 * D_MODEL) // IN_TN
IN_TILES = IN_QKV_TILES + IN_GATE_TILES + 1
IN_KV_COL = 3 * A_WIDTH + B_WIDTH
IN_GATE_COL = IN_KV_COL + 2 * B_KV_WIDTH
IN_TM = 1024
KVX_WIDTH = 8 * LANES

MOBA_PAIR = 2 * MOBA_BLOCK
MOBA_HEADS_PER_STEP = 4
SWA_TQ = 512
MERGE_TM = 512
MERGE_CH = 512
FFN_TM = 1024
FFN_TF = 512


def _dot(a, b):
    return jnp.dot(a, b, preferred_element_type=F32)


def _dot_nt(a, b):
    return lax.dot_general(a, b, (((1,), (1,)), ((), ())), preferred_element_type=F32)


def _sigmoid(v):
    return 0.5 * jnp.tanh(0.5 * v) + 0.5


def _t5_bucket_np(dist):
    n = np.maximum(dist, 0)
    nf = np.maximum(n, 1).astype(np.float32)
    large = MAX_EXACT + (np.log(nf / np.float32(MAX_EXACT))
                         / np.float32(math.log(MAX_DISTANCE / MAX_EXACT))
                         * np.float32(NUM_BUCKETS - MAX_EXACT)).astype(np.int32)
    large = np.minimum(large, NUM_BUCKETS - 1)
    return np.where(n < MAX_EXACT, n, large).astype(np.int32)


def _bucket_lookup(tab_ref, bm, h):
    acc = jnp.zeros(bm.shape, F32)
    for b in range(NUM_BUCKETS):
        acc = jnp.where(bm == b, tab_ref[b, h], acc)
    return acc


def _swa_bias_kernel(tab_ref, bucket_ref, valid_ref, o_ref):
    h = pl.program_id(0) + A_HEADS
    for t in range(bucket_ref.shape[0]):
        o_ref[t, 0, 0] = jnp.where(valid_ref[t] != 0,
                                   _bucket_lookup(tab_ref, bucket_ref[t], h) * LOG2E, NEG)


def _swa_bias(rel_bias, buckets, valid):
    n_tiles, r, c = buckets.shape
    per_kv = B_Q_HEADS // B_KV_HEADS
    return pl.pallas_call(
        _swa_bias_kernel,
        grid=(B_Q_HEADS,),
        in_specs=[
            pl.BlockSpec(memory_space=pltpu.SMEM),
            pl.BlockSpec((n_tiles, r, c), lambda h: (0, 0, 0)),
            pl.BlockSpec((n_tiles, r, c), lambda h: (0, 0, 0)),
        ],
        out_specs=pl.BlockSpec(
            (n_tiles, 1, 1, r, c),
            lambda h: (0, h // per_kv, h % 2, 0, (h % per_kv) // 2)),
        out_shape=jax.ShapeDtypeStruct((n_tiles, B_KV_HEADS, 2, r, c * per_kv // 2), F32),
        name="swa_bias",
    )(rel_bias, buckets, valid)


def _moba_bias_kernel(tab_ref, bucket_ref, o_ref, *, shift_bucket):
    h = pl.program_id(0)
    L = MOBA_BLOCK
    shift = tab_ref[shift_bucket, h]
    tiles = [(_bucket_lookup(tab_ref, bucket_ref[t], h) - shift) * LOG2E for t in range(2)]
    kk = lax.broadcasted_iota(jnp.int32, (L, L), 0)
    qq = lax.broadcasted_iota(jnp.int32, (L, L), 1)
    own = jnp.where(qq >= kk, tiles[0], NEG)
    o_ref[0, :L, :L] = own
    o_ref[0, L:, L:] = own
    o_ref[0, :L, L:] = tiles[1]
    o_ref[0, L:, :L] = jnp.full((L, L), NEG, F32)


def _moba_bias(rel_bias, buckets, shift_bucket):
    L = MOBA_BLOCK
    return pl.pallas_call(
        functools.partial(_moba_bias_kernel, shift_bucket=shift_bucket),
        grid=(A_HEADS,),
        in_specs=[
            pl.BlockSpec(memory_space=pltpu.SMEM),
            pl.BlockSpec((2, L, L), lambda h: (0, 0, 0)),
        ],
        out_specs=pl.BlockSpec((1, 2 * L, 2 * L), lambda h: (h, 0, 0)),
        out_shape=jax.ShapeDtypeStruct((A_HEADS, 2 * L, 2 * L), F32),
        name="moba_bias",
    )(rel_bias, buckets)


def _inproj_kernel(x_ref, g1_ref, w_ref, cg_ref,
                   qkv_ref, gates_ref, kvx_ref, h_ref):
    j = pl.program_id(1)

    @pl.when(j == 0)
    def _():
        x = x_ref[...]
        ms = jnp.mean(x * x, axis=-1, keepdims=True)
        h_ref[...] = (x * lax.rsqrt(ms + EPS) * g1_ref[...]).astype(BF16)

    cg = cg_ref[pl.ds(j, 1), :]

    def project(cols=slice(None), rows=slice(None)):
        return _dot(h_ref[rows, :], w_ref[:, cols].astype(BF16))

    def normed_tile(dh):
        n_chunks = 4
        rc = IN_TM // n_chunks
        for r in range(n_chunks):
            rows = slice(r * rc, (r + 1) * rc)
            qkv_ref[rows, :] = head_norm(project(rows=rows), dh, cg).astype(BF16)

    def head_norm(v, dh, gain):
        parts = []
        for k in range(v.shape[1] // LANES):
            vk = v[:, k * LANES:(k + 1) * LANES]
            sq = vk * vk
            if dh == LANES:
                r = lax.rsqrt(jnp.sum(sq, axis=-1, keepdims=True) * (1.0 / dh) + EPS)
            else:
                lo = lax.broadcasted_iota(jnp.int32, vk.shape, 1) < dh
                ss_lo = jnp.sum(jnp.where(lo, sq, 0.0), axis=-1, keepdims=True)
                ss_hi = jnp.sum(jnp.where(lo, 0.0, sq), axis=-1, keepdims=True)
                r = jnp.where(lo, lax.rsqrt(ss_lo * (1.0 / dh) + EPS),
                              lax.rsqrt(ss_hi * (1.0 / dh) + EPS))
            parts.append(vk * r)
        return jnp.concatenate(parts, axis=1) * gain

    n_a = 2 * A_WIDTH // IN_TN
    n_va = 3 * A_WIDTH // IN_TN

    @pl.when(j < n_a)
    def _():
        normed_tile(A_HEAD_DIM)

    @pl.when((j >= n_a) & (j < n_va))
    def _():
        qkv_ref[...] = project().astype(BF16)

    @pl.when((j >= n_va) & (j < IN_QKV_TILES))
    def _():
        normed_tile(B_HEAD_DIM)

    @pl.when((j >= IN_QKV_TILES) & (j < IN_QKV_TILES + IN_GATE_TILES))
    def _():
        gates_ref[...] = project().astype(BF16)

    @pl.when(j == IN_TILES - 1)
    def _():
        y = project(slice(0, 2 * LANES))
        yk = y[:, :LANES]
        br = lax.broadcasted_iota(jnp.int32, (LANES, LANES), 0) // B_HEAD_DIM
        bc = lax.broadcasted_iota(jnp.int32, (LANES, LANES), 1) // B_HEAD_DIM
        ss = _dot((yk * yk).astype(BF16), jnp.where(br == bc, 1.0, 0.0).astype(BF16))
        kn = yk * lax.rsqrt(ss * (1.0 / B_HEAD_DIM) + EPS) * cg[:, :LANES]
        vv = y[:, LANES:2 * LANES]
        lo = lax.broadcasted_iota(jnp.int32, kn.shape, 1) < B_HEAD_DIM
        for base, t in ((0, kn), (4, vv)):
            e0 = jnp.where(lo, t, 0.0)
            o1 = jnp.where(lo, 0.0, t)
            o0 = pltpu.roll(e0, B_HEAD_DIM, 1)
            e1 = pltpu.roll(o1, B_HEAD_DIM, 1)
            for k, piece in enumerate((e0, o0, e1, o1)):
                kvx_ref[:, (base + k) * LANES:(base + k + 1) * LANES] = piece.astype(BF16)


def _inproj_w_col(j):
    t = IN_TN // LANES
    lane_tile = jnp.where(j < IN_QKV_TILES, j * t,
                          jnp.where(j < IN_QKV_TILES + IN_GATE_TILES,
                                    IN_GATE_COL // LANES + (j - IN_QKV_TILES) * t,
                                    IN_KV_COL // LANES))
    return lane_tile * LANES


def _inproj_col_gains(q_norm_a, k_norm_a, q_norm_b, k_norm_b):
    ones = jnp.ones((IN_TN,), F32)
    rows = (
        [jnp.tile(q_norm_a * (A_HEAD_DIM ** -0.5 * LOG2E), IN_TN // A_HEAD_DIM)]
        * (A_WIDTH // IN_TN)
        + [jnp.tile(k_norm_a, IN_TN // A_HEAD_DIM)] * (A_WIDTH // IN_TN)
        + [ones] * (A_WIDTH // IN_TN)
        + [jnp.tile(q_norm_b * (B_HEAD_DIM ** -0.5 * LOG2E), IN_TN // B_HEAD_DIM)]
        * (B_WIDTH // IN_TN)
        + [ones] * IN_GATE_TILES
        + [jnp.concatenate([jnp.tile(k_norm_b, B_KV_HEADS),
                            jnp.ones((IN_TN - B_KV_WIDTH,), F32)])])
    assert len(rows) == IN_TILES
    return jnp.stack(rows + [ones] * (-IN_TILES % 8))


def _inproj(x2, g1, w, colgain):
    n = x2.shape[0]
    grid = (n // IN_TM, IN_TILES)
    return pl.pallas_call(
        _inproj_kernel,
        grid=grid,
        in_specs=[
            pl.BlockSpec((IN_TM, D_MODEL), lambda i, j: (i, 0)),
            pl.BlockSpec((1, D_MODEL), lambda i, j: (0, 0)),
            pl.BlockSpec((pl.Element(D_MODEL), pl.Element(IN_TN)),
                         lambda i, j: (0, _inproj_w_col(j))),
            pl.BlockSpec(colgain.shape, lambda i, j: (0, 0)),
        ],
        out_specs=[
            pl.BlockSpec((IN_TM, IN_TN), lambda i, j: (i, jnp.minimum(j, IN_QKV_TILES - 1))),
            pl.BlockSpec((IN_TM, IN_TN),
                         lambda i, j: (i, jnp.clip(j - IN_QKV_TILES, 0, IN_GATE_TILES - 1))),
            pl.BlockSpec((IN_TM, KVX_WIDTH), lambda i, j: (i, 0)),
        ],
        out_shape=[
            jax.ShapeDtypeStruct((n, IN_QKV_TILES * IN_TN), BF16),
            jax.ShapeDtypeStruct((n, IN_GATE_TILES * IN_TN), BF16),
            jax.ShapeDtypeStruct((n, KVX_WIDTH), BF16),
        ],
        scratch_shapes=[pltpu.VMEM((IN_TM, D_MODEL), BF16)],
        compiler_params=pltpu.CompilerParams(
            dimension_semantics=("arbitrary", "arbitrary"),
            vmem_limit_bytes=VMEM_LIMIT_BYTES),
        name="inproj",
    )(x2, g1, w, colgain)


def _moba_kernel(*refs, n_casts):
    q_ref, k_ref, v_ref, bd_ref, bc_ref = refs[:5]
    cast_src = refs[5:5 + n_casts]
    o_ref = refs[5 + n_casts]
    cast_dst = refs[6 + n_casts:6 + 2 * n_casts]
    kmh_ref, kml_ref, vt_ref, qa_ref, s_ref, m_ref, l_ref, acc_ref = refs[6 + 2 * n_casts:]

    t = pl.program_id(2)
    L = MOBA_BLOCK
    P = MOBA_PAIR
    Dh = A_HEAD_DIM
    NH = MOBA_HEADS_PER_STEP
    seq = k_ref.shape[1]
    nb = seq // L

    @pl.when(t == 0)
    def _():
        r = lax.broadcasted_iota(jnp.int32, (nb, seq), 0)
        c = lax.broadcasted_iota(jnp.int32, (nb, seq), 1)
        ind = jnp.where(lax.shift_right_logical(c, int(math.log2(L))) == r, 1.0, 0.0).astype(BF16)
        er = lax.broadcasted_iota(jnp.int32, (Dh, Dh), 0)
        ec = lax.broadcasted_iota(jnp.int32, (Dh, Dh), 1)
        eye = jnp.where(er == ec, 1.0, 0.0).astype(BF16)
        for hh in range(NH):
            km = _dot(ind, k_ref[0, :, hh * Dh:(hh + 1) * Dh]) * (1.0 / L)
            hi = km.astype(BF16)
            kmh_ref[hh] = hi
            kml_ref[hh] = (km - hi.astype(F32)).astype(BF16)
            for g in range(seq // P):
                vt_ref[hh, g] = _dot_nt(
                    eye, v_ref[0, g * P:(g + 1) * P, hh * Dh:(hh + 1) * Dh]).astype(BF16)

    lane = lax.broadcasted_iota(jnp.int32, (P, LANES), 1)
    row = lax.broadcasted_iota(jnp.int32, (P, LANES), 0)
    upper = (row >= L).astype(jnp.int32)

    def select(hh):
        q = q_ref[0, :, hh * Dh:(hh + 1) * Dh]
        g = _dot_nt(kmh_ref[hh], q) + _dot_nt(kml_ref[hh], q)
        blk = lax.broadcasted_iota(jnp.int32, g.shape, 0)
        blk_f = blk.astype(F32)
        qcol = lax.broadcasted_iota(jnp.int32, g.shape, 1)
        own_blk = 2 * t + (qcol >= L).astype(jnp.int32)
        past = blk < own_blk
        g = jnp.where(past, g, NEG)
        sel = jnp.full(g.shape, NEG, F32)
        for _ in range(MOBA_TOPK):
            mx = jnp.max(g, axis=0, keepdims=True)
            first = jnp.min(jnp.where(g == mx, blk_f, float(LANES)), axis=0, keepdims=True)
            hit = blk_f == first
            sel = jnp.where(hit & past, 0.0, sel)
            g = jnp.where(hit, -jnp.inf, g)
        sel = jnp.where(blk == own_blk, 0.0, sel)
        sel = jnp.concatenate([sel, jnp.zeros((LANES - nb, P), F32)], axis=0)
        qa_ref[hh, :, :Dh] = q
        qa_ref[hh, :, Dh:] = sel.T.astype(BF16)

    def scores(hh, pair):
        r0 = pl.multiple_of(pair * P, P)
        ind = jnp.where(lane == 2 * pair + upper, 1.0, 0.0).astype(BF16)
        k_aug = jnp.concatenate([k_ref[0, pl.ds(r0, P), hh * Dh:(hh + 1) * Dh], ind], axis=1)
        return _dot_nt(k_aug, qa_ref[hh])

    def accumulate(hh, pair):
        s = s_ref[hh]
        m_old = m_ref[hh]
        m_new = jnp.maximum(m_old, jnp.max(s, axis=0, keepdims=True))
        alpha = jnp.exp2(m_old - m_new)
        p = jnp.exp2(s - m_new)
        l_ref[hh] = alpha * l_ref[hh] + jnp.sum(p, axis=0, keepdims=True)
        acc_ref[hh] = alpha * acc_ref[hh] + _dot(vt_ref[hh, pair], p.astype(BF16))
        m_ref[hh] = m_new

    for hh in range(NH):
        select(hh)
    for src, dst in zip(cast_src, cast_dst):
        dst[...] = src[...].astype(BF16)
    for hh in range(NH):
        s = scores(hh, t) + bd_ref[hh]
        m0 = jnp.max(s, axis=0, keepdims=True)
        p = jnp.exp2(s - m0)
        m_ref[hh] = m0
        l_ref[hh] = jnp.sum(p, axis=0, keepdims=True)
        acc_ref[hh] = _dot(vt_ref[hh, t], p.astype(BF16))

    @pl.when(t >= 1)
    def _():
        for hh in range(NH):
            s_ref[hh] = scores(hh, t - 1)
            s_ref[hh, P - LANES:, :LANES] += bc_ref[hh]
        for hh in range(NH):
            accumulate(hh, t - 1)

    def far_body(pair, carry):
        for hh in range(NH):
            s_ref[hh] = scores(hh, pair)
        for hh in range(NH):
            accumulate(hh, pair)
        return carry

    lax.fori_loop(0, jnp.maximum(t - 1, 0), far_body, 0)

    for hh in range(NH):
        o_ref[0, :, hh * Dh:(hh + 1) * Dh] = (acc_ref[hh] / l_ref[hh]).T.astype(BF16)


def _cast_slab_grid(rows, cols, n_steps):
    nc = 1
    while nc <= n_steps:
        nr = n_steps // nc
        if rows % (nr * 16) == 0 and cols % (nc * LANES) == 0:
            return nr, nc
        nc *= 2
    raise ValueError(f"no slab split of {(rows, cols)} for {n_steps} steps")


def _moba(qkv3, bias_a, weights, layer):
    b, s, _ = qkv3.shape
    P = MOBA_PAIR
    Dh = A_HEAD_DIM
    NH = MOBA_HEADS_PER_STEP
    G = A_HEADS // NH
    T = s // P
    nb = s // MOBA_BLOCK
    assert nb % 16 == 0 and nb <= LANES
    n_steps = b * G * T
    src_specs, dst_specs, dst_shapes = [], [], []
    for w in weights:
        _, r, c = w.shape
        nr, nc = _cast_slab_grid(r, c, n_steps)

        def slab(bi, h, t, nc=nc):
            step = (bi * G + h) * T + t
            return step // nc, step % nc

        src_specs.append(pl.BlockSpec((None, r // nr, c // nc),
                                      lambda bi, h, t, slab=slab: (layer, *slab(bi, h, t))))
        dst_specs.append(pl.BlockSpec((r // nr, c // nc), slab))
        dst_shapes.append(jax.ShapeDtypeStruct((r, c), BF16))
    outs = pl.pallas_call(
        functools.partial(_moba_kernel, n_casts=len(weights)),
        grid=(b, G, T),
        in_specs=[
            pl.BlockSpec((1, P, NH * Dh), lambda bi, h, t: (bi, t, h)),
            pl.BlockSpec((1, s, NH * Dh), lambda bi, h, t: (bi, 0, G + h)),
            pl.BlockSpec((1, s, NH * Dh), lambda bi, h, t: (bi, 0, 2 * G + h)),
            pl.BlockSpec((NH, P, P), lambda bi, h, t: (h, 0, 0)),
            pl.BlockSpec((NH, LANES, LANES), lambda bi, h, t: (h, 1, 2)),
        ] + src_specs,
        out_specs=[pl.BlockSpec((1, P, NH * Dh), lambda bi, h, t: (bi, t, h))] + dst_specs,
        out_shape=[jax.ShapeDtypeStruct((b, s, A_WIDTH), BF16)] + dst_shapes,
        scratch_shapes=[
            pltpu.VMEM((NH, nb, Dh), BF16),
            pltpu.VMEM((NH, nb, Dh), BF16),
            pltpu.VMEM((NH, s // P, Dh, P), BF16),
            pltpu.VMEM((NH, P, Dh + LANES), BF16),
            pltpu.VMEM((NH, P, P), F32),
            pltpu.VMEM((NH, 1, P), F32),
            pltpu.VMEM((NH, 1, P), F32),
            pltpu.VMEM((NH, Dh, P), F32),
        ],
        compiler_params=pltpu.CompilerParams(
            dimension_semantics=("arbitrary", "arbitrary", "arbitrary"),
            vmem_limit_bytes=VMEM_LIMIT_BYTES),
        name="moba",
    )(qkv3, qkv3, qkv3, bias_a, bias_a, *weights)
    return outs[0], outs[1:]


def _swa_kernel(q_ref, kvx_ref, bias_ref, sink_ref, o_ref, vt_ref):
    t = pl.program_id(1)
    W = WINDOW
    n_win = q_ref.shape[1] // W
    seq = kvx_ref.shape[1]
    pairs_per_kv = B_Q_HEADS // B_KV_HEADS // 2
    n_combo = 2 * B_KV_HEADS
    chunk = 4 * W

    @pl.when(t == 0)
    def _():
        er = lax.broadcasted_iota(jnp.int32, (LANES, LANES), 0)
        ec = lax.broadcasted_iota(jnp.int32, (LANES, LANES), 1)
        eye = jnp.where(er == ec, 1.0, 0.0).astype(BF16)
        for i in range(n_combo):
            col = (n_combo + i) * LANES
            for c in range(seq // chunk):
                vt = _dot_nt(eye, kvx_ref[0, c * chunk:(c + 1) * chunk, col:col + LANES])
                for k in range(chunk // W):
                    vt_ref[i, c * (chunk // W) + k] = vt[:, k * W:(k + 1) * W].astype(BF16)

    def win(w, carry):
        n = t * n_win + w
        q0 = pl.multiple_of(w * W, W)
        sb = jnp.maximum(n - 1, 0)
        start = pl.multiple_of(sb * W, W)
        first = (n == 0).astype(jnp.int32)
        scores = []
        for kv in range(B_KV_HEADS):
            width = pairs_per_kv * LANES
            qg = q_ref[0, pl.ds(q0, W), kv * width:(kv + 1) * width]
            qs = jnp.concatenate(
                [qg[:, k * LANES:(k + 1) * LANES] for k in range(pairs_per_kv)], axis=0)
            for par in range(2):
                kcol = (2 * kv + par) * LANES
                kk = kvx_ref[0, pl.ds(start, 2 * W), kcol:kcol + LANES]
                scores.append(_dot_nt(kk, qs) + bias_ref[first, kv, par])
        outs = []
        for i, s in enumerate(scores):
            sink = sink_ref[pl.ds(i, 1), :]
            m = jnp.maximum(jnp.max(s, axis=0, keepdims=True), sink)
            pe = jnp.exp2(s - m)
            l = jnp.sum(pe, axis=0, keepdims=True) + jnp.exp2(sink - m)
            vt = jnp.concatenate([vt_ref[i, sb], vt_ref[i, sb + 1]], axis=1)
            outs.append(_dot(vt, pe.astype(BF16)) / l)
        for kv in range(B_KV_HEADS):
            ot = outs[2 * kv] + outs[2 * kv + 1]
            for k in range(pairs_per_kv):
                p = kv * pairs_per_kv + k
                o_ref[0, pl.ds(q0, W), p * LANES:(p + 1) * LANES] = (
                    ot[:, k * W:(k + 1) * W].T.astype(BF16))
        return carry

    lax.fori_loop(0, n_win, win, 0, unroll=4)


def _swa(qkv3, kvx3, bias_b, sink_rows):
    b, s, _ = qkv3.shape
    qb_blk = 3 * A_WIDTH // B_WIDTH
    return pl.pallas_call(
        _swa_kernel,
        grid=(b, s // SWA_TQ),
        in_specs=[
            pl.BlockSpec((1, SWA_TQ, B_WIDTH), lambda bi, t: (bi, t, qb_blk)),
            pl.BlockSpec((1, s, KVX_WIDTH), lambda bi, t: (bi, 0, 0)),
            pl.BlockSpec(bias_b.shape, lambda bi, t: (0, 0, 0, 0, 0)),
            pl.BlockSpec(sink_rows.shape, lambda bi, t: (0, 0)),
        ],
        out_specs=pl.BlockSpec((1, SWA_TQ, B_WIDTH), lambda bi, t: (bi, t, 0)),
        out_shape=jax.ShapeDtypeStruct((b, s, B_WIDTH), BF16),
        scratch_shapes=[pltpu.VMEM((2 * B_KV_HEADS, s // WINDOW, LANES, WINDOW), BF16)],
        compiler_params=pltpu.CompilerParams(
            dimension_semantics=("arbitrary", "arbitrary"),
            vmem_limit_bytes=VMEM_LIMIT_BYTES),
        name="swa",
    )(qkv3, kvx3, bias_b, sink_rows)


def _merge_kernel(ya_ref, yb_ref, gt_ref, x_ref, wa_ref, wb_ref, wo_ref, o_ref, mg_ref):
    ch = MERGE_CH
    for c in range(D_MODEL // ch):
        cols = slice(c * ch, (c + 1) * ch)
        ta = _dot(ya_ref[...], wa_ref[:, cols])
        tb = _dot(yb_ref[...], wb_ref[:, cols])
        ga = _sigmoid(gt_ref[:, cols].astype(F32))
        gb = _sigmoid(gt_ref[:, D_MODEL + c * ch:D_MODEL + (c + 1) * ch].astype(F32))
        mg_ref[:, cols] = (ga * ta + gb * tb).astype(BF16)
    for c in range(D_MODEL // ch):
        cols = slice(c * ch, (c + 1) * ch)
        o_ref[:, cols] = x_ref[:, cols] + _dot(mg_ref[...], wo_ref[:, cols])


def _merge(ya, yb, gates, x2, wa, wb, wo):
    n = x2.shape[0]
    tm = MERGE_TM
    resident = functools.partial(pl.BlockSpec, pipeline_mode=pl.Buffered(1))
    return pl.pallas_call(
        _merge_kernel,
        grid=(n // tm,),
        in_specs=[
            pl.BlockSpec((tm, A_WIDTH), lambda i: (i, 0)),
            pl.BlockSpec((tm, B_WIDTH), lambda i: (i, 0)),
            pl.BlockSpec((tm, 2 * D_MODEL), lambda i: (i, 0)),
            pl.BlockSpec((tm, D_MODEL), lambda i: (i, 0)),
            resident((A_WIDTH, D_MODEL), lambda i: (0, 0)),
            resident((B_WIDTH, D_MODEL), lambda i: (0, 0)),
            resident((D_MODEL, D_MODEL), lambda i: (0, 0)),
        ],
        out_specs=pl.BlockSpec((tm, D_MODEL), lambda i: (i, 0)),
        out_shape=jax.ShapeDtypeStruct((n, D_MODEL), F32),
        scratch_shapes=[pltpu.VMEM((tm, D_MODEL), BF16)],
        compiler_params=pltpu.CompilerParams(
            dimension_semantics=("arbitrary",),
            vmem_limit_bytes=VMEM_LIMIT_BYTES),
        name="merge",
    )(ya, yb, gates, x2, wa, wb, wo)


def _ffn_kernel(x_ref, g2_ref, wg_ref, wu_ref, wd_ref, o_ref, h_ref):
    f = pl.program_id(1)

    @pl.when(f == 0)
    def _():
        x = x_ref[...]
        ms = jnp.mean(x * x, axis=-1, keepdims=True)
        h_ref[...] = (x * lax.rsqrt(ms + EPS) * g2_ref[...]).astype(BF16)
        o_ref[...] = x

    h = h_ref[...]
    g = _dot(h, wg_ref[...])
    u = _dot(h, wu_ref[...])
    a = (g * _sigmoid(g) * u).astype(BF16)
    o_ref[...] += _dot(a, wd_ref[...])


def _ffn(x1, g2, wgu, wd):
    n = x1.shape[0]
    tm, tf = FFN_TM, FFN_TF
    nf = D_FF // tf
    return pl.pallas_call(
        _ffn_kernel,
        grid=(n // tm, nf),
        in_specs=[
            pl.BlockSpec((tm, D_MODEL), lambda i, f: (i, 0)),
            pl.BlockSpec((1, D_MODEL), lambda i, f: (0, 0)),
            pl.BlockSpec((D_MODEL, tf), lambda i, f: (0, f)),
            pl.BlockSpec((D_MODEL, tf), lambda i, f: (0, nf + f)),
            pl.BlockSpec((tf, D_MODEL), lambda i, f: (f, 0)),
        ],
        out_specs=pl.BlockSpec((tm, D_MODEL), lambda i, f: (i, 0)),
        out_shape=jax.ShapeDtypeStruct((n, D_MODEL), F32),
        scratch_shapes=[pltpu.VMEM((tm, D_MODEL), BF16)],
        compiler_params=pltpu.CompilerParams(
            dimension_semantics=("arbitrary", "arbitrary"),
            vmem_limit_bytes=VMEM_LIMIT_BYTES),
        name="ffn",
    )(x1, g2, wgu, wgu, wd)


def _moba_bucket_tables():
    L = MOBA_BLOCK
    kj = np.arange(L)[:, None]
    qi = np.arange(L)[None, :]
    return np.stack([_t5_bucket_np(qi - kj), _t5_bucket_np(L + qi - kj)])


def _swa_bias_tables():
    W = WINDOW
    qi = np.arange(W)[None, :]
    c = np.arange(2 * W)[:, None]
    d = W + qi - c
    v = (d >= 0) & (d < W)
    d0 = qi - c
    v0 = (d0 >= 0) & (d0 < W)
    buckets = np.stack([_t5_bucket_np(d), _t5_bucket_np(d0)])
    valid = np.stack([v, v0]).astype(np.int32)
    return buckets, valid


def kernel(x, norm1_g, w_in, q_norm_a, k_norm_a, q_norm_b, k_norm_b, rel_bias, sinks,
           w_branch_a, w_branch_b, w_out, norm2_g, w_gate_up, w_down):
    b, s, d = x.shape
    depth = w_in.shape[0]
    assert d == D_MODEL and s % MOBA_PAIR == 0 and s % SWA_TQ == 0
    assert (b * s) % IN_TM == 0 and (b * s) % MERGE_TM == 0 and (b * s) % FFN_TM == 0
    far = _t5_bucket_np(np.arange(LANES + 1, s + MOBA_PAIR))
    assert (far == far[0]).all()
    far_bucket = int(far[0])

    bias_a = _moba_bias(rel_bias, jnp.asarray(_moba_bucket_tables()), far_bucket)
    bk_b, va_b = _swa_bias_tables()
    bias_b = _swa_bias(rel_bias, jnp.asarray(bk_b), jnp.asarray(va_b))
    x2 = x.reshape(b * s, d)
    for l in range(depth):
        colgain = _inproj_col_gains(q_norm_a[l], k_norm_a[l], q_norm_b[l], k_norm_b[l])
        qkv, gates, kvx = _inproj(x2, norm1_g[l][None, :], w_in[l], colgain)
        qkv3 = qkv.reshape(b, s, -1)
        ya, (wa, wb, wo, wgu, wd) = _moba(
            qkv3, bias_a, [w_branch_a, w_branch_b, w_out, w_gate_up, w_down], l)
        per_kv = B_Q_HEADS // B_KV_HEADS
        sink_rows = jnp.repeat(
            sinks[l].reshape(B_KV_HEADS, per_kv // 2, 2).transpose(0, 2, 1).reshape(2 * B_KV_HEADS, -1),
            WINDOW, axis=1) * LOG2E
        sink_rows = jnp.pad(sink_rows, ((0, 8 - 2 * B_KV_HEADS), (0, 0)))
        yb = _swa(qkv3, kvx.reshape(b, s, -1), bias_b, sink_rows)
        x1 = _merge(ya.reshape(b * s, -1), yb.reshape(b * s, -1), gates, x2, wa, wb, wo)
        x2 = _ffn(x1, norm2_g[l][None, :], wgu, wd)
    return x2.reshape(b, s, d)
```

```python
import functools
import math

import jax
import jax.numpy as jnp
import numpy as np
from jax import lax
from jax.experimental import pallas as pl
from jax.experimental.pallas import tpu as pltpu

F32 = jnp.float32
BF16 = jnp.bfloat16

D_MODEL = 2048
A_HEADS = 8
A_HEAD_DIM = 128
A_WIDTH = A_HEADS * A_HEAD_DIM
MOBA_BLOCK = 256
MOBA_TOPK = 3
B_Q_HEADS = 16
B_KV_HEADS = 2
B_HEAD_DIM = 64
B_WIDTH = B_Q_HEADS * B_HEAD_DIM
B_KV_WIDTH = B_KV_HEADS * B_HEAD_DIM
WINDOW = 128
NUM_BUCKETS = 32
MAX_DISTANCE = 128
MAX_EXACT = NUM_BUCKETS // 2
D_FF = -(-(8 * D_MODEL) // (3 * 256)) * 256
IN_WIDTH = 3 * A_WIDTH + B_WIDTH + 2 * B_KV_WIDTH + 2 * D_MODEL
EPS = 1e-6
NEG = -1e30
LOG2E = math.log2(math.e)

LANES = 128
VMEM_LIMIT_BYTES = 56 * 1024 * 1024

IN_TN = 1024
IN_QKV_TILES = (3 * A_WIDTH + B_WIDTH) // IN_TN
IN_GATE_TILES = (2 * D_MODEL) // IN_TN
IN_TILES = IN_QKV_TILES + IN_GATE_TILES + 1
IN_KV_COL = 3 * A_WIDTH + B_WIDTH
IN_GATE_COL = IN_KV_COL + 2 * B_KV_WIDTH
IN_TM = 1024
KVX_WIDTH = 8 * LANES

MOBA_PAIR = 2 * MOBA_BLOCK
MOBA_HEADS_PER_STEP = 4
SWA_TQ = 512
MERGE_TM = 512
MERGE_CH = 512
FFN_TM = 1024
FFN_TF = 512


def _dot(a, b):
    return jnp.dot(a, b, preferred_element_type=F32)


def _dot_nt(a, b):
    return lax.dot_general(a, b, (((1,), (1,)), ((), ())), preferred_element_type=F32)


def _sigmoid(v):
    return 0.5 * jnp.tanh(0.5 * v) + 0.5


def _t5_bucket_np(dist):
    n = np.maximum(dist, 0)
    nf = np.maximum(n, 1).astype(np.float32)
    large = MAX_EXACT + (np.log(nf / np.float32(MAX_EXACT))
                         / np.float32(math.log(MAX_DISTANCE / MAX_EXACT))
                         * np.float32(NUM_BUCKETS - MAX_EXACT)).astype(np.int32)
    large = np.minimum(large, NUM_BUCKETS - 1)
    return np.where(n < MAX_EXACT, n, large).astype(np.int32)


def _bucket_lookup(tab_ref, bm, h):
    acc = jnp.zeros(bm.shape, F32)
    for b in range(NUM_BUCKETS):
        acc = jnp.where(bm == b, tab_ref[b, h], acc)
    return acc


def _swa_bias_kernel(tab_ref, bucket_ref, valid_ref, o_ref):
    h = pl.program_id(0) + A_HEADS
    for t in range(bucket_ref.shape[0]):
        o_ref[t, 0, 0] = jnp.where(valid_ref[t] != 0,
                                   _bucket_lookup(tab_ref, bucket_ref[t], h) * LOG2E, NEG)


def _swa_bias(rel_bias, buckets, valid):
    n_tiles, r, c = buckets.shape
    per_kv = B_Q_HEADS // B_KV_HEADS
    return pl.pallas_call(
        _swa_bias_kernel,
        grid=(B_Q_HEADS,),
        in_specs=[
            pl.BlockSpec(memory_space=pltpu.SMEM),
            pl.BlockSpec((n_tiles, r, c), lambda h: (0, 0, 0)),
            pl.BlockSpec((n_tiles, r, c), lambda h: (0, 0, 0)),
        ],
        out_specs=pl.BlockSpec(
            (n_tiles, 1, 1, r, c),
            lambda h: (0, h // per_kv, h % 2, 0, (h % per_kv) // 2)),
        out_shape=jax.ShapeDtypeStruct((n_tiles, B_KV_HEADS, 2, r, c * per_kv // 2), F32),
        name="swa_bias",
    )(rel_bias, buckets, valid)


def _moba_bias_kernel(tab_ref, bucket_ref, o_ref, *, shift_bucket):
    h = pl.program_id(0)
    L = MOBA_BLOCK
    shift = tab_ref[shift_bucket, h]
    tiles = [(_bucket_lookup(tab_ref, bucket_ref[t], h) - shift) * LOG2E for t in range(2)]
    kk = lax.broadcasted_iota(jnp.int32, (L, L), 0)
    qq = lax.broadcasted_iota(jnp.int32, (L, L), 1)
    own = jnp.where(qq >= kk, tiles[0], NEG)
    o_ref[0, :L, :L] = own
    o_ref[0, L:, L:] = own
    o_ref[0, :L, L:] = tiles[1]
    o_ref[0, L:, :L] = jnp.full((L, L), NEG, F32)


def _moba_bias(rel_bias, buckets, shift_bucket):
    L = MOBA_BLOCK
    return pl.pallas_call(
        functools.partial(_moba_bias_kernel, shift_bucket=shift_bucket),
        grid=(A_HEADS,),
        in_specs=[
            pl.BlockSpec(memory_space=pltpu.SMEM),
            pl.BlockSpec((2, L, L), lambda h: (0, 0, 0)),
        ],
        out_specs=pl.BlockSpec((1, 2 * L, 2 * L), lambda h: (h, 0, 0)),
        out_shape=jax.ShapeDtypeStruct((A_HEADS, 2 * L, 2 * L), F32),
        name="moba_bias",
    )(rel_bias, buckets)


def _inproj_kernel(x_ref, g1_ref, w_ref, cg_ref,
                   qkv_ref, gates_ref, kvx_ref, h_ref):
    j = pl.program_id(1)

    @pl.when(j == 0)
    def _():
        x = x_ref[...]
        ms = jnp.mean(x * x, axis=-1, keepdims=True)
        h_ref[...] = (x * lax.rsqrt(ms + EPS) * g1_ref[...]).astype(BF16)

    cg = cg_ref[pl.ds(j, 1), :]

    def project(cols=slice(None), rows=slice(None)):
        return _dot(h_ref[rows, :], w_ref[:, cols])

    def normed_tile(dh):
        n_chunks = 4
        rc = IN_TM // n_chunks
        for r in range(n_chunks):
            rows = slice(r * rc, (r + 1) * rc)
            qkv_ref[rows, :] = head_norm(project(rows=rows), dh, cg).astype(BF16)

    def head_norm(v, dh, gain):
        parts = []
        for k in range(v.shape[1] // LANES):
            vk = v[:, k * LANES:(k + 1) * LANES]
            sq = vk * vk
            if dh == LANES:
                r = lax.rsqrt(jnp.sum(sq, axis=-1, keepdims=True) * (1.0 / dh) + EPS)
            else:
                lo = lax.broadcasted_iota(jnp.int32, vk.shape, 1) < dh
                ss_lo = jnp.sum(jnp.where(lo, sq, 0.0), axis=-1, keepdims=True)
                ss_hi = jnp.sum(jnp.where(lo, 0.0, sq), axis=-1, keepdims=True)
                r = jnp.where(lo, lax.rsqrt(ss_lo * (1.0 / dh) + EPS),
                              lax.rsqrt(ss_hi * (1.0 / dh) + EPS))
            parts.append(vk * r)
        return jnp.concatenate(parts, axis=1) * gain

    n_a = 2 * A_WIDTH // IN_TN
    n_va = 3 * A_WIDTH // IN_TN

    @pl.when(j < n_a)
    def _():
        normed_tile(A_HEAD_DIM)

    @pl.when((j >= n_a) & (j < n_va))
    def _():
        qkv_ref[...] = project().astype(BF16)

    @pl.when((j >= n_va) & (j < IN_QKV_TILES))
    def _():
        normed_tile(B_HEAD_DIM)

    @pl.when((j >= IN_QKV_TILES) & (j < IN_QKV_TILES + IN_GATE_TILES))
    def _():
        gates_ref[...] = project().astype(BF16)

    @pl.when(j == IN_TILES - 1)
    def _():
        y = project(slice(0, 2 * LANES))
        yk = y[:, :LANES]
        br = lax.broadcasted_iota(jnp.int32, (LANES, LANES), 0) // B_HEAD_DIM
        bc = lax.broadcasted_iota(jnp.int32, (LANES, LANES), 1) // B_HEAD_DIM
        ss = _dot((yk * yk).astype(BF16), jnp.where(br == bc, 1.0, 0.0).astype(BF16))
        kn = yk * lax.rsqrt(ss * (1.0 / B_HEAD_DIM) + EPS) * cg[:, :LANES]
        vv = y[:, LANES:2 * LANES]
        lo = lax.broadcasted_iota(jnp.int32, kn.shape, 1) < B_HEAD_DIM
        for base, t in ((0, kn), (4, vv)):
            e0 = jnp.where(lo, t, 0.0)
            o1 = jnp.where(lo, 0.0, t)
            o0 = pltpu.roll(e0, B_HEAD_DIM, 1)
            e1 = pltpu.roll(o1, B_HEAD_DIM, 1)
            for k, piece in enumerate((e0, o0, e1, o1)):
                kvx_ref[:, (base + k) * LANES:(base + k + 1) * LANES] = piece.astype(BF16)


def _inproj_w_col(j):
    if j < IN_QKV_TILES:
        return j * IN_TN
    if j < IN_QKV_TILES + IN_GATE_TILES:
        return IN_GATE_COL + (j - IN_QKV_TILES) * IN_TN
    return IN_KV_COL


def _w_in_tiles_kernel(w_ref, o_ref):
    for n in range(IN_TILES):
        c0 = _inproj_w_col(n)
        width = IN_TN if n < IN_TILES - 1 else 2 * B_KV_WIDTH
        o_ref[n, :, :width] = w_ref[:, c0:c0 + width].astype(BF16)
        if width < IN_TN:
            o_ref[n, :, width:] = jnp.zeros((o_ref.shape[1], IN_TN - width), BF16)


def _w_in_tiles(w_in, layer):
    rows = 128
    return pl.pallas_call(
        _w_in_tiles_kernel,
        grid=(D_MODEL // rows,),
        in_specs=[pl.BlockSpec((None, rows, IN_WIDTH), lambda r: (layer, r, 0))],
        out_specs=pl.BlockSpec((IN_TILES, rows, IN_TN), lambda r: (0, r, 0)),
        out_shape=jax.ShapeDtypeStruct((IN_TILES, D_MODEL, IN_TN), BF16),
        compiler_params=pltpu.CompilerParams(vmem_limit_bytes=VMEM_LIMIT_BYTES),
        name="w_in_tiles",
    )(w_in)


def _inproj_col_gains(q_norm_a, k_norm_a, q_norm_b, k_norm_b):
    ones = jnp.ones((IN_TN,), F32)
    rows = (
        [jnp.tile(q_norm_a * (A_HEAD_DIM ** -0.5 * LOG2E), IN_TN // A_HEAD_DIM)]
        * (A_WIDTH // IN_TN)
        + [jnp.tile(k_norm_a, IN_TN // A_HEAD_DIM)] * (A_WIDTH // IN_TN)
        + [ones] * (A_WIDTH // IN_TN)
        + [jnp.tile(q_norm_b * (B_HEAD_DIM ** -0.5 * LOG2E), IN_TN // B_HEAD_DIM)]
        * (B_WIDTH // IN_TN)
        + [ones] * IN_GATE_TILES
        + [jnp.concatenate([jnp.tile(k_norm_b, B_KV_HEADS),
                            jnp.ones((IN_TN - B_KV_WIDTH,), F32)])])
    assert len(rows) == IN_TILES
    return jnp.stack(rows + [ones] * (-IN_TILES % 8))


def _inproj(x2, g1, w, colgain):
    n = x2.shape[0]
    grid = (n // IN_TM, IN_TILES)
    return pl.pallas_call(
        _inproj_kernel,
        grid=grid,
        in_specs=[
            pl.BlockSpec((IN_TM, D_MODEL), lambda i, j: (i, 0)),
            pl.BlockSpec((1, D_MODEL), lambda i, j: (0, 0)),
            pl.BlockSpec((None, D_MODEL, IN_TN), lambda i, j: (j, 0, 0)),
            pl.BlockSpec(colgain.shape, lambda i, j: (0, 0)),
        ],
        out_specs=[
            pl.BlockSpec((IN_TM, IN_TN), lambda i, j: (i, jnp.minimum(j, IN_QKV_TILES - 1))),
            pl.BlockSpec((IN_TM, IN_TN),
                         lambda i, j: (i, jnp.clip(j - IN_QKV_TILES, 0, IN_GATE_TILES - 1))),
            pl.BlockSpec((IN_TM, KVX_WIDTH), lambda i, j: (i, 0)),
        ],
        out_shape=[
            jax.ShapeDtypeStruct((n, IN_QKV_TILES * IN_TN), BF16),
            jax.ShapeDtypeStruct((n, IN_GATE_TILES * IN_TN), BF16),
            jax.ShapeDtypeStruct((n, KVX_WIDTH), BF16),
        ],
        scratch_shapes=[pltpu.VMEM((IN_TM, D_MODEL), BF16)],
        compiler_params=pltpu.CompilerParams(
            dimension_semantics=("arbitrary", "arbitrary"),
            vmem_limit_bytes=VMEM_LIMIT_BYTES),
        name="inproj",
    )(x2, g1, w, colgain)


def _moba_kernel(*refs, n_casts):
    q_ref, k_ref, v_ref, bd_ref, bc_ref = refs[:5]
    cast_src = refs[5:5 + n_casts]
    o_ref = refs[5 + n_casts]
    cast_dst = refs[6 + n_casts:6 + 2 * n_casts]
    kmh_ref, kml_ref, vt_ref, qa_ref, s_ref, m_ref, l_ref, acc_ref = refs[6 + 2 * n_casts:]

    t = pl.program_id(2)
    L = MOBA_BLOCK
    P = MOBA_PAIR
    Dh = A_HEAD_DIM
    NH = MOBA_HEADS_PER_STEP
    seq = k_ref.shape[1]
    nb = seq // L

    @pl.when(t == 0)
    def _():
        r = lax.broadcasted_iota(jnp.int32, (nb, seq), 0)
        c = lax.broadcasted_iota(jnp.int32, (nb, seq), 1)
        ind = jnp.where(lax.shift_right_logical(c, int(math.log2(L))) == r, 1.0, 0.0).astype(BF16)
        er = lax.broadcasted_iota(jnp.int32, (Dh, Dh), 0)
        ec = lax.broadcasted_iota(jnp.int32, (Dh, Dh), 1)
        eye = jnp.where(er == ec, 1.0, 0.0).astype(BF16)
        for hh in range(NH):
            km = _dot(ind, k_ref[0, :, hh * Dh:(hh + 1) * Dh]) * (1.0 / L)
            hi = km.astype(BF16)
            kmh_ref[hh] = hi
            kml_ref[hh] = (km - hi.astype(F32)).astype(BF16)
            for g in range(seq // P):
                vt_ref[hh, g] = _dot_nt(
                    eye, v_ref[0, g * P:(g + 1) * P, hh * Dh:(hh + 1) * Dh]).astype(BF16)

    lane = lax.broadcasted_iota(jnp.int32, (P, LANES), 1)
    row = lax.broadcasted_iota(jnp.int32, (P, LANES), 0)
    upper = (row >= L).astype(jnp.int32)

    def select(hh):
        q = q_ref[0, :, hh * Dh:(hh + 1) * Dh]
        g = _dot_nt(kmh_ref[hh], q) + _dot_nt(kml_ref[hh], q)
        blk = lax.broadcasted_iota(jnp.int32, g.shape, 0)
        blk_f = blk.astype(F32)
        qcol = lax.broadcasted_iota(jnp.int32, g.shape, 1)
        own_blk = 2 * t + (qcol >= L).astype(jnp.int32)
        past = blk < own_blk
        g = jnp.where(past, g, NEG)
        sel = jnp.full(g.shape, NEG, F32)
        for _ in range(MOBA_TOPK):
            mx = jnp.max(g, axis=0, keepdims=True)
            first = jnp.min(jnp.where(g == mx, blk_f, float(LANES)), axis=0, keepdims=True)
            hit = blk_f == first
            sel = jnp.where(hit & past, 0.0, sel)
            g = jnp.where(hit, -jnp.inf, g)
        sel = jnp.where(blk == own_blk, 0.0, sel)
        sel = jnp.concatenate([sel, jnp.zeros((LANES - nb, P), F32)], axis=0)
        qa_ref[hh, :, :Dh] = q
        qa_ref[hh, :, Dh:] = sel.T.astype(BF16)

    def scores(hh, pair):
        r0 = pl.multiple_of(pair * P, P)
        ind = jnp.where(lane == 2 * pair + upper, 1.0, 0.0).astype(BF16)
        k_aug = jnp.concatenate([k_ref[0, pl.ds(r0, P), hh * Dh:(hh + 1) * Dh], ind], axis=1)
        return _dot_nt(k_aug, qa_ref[hh])

    def accumulate(hh, pair):
        s = s_ref[hh]
        m_old = m_ref[hh]
        m_new = jnp.maximum(m_old, jnp.max(s, axis=0, keepdims=True))
        alpha = jnp.exp2(m_old - m_new)
        p = jnp.exp2(s - m_new)
        l_ref[hh] = alpha * l_ref[hh] + jnp.sum(p, axis=0, keepdims=True)
        acc_ref[hh] = alpha * acc_ref[hh] + _dot(vt_ref[hh, pair], p.astype(BF16))
        m_ref[hh] = m_new

    for hh in range(NH):
        select(hh)
    for src, dst in zip(cast_src, cast_dst):
        if len(dst.shape) == 3:
            tc = dst.shape[2]
            for n in range(dst.shape[0]):
                dst[n] = src[:, n * tc:(n + 1) * tc].astype(BF16)
        else:
            dst[...] = src[...].astype(BF16)
    for hh in range(NH):
        s = scores(hh, t) + bd_ref[hh]
        m0 = jnp.max(s, axis=0, keepdims=True)
        p = jnp.exp2(s - m0)
        m_ref[hh] = m0
        l_ref[hh] = jnp.sum(p, axis=0, keepdims=True)
        acc_ref[hh] = _dot(vt_ref[hh, t], p.astype(BF16))

    @pl.when(t >= 1)
    def _():
        for hh in range(NH):
            s_ref[hh] = scores(hh, t - 1)
            s_ref[hh, P - LANES:, :LANES] += bc_ref[hh]
        for hh in range(NH):
            accumulate(hh, t - 1)

    def far_body(pair, carry):
        for hh in range(NH):
            s_ref[hh] = scores(hh, pair)
        for hh in range(NH):
            accumulate(hh, pair)
        return carry

    lax.fori_loop(0, jnp.maximum(t - 1, 0), far_body, 0)

    for hh in range(NH):
        o_ref[0, :, hh * Dh:(hh + 1) * Dh] = (acc_ref[hh] / l_ref[hh]).T.astype(BF16)


def _cast_slab_grid(rows, cols, n_steps):
    nc = 1
    while nc <= n_steps:
        nr = n_steps // nc
        if rows % (nr * 16) == 0 and cols % (nc * LANES) == 0:
            return nr, nc
        nc *= 2
    raise ValueError(f"no slab split of {(rows, cols)} for {n_steps} steps")


def _moba(qkv3, bias_a, weights, layer):
    b, s, _ = qkv3.shape
    P = MOBA_PAIR
    Dh = A_HEAD_DIM
    NH = MOBA_HEADS_PER_STEP
    G = A_HEADS // NH
    T = s // P
    nb = s // MOBA_BLOCK
    assert nb % 16 == 0 and nb <= LANES
    n_steps = b * G * T
    src_specs, dst_specs, dst_shapes = [], [], []
    for w, tile_cols in weights:
        _, r, c = w.shape
        if tile_cols is None:
            nr, nc = _cast_slab_grid(r, c, n_steps)
        else:
            nr, nc = n_steps, 1
            assert r % (nr * 16) == 0 and c % tile_cols == 0 and tile_cols % LANES == 0

        def slab(bi, h, t, nc=nc):
            step = (bi * G + h) * T + t
            return step // nc, step % nc

        src_specs.append(pl.BlockSpec((None, r // nr, c // nc),
                                      lambda bi, h, t, slab=slab: (layer, *slab(bi, h, t))))
        if tile_cols is None:
            dst_specs.append(pl.BlockSpec((r // nr, c // nc), slab))
            dst_shapes.append(jax.ShapeDtypeStruct((r, c), BF16))
        else:
            dst_specs.append(pl.BlockSpec((c // tile_cols, r // nr, tile_cols),
                                          lambda bi, h, t, slab=slab: (0, slab(bi, h, t)[0], 0)))
            dst_shapes.append(jax.ShapeDtypeStruct((c // tile_cols, r, tile_cols), BF16))
    outs = pl.pallas_call(
        functools.partial(_moba_kernel, n_casts=len(weights)),
        grid=(b, G, T),
        in_specs=[
            pl.BlockSpec((1, P, NH * Dh), lambda bi, h, t: (bi, t, h)),
            pl.BlockSpec((1, s, NH * Dh), lambda bi, h, t: (bi, 0, G + h)),
            pl.BlockSpec((1, s, NH * Dh), lambda bi, h, t: (bi, 0, 2 * G + h)),
            pl.BlockSpec((NH, P, P), lambda bi, h, t: (h, 0, 0)),
            pl.BlockSpec((NH, LANES, LANES), lambda bi, h, t: (h, 1, 2)),
        ] + src_specs,
        out_specs=[pl.BlockSpec((1, P, NH * Dh), lambda bi, h, t: (bi, t, h))] + dst_specs,
        out_shape=[jax.ShapeDtypeStruct((b, s, A_WIDTH), BF16)] + dst_shapes,
        scratch_shapes=[
            pltpu.VMEM((NH, nb, Dh), BF16),
            pltpu.VMEM((NH, nb, Dh), BF16),
            pltpu.VMEM((NH, s // P, Dh, P), BF16),
            pltpu.VMEM((NH, P, Dh + LANES), BF16),
            pltpu.VMEM((NH, P, P), F32),
            pltpu.VMEM((NH, 1, P), F32),
            pltpu.VMEM((NH, 1, P), F32),
            pltpu.VMEM((NH, Dh, P), F32),
        ],
        compiler_params=pltpu.CompilerParams(
            dimension_semantics=("arbitrary", "arbitrary", "arbitrary"),
            vmem_limit_bytes=VMEM_LIMIT_BYTES),
        name="moba",
    )(qkv3, qkv3, qkv3, bias_a, bias_a, *[w for w, _ in weights])
    return outs[0], outs[1:]


def _swa_kernel(q_ref, kvx_ref, bias_ref, sink_ref, o_ref, vt_ref):
    t = pl.program_id(1)
    W = WINDOW
    n_win = q_ref.shape[1] // W
    seq = kvx_ref.shape[1]
    pairs_per_kv = B_Q_HEADS // B_KV_HEADS // 2
    n_combo = 2 * B_KV_HEADS
    chunk = 4 * W

    @pl.when(t == 0)
    def _():
        er = lax.broadcasted_iota(jnp.int32, (LANES, LANES), 0)
        ec = lax.broadcasted_iota(jnp.int32, (LANES, LANES), 1)
        eye = jnp.where(er == ec, 1.0, 0.0).astype(BF16)
        for i in range(n_combo):
            col = (n_combo + i) * LANES
            for c in range(seq // chunk):
                vt = _dot_nt(eye, kvx_ref[0, c * chunk:(c + 1) * chunk, col:col + LANES])
                for k in range(chunk // W):
                    vt_ref[i, c * (chunk // W) + k] = vt[:, k * W:(k + 1) * W].astype(BF16)

    def win(w, carry):
        n = t * n_win + w
        q0 = pl.multiple_of(w * W, W)
        sb = jnp.maximum(n - 1, 0)
        start = pl.multiple_of(sb * W, W)
        first = (n == 0).astype(jnp.int32)
        scores = []
        for kv in range(B_KV_HEADS):
            width = pairs_per_kv * LANES
            qg = q_ref[0, pl.ds(q0, W), kv * width:(kv + 1) * width]
            qs = jnp.concatenate(
                [qg[:, k * LANES:(k + 1) * LANES] for k in range(pairs_per_kv)], axis=0)
            for par in range(2):
                kcol = (2 * kv + par) * LANES
                kk = kvx_ref[0, pl.ds(start, 2 * W), kcol:kcol + LANES]
                scores.append(_dot_nt(kk, qs) + bias_ref[first, kv, par])
        outs = []
        for i, s in enumerate(scores):
            sink = sink_ref[pl.ds(i, 1), :]
            m = jnp.maximum(jnp.max(s, axis=0, keepdims=True), sink)
            pe = jnp.exp2(s - m)
            l = jnp.sum(pe, axis=0, keepdims=True) + jnp.exp2(sink - m)
            vt = jnp.concatenate([vt_ref[i, sb], vt_ref[i, sb + 1]], axis=1)
            outs.append(_dot(vt, pe.astype(BF16)) / l)
        for kv in range(B_KV_HEADS):
            ot = outs[2 * kv] + outs[2 * kv + 1]
            for k in range(pairs_per_kv):
                p = kv * pairs_per_kv + k
                o_ref[0, pl.ds(q0, W), p * LANES:(p + 1) * LANES] = (
                    ot[:, k * W:(k + 1) * W].T.astype(BF16))
        return carry

    lax.fori_loop(0, n_win, win, 0, unroll=4)


def _swa(qkv3, kvx3, bias_b, sink_rows):
    b, s, _ = qkv3.shape
    qb_blk = 3 * A_WIDTH // B_WIDTH
    return pl.pallas_call(
        _swa_kernel,
        grid=(b, s // SWA_TQ),
        in_specs=[
            pl.BlockSpec((1, SWA_TQ, B_WIDTH), lambda bi, t: (bi, t, qb_blk)),
            pl.BlockSpec((1, s, KVX_WIDTH), lambda bi, t: (bi, 0, 0)),
            pl.BlockSpec(bias_b.shape, lambda bi, t: (0, 0, 0, 0, 0)),
            pl.BlockSpec(sink_rows.shape, lambda bi, t: (0, 0)),
        ],
        out_specs=pl.BlockSpec((1, SWA_TQ, B_WIDTH), lambda bi, t: (bi, t, 0)),
        out_shape=jax.ShapeDtypeStruct((b, s, B_WIDTH), BF16),
        scratch_shapes=[pltpu.VMEM((2 * B_KV_HEADS, s // WINDOW, LANES, WINDOW), BF16)],
        compiler_params=pltpu.CompilerParams(
            dimension_semantics=("arbitrary", "arbitrary"),
            vmem_limit_bytes=VMEM_LIMIT_BYTES),
        name="swa",
    )(qkv3, kvx3, bias_b, sink_rows)


def _merge_kernel(ya_ref, yb_ref, gt_ref, x_ref, wa_ref, wb_ref, wo_ref, o_ref, mg_ref):
    ch = MERGE_CH
    for c in range(D_MODEL // ch):
        cols = slice(c * ch, (c + 1) * ch)
        ta = _dot(ya_ref[...], wa_ref[:, cols])
        tb = _dot(yb_ref[...], wb_ref[:, cols])
        ga = _sigmoid(gt_ref[:, cols].astype(F32))
        gb = _sigmoid(gt_ref[:, D_MODEL + c * ch:D_MODEL + (c + 1) * ch].astype(F32))
        mg_ref[:, cols] = (ga * ta + gb * tb).astype(BF16)
    for c in range(D_MODEL // ch):
        cols = slice(c * ch, (c + 1) * ch)
        o_ref[:, cols] = x_ref[:, cols] + _dot(mg_ref[...], wo_ref[:, cols])


def _merge(ya, yb, gates, x2, wa, wb, wo):
    n = x2.shape[0]
    tm = MERGE_TM
    resident = functools.partial(pl.BlockSpec, pipeline_mode=pl.Buffered(1))
    return pl.pallas_call(
        _merge_kernel,
        grid=(n // tm,),
        in_specs=[
            pl.BlockSpec((tm, A_WIDTH), lambda i: (i, 0)),
            pl.BlockSpec((tm, B_WIDTH), lambda i: (i, 0)),
            pl.BlockSpec((tm, 2 * D_MODEL), lambda i: (i, 0)),
            pl.BlockSpec((tm, D_MODEL), lambda i: (i, 0)),
            resident((A_WIDTH, D_MODEL), lambda i: (0, 0)),
            resident((B_WIDTH, D_MODEL), lambda i: (0, 0)),
            resident((D_MODEL, D_MODEL), lambda i: (0, 0)),
        ],
        out_specs=pl.BlockSpec((tm, D_MODEL), lambda i: (i, 0)),
        out_shape=jax.ShapeDtypeStruct((n, D_MODEL), F32),
        scratch_shapes=[pltpu.VMEM((tm, D_MODEL), BF16)],
        compiler_params=pltpu.CompilerParams(
            dimension_semantics=("arbitrary",),
            vmem_limit_bytes=VMEM_LIMIT_BYTES),
        name="merge",
    )(ya, yb, gates, x2, wa, wb, wo)


def _ffn_kernel(x_ref, g2_ref, wg_ref, wu_ref, wd_ref, o_ref, h_ref):
    f = pl.program_id(1)

    @pl.when(f == 0)
    def _():
        x = x_ref[...]
        ms = jnp.mean(x * x, axis=-1, keepdims=True)
        h_ref[...] = (x * lax.rsqrt(ms + EPS) * g2_ref[...]).astype(BF16)
        o_ref[...] = x

    h = h_ref[...]
    g = _dot(h, wg_ref[...])
    u = _dot(h, wu_ref[...])
    a = (g * _sigmoid(g) * u).astype(BF16)
    o_ref[...] += _dot(a, wd_ref[...])


def _ffn(x1, g2, wgu, wd):
    n = x1.shape[0]
    tm, tf = FFN_TM, FFN_TF
    nf = D_FF // tf
    return pl.pallas_call(
        _ffn_kernel,
        grid=(n // tm, nf),
        in_specs=[
            pl.BlockSpec((tm, D_MODEL), lambda i, f: (i, 0)),
            pl.BlockSpec((1, D_MODEL), lambda i, f: (0, 0)),
            pl.BlockSpec((None, D_MODEL, tf), lambda i, f: (f, 0, 0)),
            pl.BlockSpec((None, D_MODEL, tf), lambda i, f: (nf + f, 0, 0)),
            pl.BlockSpec((tf, D_MODEL), lambda i, f: (f, 0)),
        ],
        out_specs=pl.BlockSpec((tm, D_MODEL), lambda i, f: (i, 0)),
        out_shape=jax.ShapeDtypeStruct((n, D_MODEL), F32),
        scratch_shapes=[pltpu.VMEM((tm, D_MODEL), BF16)],
        compiler_params=pltpu.CompilerParams(
            dimension_semantics=("arbitrary", "arbitrary"),
            vmem_limit_bytes=VMEM_LIMIT_BYTES),
        name="ffn",
    )(x1, g2, wgu, wgu, wd)


def _moba_bucket_tables():
    L = MOBA_BLOCK
    kj = np.arange(L)[:, None]
    qi = np.arange(L)[None, :]
    return np.stack([_t5_bucket_np(qi - kj), _t5_bucket_np(L + qi - kj)])


def _swa_bias_tables():
    W = WINDOW
    qi = np.arange(W)[None, :]
    c = np.arange(2 * W)[:, None]
    d = W + qi - c
    v = (d >= 0) & (d < W)
    d0 = qi - c
    v0 = (d0 >= 0) & (d0 < W)
    buckets = np.stack([_t5_bucket_np(d), _t5_bucket_np(d0)])
    valid = np.stack([v, v0]).astype(np.int32)
    return buckets, valid


def kernel(x, norm1_g, w_in, q_norm_a, k_norm_a, q_norm_b, k_norm_b, rel_bias, sinks,
           w_branch_a, w_branch_b, w_out, norm2_g, w_gate_up, w_down):
    b, s, d = x.shape
    depth = w_in.shape[0]
    assert d == D_MODEL and s % MOBA_PAIR == 0 and s % SWA_TQ == 0
    assert (b * s) % IN_TM == 0 and (b * s) % MERGE_TM == 0 and (b * s) % FFN_TM == 0
    far = _t5_bucket_np(np.arange(LANES + 1, s + MOBA_PAIR))
    assert (far == far[0]).all()
    far_bucket = int(far[0])

    bias_a = _moba_bias(rel_bias, jnp.asarray(_moba_bucket_tables()), far_bucket)
    bk_b, va_b = _swa_bias_tables()
    bias_b = _swa_bias(rel_bias, jnp.asarray(bk_b), jnp.asarray(va_b))
    x2 = x.reshape(b * s, d)
    for l in range(depth):
        colgain = _inproj_col_gains(q_norm_a[l], k_norm_a[l], q_norm_b[l], k_norm_b[l])
        qkv, gates, kvx = _inproj(x2, norm1_g[l][None, :], _w_in_tiles(w_in, l), colgain)
        qkv3 = qkv.reshape(b, s, -1)
        ya, (wa, wb, wo, wgu, wd) = _moba(
            qkv3, bias_a, [(w_branch_a, None), (w_branch_b, None), (w_out, None),
                           (w_gate_up, FFN_TF), (w_down, None)], l)
        per_kv = B_Q_HEADS // B_KV_HEADS
        sink_rows = jnp.repeat(
            sinks[l].reshape(B_KV_HEADS, per_kv // 2, 2).transpose(0, 2, 1).reshape(2 * B_KV_HEADS, -1),
            WINDOW, axis=1) * LOG2E
        sink_rows = jnp.pad(sink_rows, ((0, 8 - 2 * B_KV_HEADS), (0, 0)))
        yb = _swa(qkv3, kvx.reshape(b, s, -1), bias_b, sink_rows)
        x1 = _merge(ya.reshape(b * s, -1), yb.reshape(b * s, -1), gates, x2, wa, wb, wo)
        x2 = _ffn(x1, norm2_g[l][None, :], wgu, wd)
    return x2.reshape(b, s, d)
```

```python
import functools
import math

import jax
import jax.numpy as jnp
import numpy as np
from jax import lax
from jax.experimental import pallas as pl
from jax.experimental.pallas import tpu as pltpu

F32 = jnp.float32
BF16 = jnp.bfloat16

D_MODEL = 2048
A_HEADS = 8
A_HEAD_DIM = 128
A_WIDTH = A_HEADS * A_HEAD_DIM
MOBA_BLOCK = 256
MOBA_TOPK = 3
B_Q_HEADS = 16
B_KV_HEADS = 2
B_HEAD_DIM = 64
B_WIDTH = B_Q_HEADS * B_HEAD_DIM
B_KV_WIDTH = B_KV_HEADS * B_HEAD_DIM
WINDOW = 128
NUM_BUCKETS = 32
MAX_DISTANCE = 128
MAX_EXACT = NUM_BUCKETS // 2
D_FF = -(-(8 * D_MODEL) // (3 * 256)) * 256
IN_WIDTH = 3 * A_WIDTH + B_WIDTH + 2 * B_KV_WIDTH + 2 * D_MODEL
EPS = 1e-6
NEG = -1e30
LOG2E = math.log2(math.e)

LANES = 128
VMEM_LIMIT_BYTES = 56 * 1024 * 1024

IN_TN = 1024
IN_QKV_TILES = (3 * A_WIDTH + B_WIDTH) // IN_TN
IN_GATE_TILES = (2 * D_MODEL) // IN_TN
IN_TILES = IN_QKV_TILES + IN_GATE_TILES + 1
IN_KV_COL = 3 * A_WIDTH + B_WIDTH
IN_GATE_COL = IN_KV_COL + 2 * B_KV_WIDTH
IN_TM = 1024
KVX_WIDTH = 8 * LANES

MOBA_PAIR = 2 * MOBA_BLOCK
MOBA_HEADS_PER_STEP = 4
SWA_TQ = 512
MERGE_TM = 512
MERGE_CH = 512
FFN_TM = 1024
FFN_TF = 512


def _dot(a, b):
    return jnp.dot(a, b, preferred_element_type=F32)


def _dot_nt(a, b):
    return lax.dot_general(a, b, (((1,), (1,)), ((), ())), preferred_element_type=F32)


def _sigmoid(v):
    return 0.5 * jnp.tanh(0.5 * v) + 0.5


def _t5_bucket_np(dist):
    n = np.maximum(dist, 0)
    nf = np.maximum(n, 1).astype(np.float32)
    large = MAX_EXACT + (np.log(nf / np.float32(MAX_EXACT))
                         / np.float32(math.log(MAX_DISTANCE / MAX_EXACT))
                         * np.float32(NUM_BUCKETS - MAX_EXACT)).astype(np.int32)
    large = np.minimum(large, NUM_BUCKETS - 1)
    return np.where(n < MAX_EXACT, n, large).astype(np.int32)


def _bucket_lookup(tab_ref, bm, h):
    acc = jnp.zeros(bm.shape, F32)
    for b in range(NUM_BUCKETS):
        acc = jnp.where(bm == b, tab_ref[b, h], acc)
    return acc


def _swa_bias_kernel(tab_ref, bucket_ref, valid_ref, o_ref):
    h = pl.program_id(0) + A_HEADS
    for t in range(bucket_ref.shape[0]):
        o_ref[t, 0, 0] = jnp.where(valid_ref[t] != 0,
                                   _bucket_lookup(tab_ref, bucket_ref[t], h) * LOG2E, NEG)


def _swa_bias(rel_bias, buckets, valid):
    n_tiles, r, c = buckets.shape
    per_kv = B_Q_HEADS // B_KV_HEADS
    return pl.pallas_call(
        _swa_bias_kernel,
        grid=(B_Q_HEADS,),
        in_specs=[
            pl.BlockSpec(memory_space=pltpu.SMEM),
            pl.BlockSpec((n_tiles, r, c), lambda h: (0, 0, 0)),
            pl.BlockSpec((n_tiles, r, c), lambda h: (0, 0, 0)),
        ],
        out_specs=pl.BlockSpec(
            (n_tiles, 1, 1, r, c),
            lambda h: (0, h // per_kv, h % 2, 0, (h % per_kv) // 2)),
        out_shape=jax.ShapeDtypeStruct((n_tiles, B_KV_HEADS, 2, r, c * per_kv // 2), F32),
        name="swa_bias",
    )(rel_bias, buckets, valid)


def _moba_bias_kernel(tab_ref, bucket_ref, o_ref, *, shift_bucket):
    h = pl.program_id(0)
    L = MOBA_BLOCK
    shift = tab_ref[shift_bucket, h]
    tiles = [(_bucket_lookup(tab_ref, bucket_ref[t], h) - shift) * LOG2E for t in range(2)]
    kk = lax.broadcasted_iota(jnp.int32, (L, L), 0)
    qq = lax.broadcasted_iota(jnp.int32, (L, L), 1)
    own = jnp.where(qq >= kk, tiles[0], NEG)
    o_ref[0, :L, :L] = own
    o_ref[0, L:, L:] = own
    o_ref[0, :L, L:] = tiles[1]
    o_ref[0, L:, :L] = jnp.full((L, L), NEG, F32)


def _moba_bias(rel_bias, buckets, shift_bucket):
    L = MOBA_BLOCK
    return pl.pallas_call(
        functools.partial(_moba_bias_kernel, shift_bucket=shift_bucket),
        grid=(A_HEADS,),
        in_specs=[
            pl.BlockSpec(memory_space=pltpu.SMEM),
            pl.BlockSpec((2, L, L), lambda h: (0, 0, 0)),
        ],
        out_specs=pl.BlockSpec((1, 2 * L, 2 * L), lambda h: (h, 0, 0)),
        out_shape=jax.ShapeDtypeStruct((A_HEADS, 2 * L, 2 * L), F32),
        name="moba_bias",
    )(rel_bias, buckets)


def _inproj_kernel(x_ref, g1_ref, w_ref, cg_ref,
                   qkv_ref, gates_ref, kvx_ref, h_ref):
    j = pl.program_id(1)

    @pl.when(j == 0)
    def _():
        x = x_ref[...]
        ms = jnp.mean(x * x, axis=-1, keepdims=True)
        h_ref[...] = (x * lax.rsqrt(ms + EPS) * g1_ref[...]).astype(BF16)

    cg = cg_ref[pl.ds(j, 1), :]

    def project(cols=slice(None), rows=slice(None)):
        return _dot(h_ref[rows, :], w_ref[:, cols].astype(BF16))

    def normed_tile(dh):
        n_chunks = 4
        rc = IN_TM // n_chunks
        for r in range(n_chunks):
            rows = slice(r * rc, (r + 1) * rc)
            qkv_ref[rows, :] = head_norm(project(rows=rows), dh, cg).astype(BF16)

    def head_norm(v, dh, gain):
        parts = []
        for k in range(v.shape[1] // LANES):
            vk = v[:, k * LANES:(k + 1) * LANES]
            sq = vk * vk
            if dh == LANES:
                r = lax.rsqrt(jnp.sum(sq, axis=-1, keepdims=True) * (1.0 / dh) + EPS)
            else:
                lo = lax.broadcasted_iota(jnp.int32, vk.shape, 1) < dh
                ss_lo = jnp.sum(jnp.where(lo, sq, 0.0), axis=-1, keepdims=True)
                ss_hi = jnp.sum(jnp.where(lo, 0.0, sq), axis=-1, keepdims=True)
                r = jnp.where(lo, lax.rsqrt(ss_lo * (1.0 / dh) + EPS),
                              lax.rsqrt(ss_hi * (1.0 / dh) + EPS))
            parts.append(vk * r)
        return jnp.concatenate(parts, axis=1) * gain

    n_a = 2 * A_WIDTH // IN_TN
    n_va = 3 * A_WIDTH // IN_TN

    @pl.when(j < n_a)
    def _():
        normed_tile(A_HEAD_DIM)

    @pl.when((j >= n_a) & (j < n_va))
    def _():
        qkv_ref[...] = project().astype(BF16)

    @pl.when((j >= n_va) & (j < IN_QKV_TILES))
    def _():
        normed_tile(B_HEAD_DIM)

    @pl.when((j >= IN_QKV_TILES) & (j < IN_QKV_TILES + IN_GATE_TILES))
    def _():
        gates_ref[...] = project().astype(BF16)

    @pl.when(j == IN_TILES - 1)
    def _():
        y = project(slice(0, 2 * LANES))
        yk = y[:, :LANES]
        br = lax.broadcasted_iota(jnp.int32, (LANES, LANES), 0) // B_HEAD_DIM
        bc = lax.broadcasted_iota(jnp.int32, (LANES, LANES), 1) // B_HEAD_DIM
        ss = _dot((yk * yk).astype(BF16), jnp.where(br == bc, 1.0, 0.0).astype(BF16))
        kn = yk * lax.rsqrt(ss * (1.0 / B_HEAD_DIM) + EPS) * cg[:, :LANES]
        vv = y[:, LANES:2 * LANES]
        lo = lax.broadcasted_iota(jnp.int32, kn.shape, 1) < B_HEAD_DIM
        for base, t in ((0, kn), (4, vv)):
            e0 = jnp.where(lo, t, 0.0)
            o1 = jnp.where(lo, 0.0, t)
            o0 = pltpu.roll(e0, B_HEAD_DIM, 1)
            e1 = pltpu.roll(o1, B_HEAD_DIM, 1)
            for k, piece in enumerate((e0, o0, e1, o1)):
                kvx_ref[:, (base + k) * LANES:(base + k + 1) * LANES] = piece.astype(BF16)


def _inproj_w_col(j):
    t = IN_TN // LANES
    lane_tile = jnp.where(j < IN_QKV_TILES, j * t,
                          jnp.where(j < IN_QKV_TILES + IN_GATE_TILES,
                                    IN_GATE_COL // LANES + (j - IN_QKV_TILES) * t,
                                    IN_KV_COL // LANES))
    return lane_tile * LANES


def _inproj_col_gains(q_norm_a, k_norm_a, q_norm_b, k_norm_b):
    ones = jnp.ones((IN_TN,), F32)
    rows = (
        [jnp.tile(q_norm_a * (A_HEAD_DIM ** -0.5 * LOG2E), IN_TN // A_HEAD_DIM)]
        * (A_WIDTH // IN_TN)
        + [jnp.tile(k_norm_a, IN_TN // A_HEAD_DIM)] * (A_WIDTH // IN_TN)
        + [ones] * (A_WIDTH // IN_TN)
        + [jnp.tile(q_norm_b * (B_HEAD_DIM ** -0.5 * LOG2E), IN_TN // B_HEAD_DIM)]
        * (B_WIDTH // IN_TN)
        + [ones] * IN_GATE_TILES
        + [jnp.concatenate([jnp.tile(k_norm_b, B_KV_HEADS),
                            jnp.ones((IN_TN - B_KV_WIDTH,), F32)])])
    assert len(rows) == IN_TILES
    return jnp.stack(rows + [ones] * (-IN_TILES % 8))


def _inproj(x2, g1, w, colgain):
    n = x2.shape[0]
    grid = (n // IN_TM, IN_TILES)
    return pl.pallas_call(
        _inproj_kernel,
        grid=grid,
        in_specs=[
            pl.BlockSpec((IN_TM, D_MODEL), lambda i, j: (i, 0)),
            pl.BlockSpec((1, D_MODEL), lambda i, j: (0, 0)),
            pl.BlockSpec((pl.Element(D_MODEL), pl.Element(IN_TN)),
                         lambda i, j: (0, _inproj_w_col(j))),
            pl.BlockSpec(colgain.shape, lambda i, j: (0, 0)),
        ],
        out_specs=[
            pl.BlockSpec((IN_TM, IN_TN), lambda i, j: (i, jnp.minimum(j, IN_QKV_TILES - 1))),
            pl.BlockSpec((IN_TM, IN_TN),
                         lambda i, j: (i, jnp.clip(j - IN_QKV_TILES, 0, IN_GATE_TILES - 1))),
            pl.BlockSpec((IN_TM, KVX_WIDTH), lambda i, j: (i, 0)),
        ],
        out_shape=[
            jax.ShapeDtypeStruct((n, IN_QKV_TILES * IN_TN), BF16),
            jax.ShapeDtypeStruct((n, IN_GATE_TILES * IN_TN), BF16),
            jax.ShapeDtypeStruct((n, KVX_WIDTH), BF16),
        ],
        scratch_shapes=[pltpu.VMEM((IN_TM, D_MODEL), BF16)],
        compiler_params=pltpu.CompilerParams(
            dimension_semantics=("arbitrary", "arbitrary"),
            vmem_limit_bytes=VMEM_LIMIT_BYTES),
        name="inproj",
    )(x2, g1, w, colgain)


def _moba_kernel(*refs, n_casts):
    q_ref, k_ref, v_ref, bd_ref, bc_ref = refs[:5]
    cast_src = refs[5:5 + n_casts]
    o_ref = refs[5 + n_casts]
    cast_dst = refs[6 + n_casts:6 + 2 * n_casts]
    kmh_ref, kml_ref, vt_ref, qa_ref, s_ref, m_ref, l_ref, acc_ref = refs[6 + 2 * n_casts:]

    t = pl.program_id(2)
    L = MOBA_BLOCK
    P = MOBA_PAIR
    Dh = A_HEAD_DIM
    NH = MOBA_HEADS_PER_STEP
    seq = k_ref.shape[1]
    nb = seq // L

    @pl.when(t == 0)
    def _():
        r = lax.broadcasted_iota(jnp.int32, (nb, seq), 0)
        c = lax.broadcasted_iota(jnp.int32, (nb, seq), 1)
        ind = jnp.where(lax.shift_right_logical(c, int(math.log2(L))) == r, 1.0, 0.0).astype(BF16)
        er = lax.broadcasted_iota(jnp.int32, (Dh, Dh), 0)
        ec = lax.broadcasted_iota(jnp.int32, (Dh, Dh), 1)
        eye = jnp.where(er == ec, 1.0, 0.0).astype(BF16)
        for hh in range(NH):
            km = _dot(ind, k_ref[0, :, hh * Dh:(hh + 1) * Dh]) * (1.0 / L)
            hi = km.astype(BF16)
            kmh_ref[hh] = hi
            kml_ref[hh] = (km - hi.astype(F32)).astype(BF16)
            for g in range(seq // P):
                vt_ref[hh, g] = _dot_nt(
                    eye, v_ref[0, g * P:(g + 1) * P, hh * Dh:(hh + 1) * Dh]).astype(BF16)

    lane = lax.broadcasted_iota(jnp.int32, (P, LANES), 1)
    row = lax.broadcasted_iota(jnp.int32, (P, LANES), 0)
    upper = (row >= L).astype(jnp.int32)

    def select(hh):
        q = q_ref[0, :, hh * Dh:(hh + 1) * Dh]
        g = _dot_nt(kmh_ref[hh], q) + _dot_nt(kml_ref[hh], q)
        blk = lax.broadcasted_iota(jnp.int32, g.shape, 0)
        blk_f = blk.astype(F32)
        qcol = lax.broadcasted_iota(jnp.int32, g.shape, 1)
        own_blk = 2 * t + (qcol >= L).astype(jnp.int32)
        past = blk < own_blk
        g = jnp.where(past, g, NEG)
        sel = jnp.full(g.shape, NEG, F32)
        for _ in range(MOBA_TOPK):
            mx = jnp.max(g, axis=0, keepdims=True)
            first = jnp.min(jnp.where(g == mx, blk_f, float(LANES)), axis=0, keepdims=True)
            hit = blk_f == first
            sel = jnp.where(hit & past, 0.0, sel)
            g = jnp.where(hit, -jnp.inf, g)
        sel = jnp.where(blk == own_blk, 0.0, sel)
        sel = jnp.concatenate([sel, jnp.zeros((LANES - nb, P), F32)], axis=0)
        qa_ref[hh, :, :Dh] = q
        qa_ref[hh, :, Dh:] = sel.T.astype(BF16)

    def scores(hh, pair):
        r0 = pl.multiple_of(pair * P, P)
        ind = jnp.where(lane == 2 * pair + upper, 1.0, 0.0).astype(BF16)
        k_aug = jnp.concatenate([k_ref[0, pl.ds(r0, P), hh * Dh:(hh + 1) * Dh], ind], axis=1)
        return _dot_nt(k_aug, qa_ref[hh])

    def accumulate(hh, pair, buf):
        m_old = m_ref[hh]
        m_new = jnp.maximum(m_old, jnp.max(s_ref[buf, hh], axis=0, keepdims=True))
        alpha = jnp.exp2(m_old - m_new)
        p = jnp.exp2(s_ref[buf, hh] - m_new)
        l_ref[hh] = alpha * l_ref[hh] + jnp.sum(p, axis=0, keepdims=True)
        acc_ref[hh] = alpha * acc_ref[hh] + _dot(vt_ref[hh, pair], p.astype(BF16))
        m_ref[hh] = m_new

    def phase(pair, buf, fetch=True):
        if fetch:
            for hh in range(NH):
                s_ref[1 - buf, hh] = scores(hh, jnp.maximum(pair - 1, 0))
        for hh in range(NH):
            accumulate(hh, pair, buf)

    for hh in range(NH):
        select(hh)
    for src, dst in zip(cast_src, cast_dst):
        dst[...] = src[...].astype(BF16)
    for hh in range(NH):
        s_ref[1, hh] = scores(hh, t)
    for hh in range(NH):
        s_ref[0, hh] = scores(hh, jnp.maximum(t - 1, 0))
        s_ref[0, hh, P - LANES:, :LANES] += bc_ref[hh]
    for hh in range(NH):
        s = s_ref[1, hh] + bd_ref[hh]
        m0 = jnp.max(s, axis=0, keepdims=True)
        p = jnp.exp2(s - m0)
        m_ref[hh] = m0
        l_ref[hh] = jnp.sum(p, axis=0, keepdims=True)
        acc_ref[hh] = _dot(vt_ref[hh, t], p.astype(BF16))

    @pl.when(t >= 1)
    def _():
        phase(t - 1, 0)

    def two_phases(i, carry):
        pair = t - 2 - 2 * i
        phase(pair, 1)
        phase(pair - 1, 0)
        return carry

    n_older = jnp.maximum(t - 1, 0)
    lax.fori_loop(0, n_older // 2, two_phases, 0)

    @pl.when(n_older % 2 == 1)
    def _():
        phase(0, 1, fetch=False)

    for hh in range(NH):
        o_ref[0, :, hh * Dh:(hh + 1) * Dh] = (acc_ref[hh] / l_ref[hh]).T.astype(BF16)


def _cast_slab_grid(rows, cols, n_steps):
    nc = 1
    while nc <= n_steps:
        nr = n_steps // nc
        if rows % (nr * 16) == 0 and cols % (nc * LANES) == 0:
            return nr, nc
        nc *= 2
    raise ValueError(f"no slab split of {(rows, cols)} for {n_steps} steps")


def _moba(qkv3, bias_a, weights, layer):
    b, s, _ = qkv3.shape
    P = MOBA_PAIR
    Dh = A_HEAD_DIM
    NH = MOBA_HEADS_PER_STEP
    G = A_HEADS // NH
    T = s // P
    nb = s // MOBA_BLOCK
    assert nb % 16 == 0 and nb <= LANES
    n_steps = b * G * T
    src_specs, dst_specs, dst_shapes = [], [], []
    for w in weights:
        _, r, c = w.shape
        nr, nc = _cast_slab_grid(r, c, n_steps)

        def slab(bi, h, t, nc=nc):
            step = (bi * G + h) * T + t
            return step // nc, step % nc

        src_specs.append(pl.BlockSpec((None, r // nr, c // nc),
                                      lambda bi, h, t, slab=slab: (layer, *slab(bi, h, t))))
        dst_specs.append(pl.BlockSpec((r // nr, c // nc), slab))
        dst_shapes.append(jax.ShapeDtypeStruct((r, c), BF16))
    outs = pl.pallas_call(
        functools.partial(_moba_kernel, n_casts=len(weights)),
        grid=(b, G, T),
        in_specs=[
            pl.BlockSpec((1, P, NH * Dh), lambda bi, h, t: (bi, t, h)),
            pl.BlockSpec((1, s, NH * Dh), lambda bi, h, t: (bi, 0, G + h)),
            pl.BlockSpec((1, s, NH * Dh), lambda bi, h, t: (bi, 0, 2 * G + h)),
            pl.BlockSpec((NH, P, P), lambda bi, h, t: (h, 0, 0)),
            pl.BlockSpec((NH, LANES, LANES), lambda bi, h, t: (h, 1, 2)),
        ] + src_specs,
        out_specs=[pl.BlockSpec((1, P, NH * Dh), lambda bi, h, t: (bi, t, h))] + dst_specs,
        out_shape=[jax.ShapeDtypeStruct((b, s, A_WIDTH), BF16)] + dst_shapes,
        scratch_shapes=[
            pltpu.VMEM((NH, nb, Dh), BF16),
            pltpu.VMEM((NH, nb, Dh), BF16),
            pltpu.VMEM((NH, s // P, Dh, P), BF16),
            pltpu.VMEM((NH, P, Dh + LANES), BF16),
            pltpu.VMEM((2, NH, P, P), F32),
            pltpu.VMEM((NH, 1, P), F32),
            pltpu.VMEM((NH, 1, P), F32),
            pltpu.VMEM((NH, Dh, P), F32),
        ],
        compiler_params=pltpu.CompilerParams(
            dimension_semantics=("arbitrary", "arbitrary", "arbitrary"),
            vmem_limit_bytes=VMEM_LIMIT_BYTES),
        name="moba",
    )(qkv3, qkv3, qkv3, bias_a, bias_a, *weights)
    return outs[0], outs[1:]


def _swa_kernel(q_ref, kvx_ref, bias_ref, sink_ref, o_ref, vt_ref):
    t = pl.program_id(1)
    W = WINDOW
    n_win = q_ref.shape[1] // W
    seq = kvx_ref.shape[1]
    pairs_per_kv = B_Q_HEADS // B_KV_HEADS // 2
    n_combo = 2 * B_KV_HEADS
    chunk = 4 * W

    @pl.when(t == 0)
    def _():
        er = lax.broadcasted_iota(jnp.int32, (LANES, LANES), 0)
        ec = lax.broadcasted_iota(jnp.int32, (LANES, LANES), 1)
        eye = jnp.where(er == ec, 1.0, 0.0).astype(BF16)
        for i in range(n_combo):
            col = (n_combo + i) * LANES
            for c in range(seq // chunk):
                vt = _dot_nt(eye, kvx_ref[0, c * chunk:(c + 1) * chunk, col:col + LANES])
                for k in range(chunk // W):
                    vt_ref[i, c * (chunk // W) + k] = vt[:, k * W:(k + 1) * W].astype(BF16)

    def win(w, carry):
        n = t * n_win + w
        q0 = pl.multiple_of(w * W, W)
        sb = jnp.maximum(n - 1, 0)
        start = pl.multiple_of(sb * W, W)
        first = (n == 0).astype(jnp.int32)
        scores = []
        for kv in range(B_KV_HEADS):
            width = pairs_per_kv * LANES
            qg = q_ref[0, pl.ds(q0, W), kv * width:(kv + 1) * width]
            qs = jnp.concatenate(
                [qg[:, k * LANES:(k + 1) * LANES] for k in range(pairs_per_kv)], axis=0)
            for par in range(2):
                kcol = (2 * kv + par) * LANES
                kk = kvx_ref[0, pl.ds(start, 2 * W), kcol:kcol + LANES]
                scores.append(_dot_nt(kk, qs) + bias_ref[first, kv, par])
        outs = []
        for i, s in enumerate(scores):
            sink = sink_ref[pl.ds(i, 1), :]
            m = jnp.maximum(jnp.max(s, axis=0, keepdims=True), sink)
            pe = jnp.exp2(s - m)
            l = jnp.sum(pe, axis=0, keepdims=True) + jnp.exp2(sink - m)
            vt = jnp.concatenate([vt_ref[i, sb], vt_ref[i, sb + 1]], axis=1)
            outs.append(_dot(vt, pe.astype(BF16)) / l)
        for kv in range(B_KV_HEADS):
            ot = outs[2 * kv] + outs[2 * kv + 1]
            for k in range(pairs_per_kv):
                p = kv * pairs_per_kv + k
                o_ref[0, pl.ds(q0, W), p * LANES:(p + 1) * LANES] = (
                    ot[:, k * W:(k + 1) * W].T.astype(BF16))
        return carry

    lax.fori_loop(0, n_win, win, 0, unroll=4)


def _swa(qkv3, kvx3, bias_b, sink_rows):
    b, s, _ = qkv3.shape
    qb_blk = 3 * A_WIDTH // B_WIDTH
    return pl.pallas_call(
        _swa_kernel,
        grid=(b, s // SWA_TQ),
        in_specs=[
            pl.BlockSpec((1, SWA_TQ, B_WIDTH), lambda bi, t: (bi, t, qb_blk)),
            pl.BlockSpec((1, s, KVX_WIDTH), lambda bi, t: (bi, 0, 0)),
            pl.BlockSpec(bias_b.shape, lambda bi, t: (0, 0, 0, 0, 0)),
            pl.BlockSpec(sink_rows.shape, lambda bi, t: (0, 0)),
        ],
        out_specs=pl.BlockSpec((1, SWA_TQ, B_WIDTH), lambda bi, t: (bi, t, 0)),
        out_shape=jax.ShapeDtypeStruct((b, s, B_WIDTH), BF16),
        scratch_shapes=[pltpu.VMEM((2 * B_KV_HEADS, s // WINDOW, LANES, WINDOW), BF16)],
        compiler_params=pltpu.CompilerParams(
            dimension_semantics=("arbitrary", "arbitrary"),
            vmem_limit_bytes=VMEM_LIMIT_BYTES),
        name="swa",
    )(qkv3, kvx3, bias_b, sink_rows)


def _merge_kernel(ya_ref, yb_ref, gt_ref, x_ref, wa_ref, wb_ref, wo_ref, o_ref, mg_ref):
    ch = MERGE_CH
    for c in range(D_MODEL // ch):
        cols = slice(c * ch, (c + 1) * ch)
        ta = _dot(ya_ref[...], wa_ref[:, cols])
        tb = _dot(yb_ref[...], wb_ref[:, cols])
        ga = _sigmoid(gt_ref[:, cols].astype(F32))
        gb = _sigmoid(gt_ref[:, D_MODEL + c * ch:D_MODEL + (c + 1) * ch].astype(F32))
        mg_ref[:, cols] = (ga * ta + gb * tb).astype(BF16)
    for c in range(D_MODEL // ch):
        cols = slice(c * ch, (c + 1) * ch)
        o_ref[:, cols] = x_ref[:, cols] + _dot(mg_ref[...], wo_ref[:, cols])


def _merge(ya, yb, gates, x2, wa, wb, wo):
    n = x2.shape[0]
    tm = MERGE_TM
    resident = functools.partial(pl.BlockSpec, pipeline_mode=pl.Buffered(1))
    return pl.pallas_call(
        _merge_kernel,
        grid=(n // tm,),
        in_specs=[
            pl.BlockSpec((tm, A_WIDTH), lambda i: (i, 0)),
            pl.BlockSpec((tm, B_WIDTH), lambda i: (i, 0)),
            pl.BlockSpec((tm, 2 * D_MODEL), lambda i: (i, 0)),
            pl.BlockSpec((tm, D_MODEL), lambda i: (i, 0)),
            resident((A_WIDTH, D_MODEL), lambda i: (0, 0)),
            resident((B_WIDTH, D_MODEL), lambda i: (0, 0)),
            resident((D_MODEL, D_MODEL), lambda i: (0, 0)),
        ],
        out_specs=pl.BlockSpec((tm, D_MODEL), lambda i: (i, 0)),
        out_shape=jax.ShapeDtypeStruct((n, D_MODEL), F32),
        scratch_shapes=[pltpu.VMEM((tm, D_MODEL), BF16)],
        compiler_params=pltpu.CompilerParams(
            dimension_semantics=("arbitrary",),
            vmem_limit_bytes=VMEM_LIMIT_BYTES),
        name="merge",
    )(ya, yb, gates, x2, wa, wb, wo)


def _ffn_kernel(x_ref, g2_ref, wg_ref, wu_ref, wd_ref, o_ref, h_ref):
    f = pl.program_id(1)

    @pl.when(f == 0)
    def _():
        x = x_ref[...]
        ms = jnp.mean(x * x, axis=-1, keepdims=True)
        h_ref[...] = (x * lax.rsqrt(ms + EPS) * g2_ref[...]).astype(BF16)
        o_ref[...] = x

    h = h_ref[...]
    g = _dot(h, wg_ref[...])
    u = _dot(h, wu_ref[...])
    a = (g * _sigmoid(g) * u).astype(BF16)
    o_ref[...] += _dot(a, wd_ref[...])


def _ffn(x1, g2, wgu, wd):
    n = x1.shape[0]
    tm, tf = FFN_TM, FFN_TF
    nf = D_FF // tf
    return pl.pallas_call(
        _ffn_kernel,
        grid=(n // tm, nf),
        in_specs=[
            pl.BlockSpec((tm, D_MODEL), lambda i, f: (i, 0)),
            pl.BlockSpec((1, D_MODEL), lambda i, f: (0, 0)),
            pl.BlockSpec((D_MODEL, tf), lambda i, f: (0, f)),
            pl.BlockSpec((D_MODEL, tf), lambda i, f: (0, nf + f)),
            pl.BlockSpec((tf, D_MODEL), lambda i, f: (f, 0)),
        ],
        out_specs=pl.BlockSpec((tm, D_MODEL), lambda i, f: (i, 0)),
        out_shape=jax.ShapeDtypeStruct((n, D_MODEL), F32),
        scratch_shapes=[pltpu.VMEM((tm, D_MODEL), BF16)],
        compiler_params=pltpu.CompilerParams(
            dimension_semantics=("arbitrary", "arbitrary"),
            vmem_limit_bytes=VMEM_LIMIT_BYTES),
        name="ffn",
    )(x1, g2, wgu, wgu, wd)


def _moba_bucket_tables():
    L = MOBA_BLOCK
    kj = np.arange(L)[:, None]
    qi = np.arange(L)[None, :]
    return np.stack([_t5_bucket_np(qi - kj), _t5_bucket_np(L + qi - kj)])


def _swa_bias_tables():
    W = WINDOW
    qi = np.arange(W)[None, :]
    c = np.arange(2 * W)[:, None]
    d = W + qi - c
    v = (d >= 0) & (d < W)
    d0 = qi - c
    v0 = (d0 >= 0) & (d0 < W)
    buckets = np.stack([_t5_bucket_np(d), _t5_bucket_np(d0)])
    valid = np.stack([v, v0]).astype(np.int32)
    return buckets, valid


def kernel(x, norm1_g, w_in, q_norm_a, k_norm_a, q_norm_b, k_norm_b, rel_bias, sinks,
           w_branch_a, w_branch_b, w_out, norm2_g, w_gate_up, w_down):
    b, s, d = x.shape
    depth = w_in.shape[0]
    assert d == D_MODEL and s % MOBA_PAIR == 0 and s % SWA_TQ == 0
    assert (b * s) % IN_TM == 0 and (b * s) % MERGE_TM == 0 and (b * s) % FFN_TM == 0
    far = _t5_bucket_np(np.arange(LANES + 1, s + MOBA_PAIR))
    assert (far == far[0]).all()
    far_bucket = int(far[0])

    bias_a = _moba_bias(rel_bias, jnp.asarray(_moba_bucket_tables()), far_bucket)
    bk_b, va_b = _swa_bias_tables()
    bias_b = _swa_bias(rel_bias, jnp.asarray(bk_b), jnp.asarray(va_b))
    x2 = x.reshape(b * s, d)
    for l in range(depth):
        colgain = _inproj_col_gains(q_norm_a[l], k_norm_a[l], q_norm_b[l], k_norm_b[l])
        qkv, gates, kvx = _inproj(x2, norm1_g[l][None, :], w_in[l], colgain)
        qkv3 = qkv.reshape(b, s, -1)
        ya, (wa, wb, wo, wgu, wd) = _moba(
            qkv3, bias_a, [w_branch_a, w_branch_b, w_out, w_gate_up, w_down], l)
        per_kv = B_Q_HEADS // B_KV_HEADS
        sink_rows = jnp.repeat(
            sinks[l].reshape(B_KV_HEADS, per_kv // 2, 2).transpose(0, 2, 1).reshape(2 * B_KV_HEADS, -1),
            WINDOW, axis=1) * LOG2E
        sink_rows = jnp.pad(sink_rows, ((0, 8 - 2 * B_KV_HEADS), (0, 0)))
        yb = _swa(qkv3, kvx.reshape(b, s, -1), bias_b, sink_rows)
        x1 = _merge(ya.reshape(b * s, -1), yb.reshape(b * s, -1), gates, x2, wa, wb, wo)
        x2 = _ffn(x1, norm2_g[l][None, :], wgu, wd)
    return x2.reshape(b, s, d)
```

```python
import functools
import math

import jax
import jax.numpy as jnp
import numpy as np
from jax import lax
from jax.experimental import pallas as pl
from jax.experimental.pallas import tpu as pltpu

F32 = jnp.float32
BF16 = jnp.bfloat16

D_MODEL = 2048
A_HEADS = 8
A_HEAD_DIM = 128
A_WIDTH = A_HEADS * A_HEAD_DIM
MOBA_BLOCK = 256
MOBA_TOPK = 3
B_Q_HEADS = 16
B_KV_HEADS = 2
B_HEAD_DIM = 64
B_WIDTH = B_Q_HEADS * B_HEAD_DIM
B_KV_WIDTH = B_KV_HEADS * B_HEAD_DIM
WINDOW = 128
NUM_BUCKETS = 32
MAX_DISTANCE = 128
MAX_EXACT = NUM_BUCKETS // 2
D_FF = -(-(8 * D_MODEL) // (3 * 256)) * 256
IN_WIDTH = 3 * A_WIDTH + B_WIDTH + 2 * B_KV_WIDTH + 2 * D_MODEL
EPS = 1e-6
NEG = -1e30
LOG2E = math.log2(math.e)

LANES = 128
VMEM_LIMIT_BYTES = 56 * 1024 * 1024

IN_TN = 1024
IN_QKV_TILES = (3 * A_WIDTH + B_WIDTH) // IN_TN
IN_GATE_TILES = (2 * D_MODEL) // IN_TN
IN_TILES = IN_QKV_TILES + IN_GATE_TILES + 1
IN_KV_COL = 3 * A_WIDTH + B_WIDTH
IN_GATE_COL = IN_KV_COL + 2 * B_KV_WIDTH
IN_TM = 1024
KVX_WIDTH = 8 * LANES

MOBA_PAIR = 2 * MOBA_BLOCK
MOBA_HEADS_PER_STEP = 4
SWA_TQ = 512
MERGE_TM = 512
MERGE_CH = 512
FFN_TM = 1024
FFN_TF = 512


def _dot(a, b):
    return jnp.dot(a, b, preferred_element_type=F32)


def _dot_nt(a, b):
    return lax.dot_general(a, b, (((1,), (1,)), ((), ())), preferred_element_type=F32)


def _sigmoid(v):
    return 0.5 * jnp.tanh(0.5 * v) + 0.5


def _t5_bucket_np(dist):
    n = np.maximum(dist, 0)
    nf = np.maximum(n, 1).astype(np.float32)
    large = MAX_EXACT + (np.log(nf / np.float32(MAX_EXACT))
                         / np.float32(math.log(MAX_DISTANCE / MAX_EXACT))
                         * np.float32(NUM_BUCKETS - MAX_EXACT)).astype(np.int32)
    large = np.minimum(large, NUM_BUCKETS - 1)
    return np.where(n < MAX_EXACT, n, large).astype(np.int32)


def _bucket_lookup(tab_ref, bm, h):
    acc = jnp.zeros(bm.shape, F32)
    for b in range(NUM_BUCKETS):
        acc = jnp.where(bm == b, tab_ref[b, h], acc)
    return acc


def _swa_bias_kernel(tab_ref, bucket_ref, valid_ref, o_ref):
    h = pl.program_id(0) + A_HEADS
    for t in range(bucket_ref.shape[0]):
        o_ref[t, 0, 0] = jnp.where(valid_ref[t] != 0,
                                   _bucket_lookup(tab_ref, bucket_ref[t], h) * LOG2E, NEG)


def _swa_bias(rel_bias, buckets, valid):
    n_tiles, r, c = buckets.shape
    per_kv = B_Q_HEADS // B_KV_HEADS
    return pl.pallas_call(
        _swa_bias_kernel,
        grid=(B_Q_HEADS,),
        in_specs=[
            pl.BlockSpec(memory_space=pltpu.SMEM),
            pl.BlockSpec((n_tiles, r, c), lambda h: (0, 0, 0)),
            pl.BlockSpec((n_tiles, r, c), lambda h: (0, 0, 0)),
        ],
        out_specs=pl.BlockSpec(
            (n_tiles, 1, 1, r, c),
            lambda h: (0, h // per_kv, h % 2, 0, (h % per_kv) // 2)),
        out_shape=jax.ShapeDtypeStruct((n_tiles, B_KV_HEADS, 2, r, c * per_kv // 2), F32),
        name="swa_bias",
    )(rel_bias, buckets, valid)


def _moba_bias_kernel(tab_ref, bucket_ref, o_ref, *, shift_bucket):
    h = pl.program_id(0)
    L = MOBA_BLOCK
    shift = tab_ref[shift_bucket, h]
    tiles = [(_bucket_lookup(tab_ref, bucket_ref[t], h) - shift) * LOG2E for t in range(2)]
    kk = lax.broadcasted_iota(jnp.int32, (L, L), 0)
    qq = lax.broadcasted_iota(jnp.int32, (L, L), 1)
    own = jnp.where(qq >= kk, tiles[0], NEG)
    o_ref[0, :L, :L] = own
    o_ref[0, L:, L:] = own
    o_ref[0, :L, L:] = tiles[1]
    o_ref[0, L:, :L] = jnp.full((L, L), NEG, F32)


def _moba_bias(rel_bias, buckets, shift_bucket):
    L = MOBA_BLOCK
    return pl.pallas_call(
        functools.partial(_moba_bias_kernel, shift_bucket=shift_bucket),
        grid=(A_HEADS,),
        in_specs=[
            pl.BlockSpec(memory_space=pltpu.SMEM),
            pl.BlockSpec((2, L, L), lambda h: (0, 0, 0)),
        ],
        out_specs=pl.BlockSpec((1, 2 * L, 2 * L), lambda h: (h, 0, 0)),
        out_shape=jax.ShapeDtypeStruct((A_HEADS, 2 * L, 2 * L), F32),
        name="moba_bias",
    )(rel_bias, buckets)


def _inproj_kernel(x_ref, g1_ref, w_ref, cg_ref,
                   qkv_ref, gates_ref, kvx_ref, h_ref):
    j = pl.program_id(1)

    @pl.when(j == 0)
    def _():
        x = x_ref[...]
        ms = jnp.mean(x * x, axis=-1, keepdims=True)
        h_ref[...] = (x * lax.rsqrt(ms + EPS) * g1_ref[...]).astype(BF16)

    cg = cg_ref[pl.ds(j, 1), :]

    def project(cols=slice(None), rows=slice(None)):
        return _dot(h_ref[rows, :], w_ref[:, cols].astype(BF16))

    def normed_tile(dh):
        n_chunks = 4
        rc = IN_TM // n_chunks
        for r in range(n_chunks):
            rows = slice(r * rc, (r + 1) * rc)
            qkv_ref[rows, :] = head_norm(project(rows=rows), dh, cg).astype(BF16)

    def head_norm(v, dh, gain):
        parts = []
        for k in range(v.shape[1] // LANES):
            vk = v[:, k * LANES:(k + 1) * LANES]
            sq = vk * vk
            if dh == LANES:
                r = lax.rsqrt(jnp.sum(sq, axis=-1, keepdims=True) * (1.0 / dh) + EPS)
            else:
                lo = lax.broadcasted_iota(jnp.int32, vk.shape, 1) < dh
                ss_lo = jnp.sum(jnp.where(lo, sq, 0.0), axis=-1, keepdims=True)
                ss_hi = jnp.sum(jnp.where(lo, 0.0, sq), axis=-1, keepdims=True)
                r = jnp.where(lo, lax.rsqrt(ss_lo * (1.0 / dh) + EPS),
                              lax.rsqrt(ss_hi * (1.0 / dh) + EPS))
            parts.append(vk * r)
        return jnp.concatenate(parts, axis=1) * gain

    n_a = 2 * A_WIDTH // IN_TN
    n_va = 3 * A_WIDTH // IN_TN

    @pl.when(j < n_a)
    def _():
        normed_tile(A_HEAD_DIM)

    @pl.when((j >= n_a) & (j < n_va))
    def _():
        qkv_ref[...] = project().astype(BF16)

    @pl.when((j >= n_va) & (j < IN_QKV_TILES))
    def _():
        normed_tile(B_HEAD_DIM)

    @pl.when((j >= IN_QKV_TILES) & (j < IN_QKV_TILES + IN_GATE_TILES))
    def _():
        gates_ref[...] = project().astype(BF16)

    @pl.when(j == IN_TILES - 1)
    def _():
        y = project(slice(0, 2 * LANES))
        yk = y[:, :LANES]
        br = lax.broadcasted_iota(jnp.int32, (LANES, LANES), 0) // B_HEAD_DIM
        bc = lax.broadcasted_iota(jnp.int32, (LANES, LANES), 1) // B_HEAD_DIM
        ss = _dot((yk * yk).astype(BF16), jnp.where(br == bc, 1.0, 0.0).astype(BF16))
        kn = yk * lax.rsqrt(ss * (1.0 / B_HEAD_DIM) + EPS) * cg[:, :LANES]
        vv = y[:, LANES:2 * LANES]
        lo = lax.broadcasted_iota(jnp.int32, kn.shape, 1) < B_HEAD_DIM
        for base, t in ((0, kn), (4, vv)):
            e0 = jnp.where(lo, t, 0.0)
            o1 = jnp.where(lo, 0.0, t)
            o0 = pltpu.roll(e0, B_HEAD_DIM, 1)
            e1 = pltpu.roll(o1, B_HEAD_DIM, 1)
            for k, piece in enumerate((e0, o0, e1, o1)):
                kvx_ref[:, (base + k) * LANES:(base + k + 1) * LANES] = piece.astype(BF16)


def _inproj_w_col(j):
    t = IN_TN // LANES
    lane_tile = jnp.where(j < IN_QKV_TILES, j * t,
                          jnp.where(j < IN_QKV_TILES + IN_GATE_TILES,
                                    IN_GATE_COL // LANES + (j - IN_QKV_TILES) * t,
                                    IN_KV_COL // LANES))
    return lane_tile * LANES


def _inproj_col_gains(q_norm_a, k_norm_a, q_norm_b, k_norm_b):
    ones = jnp.ones((IN_TN,), F32)
    rows = (
        [jnp.tile(q_norm_a * (A_HEAD_DIM ** -0.5 * LOG2E), IN_TN // A_HEAD_DIM)]
        * (A_WIDTH // IN_TN)
        + [jnp.tile(k_norm_a, IN_TN // A_HEAD_DIM)] * (A_WIDTH // IN_TN)
        + [ones] * (A_WIDTH // IN_TN)
        + [jnp.tile(q_norm_b * (B_HEAD_DIM ** -0.5 * LOG2E), IN_TN // B_HEAD_DIM)]
        * (B_WIDTH // IN_TN)
        + [ones] * IN_GATE_TILES
        + [jnp.concatenate([jnp.tile(k_norm_b, B_KV_HEADS),
                            jnp.ones((IN_TN - B_KV_WIDTH,), F32)])])
    assert len(rows) == IN_TILES
    return jnp.stack(rows + [ones] * (-IN_TILES % 8))


def _inproj(x2, g1, w, colgain):
    n = x2.shape[0]
    grid = (n // IN_TM, IN_TILES)
    return pl.pallas_call(
        _inproj_kernel,
        grid=grid,
        in_specs=[
            pl.BlockSpec((IN_TM, D_MODEL), lambda i, j: (i, 0)),
            pl.BlockSpec((1, D_MODEL), lambda i, j: (0, 0)),
            pl.BlockSpec((pl.Element(D_MODEL), pl.Element(IN_TN)),
                         lambda i, j: (0, _inproj_w_col(j))),
            pl.BlockSpec(colgain.shape, lambda i, j: (0, 0)),
        ],
        out_specs=[
            pl.BlockSpec((IN_TM, IN_TN), lambda i, j: (i, jnp.minimum(j, IN_QKV_TILES - 1))),
            pl.BlockSpec((IN_TM, IN_TN),
                         lambda i, j: (i, jnp.clip(j - IN_QKV_TILES, 0, IN_GATE_TILES - 1))),
            pl.BlockSpec((IN_TM, KVX_WIDTH), lambda i, j: (i, 0)),
        ],
        out_shape=[
            jax.ShapeDtypeStruct((n, IN_QKV_TILES * IN_TN), BF16),
            jax.ShapeDtypeStruct((n, IN_GATE_TILES * IN_TN), BF16),
            jax.ShapeDtypeStruct((n, KVX_WIDTH), BF16),
        ],
        scratch_shapes=[pltpu.VMEM((IN_TM, D_MODEL), BF16)],
        compiler_params=pltpu.CompilerParams(
            dimension_semantics=("arbitrary", "arbitrary"),
            vmem_limit_bytes=VMEM_LIMIT_BYTES),
        name="inproj",
    )(x2, g1, w, colgain)


def _moba_kernel(*refs, n_casts):
    q_ref, k_ref, v_ref, bd_ref, bc_ref = refs[:5]
    cast_src = refs[5:5 + n_casts]
    o_ref = refs[5 + n_casts]
    cast_dst = refs[6 + n_casts:6 + 2 * n_casts]
    kmh_ref, kml_ref, vt_ref, qa_ref, s_ref, m_ref, l_ref, acc_ref = refs[6 + 2 * n_casts:]

    t = pl.program_id(2)
    L = MOBA_BLOCK
    P = MOBA_PAIR
    Dh = A_HEAD_DIM
    NH = MOBA_HEADS_PER_STEP
    seq = k_ref.shape[1]
    nb = seq // L

    @pl.when(t == 0)
    def _():
        r = lax.broadcasted_iota(jnp.int32, (nb, seq), 0)
        c = lax.broadcasted_iota(jnp.int32, (nb, seq), 1)
        ind = jnp.where(lax.shift_right_logical(c, int(math.log2(L))) == r, 1.0, 0.0).astype(BF16)
        er = lax.broadcasted_iota(jnp.int32, (Dh, Dh), 0)
        ec = lax.broadcasted_iota(jnp.int32, (Dh, Dh), 1)
        eye = jnp.where(er == ec, 1.0, 0.0).astype(BF16)
        for hh in range(NH):
            km = _dot(ind, k_ref[0, :, hh * Dh:(hh + 1) * Dh]) * (1.0 / L)
            hi = km.astype(BF16)
            kmh_ref[hh] = hi
            kml_ref[hh] = (km - hi.astype(F32)).astype(BF16)
            for g in range(seq // P):
                vt_ref[hh, g] = _dot_nt(
                    eye, v_ref[0, g * P:(g + 1) * P, hh * Dh:(hh + 1) * Dh]).astype(BF16)

    lane = lax.broadcasted_iota(jnp.int32, (P, LANES), 1)
    row = lax.broadcasted_iota(jnp.int32, (P, LANES), 0)
    upper = (row >= L).astype(jnp.int32)

    def select(hh):
        q = q_ref[0, :, hh * Dh:(hh + 1) * Dh]
        g = _dot_nt(kmh_ref[hh], q) + _dot_nt(kml_ref[hh], q)
        blk = lax.broadcasted_iota(jnp.int32, g.shape, 0)
        blk_f = blk.astype(F32)
        qcol = lax.broadcasted_iota(jnp.int32, g.shape, 1)
        own_blk = 2 * t + (qcol >= L).astype(jnp.int32)
        past = blk < own_blk
        g = jnp.where(past, g, NEG)
        sel = jnp.full(g.shape, NEG, F32)
        for _ in range(MOBA_TOPK):
            mx = jnp.max(g, axis=0, keepdims=True)
            first = jnp.min(jnp.where(g == mx, blk_f, float(LANES)), axis=0, keepdims=True)
            hit = blk_f == first
            sel = jnp.where(hit & past, 0.0, sel)
            g = jnp.where(hit, -jnp.inf, g)
        sel = jnp.where(blk == own_blk, 0.0, sel)
        sel = jnp.concatenate([sel, jnp.zeros((LANES - nb, P), F32)], axis=0)
        qa_ref[hh, :, :Dh] = q
        qa_ref[hh, :, Dh:] = sel.T.astype(BF16)

    def scores(hh, pair):
        r0 = pl.multiple_of(pair * P, P)
        ind = jnp.where(lane == 2 * pair + upper, 1.0, 0.0).astype(BF16)
        k_aug = jnp.concatenate([k_ref[0, pl.ds(r0, P), hh * Dh:(hh + 1) * Dh], ind], axis=1)
        return _dot_nt(k_aug, qa_ref[hh])

    def accumulate(hh, pair, buf):
        m_old = m_ref[hh]
        m_new = jnp.maximum(m_old, jnp.max(s_ref[buf, hh], axis=0, keepdims=True))
        alpha = jnp.exp2(m_old - m_new)
        p = jnp.exp2(s_ref[buf, hh] - m_new)
        l_ref[hh] = alpha * l_ref[hh] + jnp.sum(p, axis=0, keepdims=True)
        acc_ref[hh] = alpha * acc_ref[hh] + _dot(vt_ref[hh, pair], p.astype(BF16))
        m_ref[hh] = m_new

    def phase(pair, buf, fetch=True):
        if fetch:
            for hh in range(NH):
                s_ref[1 - buf, hh] = scores(hh, jnp.maximum(pair - 1, 0))
        for hh in range(NH):
            accumulate(hh, pair, buf)

    for hh in range(NH):
        select(hh)
    for src, dst in zip(cast_src, cast_dst):
        dst[...] = src[...].astype(BF16)
    for hh in range(NH):
        s_ref[1, hh] = scores(hh, t)
    for hh in range(NH):
        s_ref[0, hh] = scores(hh, jnp.maximum(t - 1, 0))
        s_ref[0, hh, P - LANES:, :LANES] += bc_ref[hh]
    for hh in range(NH):
        s = s_ref[1, hh] + bd_ref[hh]
        m0 = jnp.max(s, axis=0, keepdims=True)
        p = jnp.exp2(s - m0)
        m_ref[hh] = m0
        l_ref[hh] = jnp.sum(p, axis=0, keepdims=True)
        acc_ref[hh] = _dot(vt_ref[hh, t], p.astype(BF16))

    def two_phases(i, carry):
        pair = t - 1 - 2 * i
        phase(pair, 0)
        phase(pair - 1, 1)
        return carry

    lax.fori_loop(0, t // 2, two_phases, 0)

    @pl.when(t % 2 == 1)
    def _():
        phase(0, 0, fetch=False)

    for hh in range(NH):
        o_ref[0, :, hh * Dh:(hh + 1) * Dh] = (acc_ref[hh] / l_ref[hh]).T.astype(BF16)


def _cast_slab_grid(rows, cols, n_steps):
    nc = 1
    while nc <= n_steps:
        nr = n_steps // nc
        if rows % (nr * 16) == 0 and cols % (nc * LANES) == 0:
            return nr, nc
        nc *= 2
    raise ValueError(f"no slab split of {(rows, cols)} for {n_steps} steps")


def _moba(qkv3, bias_a, weights, layer):
    b, s, _ = qkv3.shape
    P = MOBA_PAIR
    Dh = A_HEAD_DIM
    NH = MOBA_HEADS_PER_STEP
    G = A_HEADS // NH
    T = s // P
    nb = s // MOBA_BLOCK
    assert nb % 16 == 0 and nb <= LANES
    n_steps = b * G * T
    src_specs, dst_specs, dst_shapes = [], [], []
    for w in weights:
        _, r, c = w.shape
        nr, nc = _cast_slab_grid(r, c, n_steps)

        def slab(bi, h, t, nc=nc):
            step = (bi * G + h) * T + t
            return step // nc, step % nc

        src_specs.append(pl.BlockSpec((None, r // nr, c // nc),
                                      lambda bi, h, t, slab=slab: (layer, *slab(bi, h, t))))
        dst_specs.append(pl.BlockSpec((r // nr, c // nc), slab))
        dst_shapes.append(jax.ShapeDtypeStruct((r, c), BF16))
    outs = pl.pallas_call(
        functools.partial(_moba_kernel, n_casts=len(weights)),
        grid=(b, G, T),
        in_specs=[
            pl.BlockSpec((1, P, NH * Dh), lambda bi, h, t: (bi, t, h)),
            pl.BlockSpec((1, s, NH * Dh), lambda bi, h, t: (bi, 0, G + h)),
            pl.BlockSpec((1, s, NH * Dh), lambda bi, h, t: (bi, 0, 2 * G + h)),
            pl.BlockSpec((NH, P, P), lambda bi, h, t: (h, 0, 0)),
            pl.BlockSpec((NH, LANES, LANES), lambda bi, h, t: (h, 1, 2)),
        ] + src_specs,
        out_specs=[pl.BlockSpec((1, P, NH * Dh), lambda bi, h, t: (bi, t, h))] + dst_specs,
        out_shape=[jax.ShapeDtypeStruct((b, s, A_WIDTH), BF16)] + dst_shapes,
        scratch_shapes=[
            pltpu.VMEM((NH, nb, Dh), BF16),
            pltpu.VMEM((NH, nb, Dh), BF16),
            pltpu.VMEM((NH, s // P, Dh, P), BF16),
            pltpu.VMEM((NH, P, Dh + LANES), BF16),
            pltpu.VMEM((2, NH, P, P), F32),
            pltpu.VMEM((NH, 1, P), F32),
            pltpu.VMEM((NH, 1, P), F32),
            pltpu.VMEM((NH, Dh, P), F32),
        ],
        compiler_params=pltpu.CompilerParams(
            dimension_semantics=("arbitrary", "arbitrary", "arbitrary"),
            vmem_limit_bytes=VMEM_LIMIT_BYTES),
        name="moba",
    )(qkv3, qkv3, qkv3, bias_a, bias_a, *weights)
    return outs[0], outs[1:]


def _swa_kernel(q_ref, kvx_ref, bias_ref, sink_ref, o_ref, vt_ref):
    t = pl.program_id(1)
    W = WINDOW
    n_win = q_ref.shape[1] // W
    seq = kvx_ref.shape[1]
    pairs_per_kv = B_Q_HEADS // B_KV_HEADS // 2
    n_combo = 2 * B_KV_HEADS
    chunk = 4 * W

    @pl.when(t == 0)
    def _():
        er = lax.broadcasted_iota(jnp.int32, (LANES, LANES), 0)
        ec = lax.broadcasted_iota(jnp.int32, (LANES, LANES), 1)
        eye = jnp.where(er == ec, 1.0, 0.0).astype(BF16)
        for i in range(n_combo):
            col = (n_combo + i) * LANES
            for c in range(seq // chunk):
                vt = _dot_nt(eye, kvx_ref[0, c * chunk:(c + 1) * chunk, col:col + LANES])
                for k in range(chunk // W):
                    vt_ref[i, c * (chunk // W) + k] = vt[:, k * W:(k + 1) * W].astype(BF16)

    def band(w):
        n = t * n_win + w
        return jnp.maximum(n - 1, 0), (n == 0).astype(jnp.int32)

    def window_scores(w):
        sb, first = band(w)
        start = pl.multiple_of(sb * W, W)
        scores = []
        for kv in range(B_KV_HEADS):
            width = pairs_per_kv * LANES
            qg = q_ref[0, w * W:(w + 1) * W, kv * width:(kv + 1) * width]
            qs = jnp.concatenate(
                [qg[:, k * LANES:(k + 1) * LANES] for k in range(pairs_per_kv)], axis=0)
            for par in range(2):
                kcol = (2 * kv + par) * LANES
                kk = kvx_ref[0, pl.ds(start, 2 * W), kcol:kcol + LANES]
                scores.append(_dot_nt(kk, qs) + bias_ref[first, kv, par])
        return scores

    def window_output(w, scores):
        sb, _ = band(w)
        outs = []
        for i, s in enumerate(scores):
            sink = sink_ref[pl.ds(i, 1), :]
            m = jnp.maximum(jnp.max(s, axis=0, keepdims=True), sink)
            pe = jnp.exp2(s - m)
            l = jnp.sum(pe, axis=0, keepdims=True) + jnp.exp2(sink - m)
            vt = jnp.concatenate([vt_ref[i, sb], vt_ref[i, sb + 1]], axis=1)
            outs.append(_dot(vt, pe.astype(BF16)) / l)
        for kv in range(B_KV_HEADS):
            ot = outs[2 * kv] + outs[2 * kv + 1]
            for k in range(pairs_per_kv):
                p = kv * pairs_per_kv + k
                o_ref[0, w * W:(w + 1) * W, p * LANES:(p + 1) * LANES] = (
                    ot[:, k * W:(k + 1) * W].T.astype(BF16))

    scores = window_scores(0)
    for w in range(n_win):
        nxt = window_scores(w + 1) if w + 1 < n_win else None
        window_output(w, scores)
        scores = nxt


def _swa(qkv3, kvx3, bias_b, sink_rows):
    b, s, _ = qkv3.shape
    qb_blk = 3 * A_WIDTH // B_WIDTH
    return pl.pallas_call(
        _swa_kernel,
        grid=(b, s // SWA_TQ),
        in_specs=[
            pl.BlockSpec((1, SWA_TQ, B_WIDTH), lambda bi, t: (bi, t, qb_blk)),
            pl.BlockSpec((1, s, KVX_WIDTH), lambda bi, t: (bi, 0, 0)),
            pl.BlockSpec(bias_b.shape, lambda bi, t: (0, 0, 0, 0, 0)),
            pl.BlockSpec(sink_rows.shape, lambda bi, t: (0, 0)),
        ],
        out_specs=pl.BlockSpec((1, SWA_TQ, B_WIDTH), lambda bi, t: (bi, t, 0)),
        out_shape=jax.ShapeDtypeStruct((b, s, B_WIDTH), BF16),
        scratch_shapes=[pltpu.VMEM((2 * B_KV_HEADS, s // WINDOW, LANES, WINDOW), BF16)],
        compiler_params=pltpu.CompilerParams(
            dimension_semantics=("arbitrary", "arbitrary"),
            vmem_limit_bytes=VMEM_LIMIT_BYTES),
        name="swa",
    )(qkv3, kvx3, bias_b, sink_rows)


def _merge_kernel(ya_ref, yb_ref, gt_ref, x_ref, wa_ref, wb_ref, wo_ref, o_ref, mg_ref):
    ch = MERGE_CH
    for c in range(D_MODEL // ch):
        cols = slice(c * ch, (c + 1) * ch)
        ta = _dot(ya_ref[...], wa_ref[:, cols])
        tb = _dot(yb_ref[...], wb_ref[:, cols])
        ga = _sigmoid(gt_ref[:, cols].astype(F32))
        gb = _sigmoid(gt_ref[:, D_MODEL + c * ch:D_MODEL + (c + 1) * ch].astype(F32))
        mg_ref[:, cols] = (ga * ta + gb * tb).astype(BF16)
    for c in range(D_MODEL // ch):
        cols = slice(c * ch, (c + 1) * ch)
        o_ref[:, cols] = x_ref[:, cols] + _dot(mg_ref[...], wo_ref[:, cols])


def _merge(ya, yb, gates, x2, wa, wb, wo):
    n = x2.shape[0]
    tm = MERGE_TM
    resident = functools.partial(pl.BlockSpec, pipeline_mode=pl.Buffered(1))
    return pl.pallas_call(
        _merge_kernel,
        grid=(n // tm,),
        in_specs=[
            pl.BlockSpec((tm, A_WIDTH), lambda i: (i, 0)),
            pl.BlockSpec((tm, B_WIDTH), lambda i: (i, 0)),
            pl.BlockSpec((tm, 2 * D_MODEL), lambda i: (i, 0)),
            pl.BlockSpec((tm, D_MODEL), lambda i: (i, 0)),
            resident((A_WIDTH, D_MODEL), lambda i: (0, 0)),
            resident((B_WIDTH, D_MODEL), lambda i: (0, 0)),
            resident((D_MODEL, D_MODEL), lambda i: (0, 0)),
        ],
        out_specs=pl.BlockSpec((tm, D_MODEL), lambda i: (i, 0)),
        out_shape=jax.ShapeDtypeStruct((n, D_MODEL), F32),
        scratch_shapes=[pltpu.VMEM((tm, D_MODEL), BF16)],
        compiler_params=pltpu.CompilerParams(
            dimension_semantics=("arbitrary",),
            vmem_limit_bytes=VMEM_LIMIT_BYTES),
        name="merge",
    )(ya, yb, gates, x2, wa, wb, wo)


def _ffn_kernel(x_ref, g2_ref, wg_ref, wu_ref, wd_ref, o_ref, h_ref):
    f = pl.program_id(1)

    @pl.when(f == 0)
    def _():
        x = x_ref[...]
        ms = jnp.mean(x * x, axis=-1, keepdims=True)
        h_ref[...] = (x * lax.rsqrt(ms + EPS) * g2_ref[...]).astype(BF16)
        o_ref[...] = x

    h = h_ref[...]
    g = _dot(h, wg_ref[...])
    u = _dot(h, wu_ref[...])
    a = (g * _sigmoid(g) * u).astype(BF16)
    o_ref[...] += _dot(a, wd_ref[...])


def _ffn(x1, g2, wgu, wd):
    n = x1.shape[0]
    tm, tf = FFN_TM, FFN_TF
    nf = D_FF // tf
    return pl.pallas_call(
        _ffn_kernel,
        grid=(n // tm, nf),
        in_specs=[
            pl.BlockSpec((tm, D_MODEL), lambda i, f: (i, 0)),
            pl.BlockSpec((1, D_MODEL), lambda i, f: (0, 0)),
            pl.BlockSpec((D_MODEL, tf), lambda i, f: (0, f)),
            pl.BlockSpec((D_MODEL, tf), lambda i, f: (0, nf + f)),
            pl.BlockSpec((tf, D_MODEL), lambda i, f: (f, 0)),
        ],
        out_specs=pl.BlockSpec((tm, D_MODEL), lambda i, f: (i, 0)),
        out_shape=jax.ShapeDtypeStruct((n, D_MODEL), F32),
        scratch_shapes=[pltpu.VMEM((tm, D_MODEL), BF16)],
        compiler_params=pltpu.CompilerParams(
            dimension_semantics=("arbitrary", "arbitrary"),
            vmem_limit_bytes=VMEM_LIMIT_BYTES),
        name="ffn",
    )(x1, g2, wgu, wgu, wd)


def _moba_bucket_tables():
    L = MOBA_BLOCK
    kj = np.arange(L)[:, None]
    qi = np.arange(L)[None, :]
    return np.stack([_t5_bucket_np(qi - kj), _t5_bucket_np(L + qi - kj)])


def _swa_bias_tables():
    W = WINDOW
    qi = np.arange(W)[None, :]
    c = np.arange(2 * W)[:, None]
    d = W + qi - c
    v = (d >= 0) & (d < W)
    d0 = qi - c
    v0 = (d0 >= 0) & (d0 < W)
    buckets = np.stack([_t5_bucket_np(d), _t5_bucket_np(d0)])
    valid = np.stack([v, v0]).astype(np.int32)
    return buckets, valid


def kernel(x, norm1_g, w_in, q_norm_a, k_norm_a, q_norm_b, k_norm_b, rel_bias, sinks,
           w_branch_a, w_branch_b, w_out, norm2_g, w_gate_up, w_down):
    b, s, d = x.shape
    depth = w_in.shape[0]
    assert d == D_MODEL and s % MOBA_PAIR == 0 and s % SWA_TQ == 0
    assert (b * s) % IN_TM == 0 and (b * s) % MERGE_TM == 0 and (b * s) % FFN_TM == 0
    far = _t5_bucket_np(np.arange(LANES + 1, s + MOBA_PAIR))
    assert (far == far[0]).all()
    far_bucket = int(far[0])

    bias_a = _moba_bias(rel_bias, jnp.asarray(_moba_bucket_tables()), far_bucket)
    bk_b, va_b = _swa_bias_tables()
    bias_b = _swa_bias(rel_bias, jnp.asarray(bk_b), jnp.asarray(va_b))
    x2 = x.reshape(b * s, d)
    for l in range(depth):
        colgain = _inproj_col_gains(q_norm_a[l], k_norm_a[l], q_norm_b[l], k_norm_b[l])
        qkv, gates, kvx = _inproj(x2, norm1_g[l][None, :], w_in[l], colgain)
        qkv3 = qkv.reshape(b, s, -1)
        ya, (wa, wb, wo, wgu, wd) = _moba(
            qkv3, bias_a, [w_branch_a, w_branch_b, w_out, w_gate_up, w_down], l)
        per_kv = B_Q_HEADS // B_KV_HEADS
        sink_rows = jnp.repeat(
            sinks[l].reshape(B_KV_HEADS, per_kv // 2, 2).transpose(0, 2, 1).reshape(2 * B_KV_HEADS, -1),
            WINDOW, axis=1) * LOG2E
        sink_rows = jnp.pad(sink_rows, ((0, 8 - 2 * B_KV_HEADS), (0, 0)))
        yb = _swa(qkv3, kvx.reshape(b, s, -1), bias_b, sink_rows)
        x1 = _merge(ya.reshape(b * s, -1), yb.reshape(b * s, -1), gates, x2, wa, wb, wo)
        x2 = _ffn(x1, norm2_g[l][None, :], wgu, wd)
    return x2.reshape(b, s, d)
```

```python
import functools
import math

import jax
import jax.numpy as jnp
import numpy as np
from jax import lax
from jax.experimental import pallas as pl
from jax.experimental.pallas import tpu as pltpu

F32 = jnp.float32
BF16 = jnp.bfloat16

D_MODEL = 2048
A_HEADS = 8
A_HEAD_DIM = 128
A_WIDTH = A_HEADS * A_HEAD_DIM
MOBA_BLOCK = 256
MOBA_TOPK = 3
B_Q_HEADS = 16
B_KV_HEADS = 2
B_HEAD_DIM = 64
B_WIDTH = B_Q_HEADS * B_HEAD_DIM
B_KV_WIDTH = B_KV_HEADS * B_HEAD_DIM
WINDOW = 128
NUM_BUCKETS = 32
MAX_DISTANCE = 128
MAX_EXACT = NUM_BUCKETS // 2
D_FF = -(-(8 * D_MODEL) // (3 * 256)) * 256
IN_WIDTH = 3 * A_WIDTH + B_WIDTH + 2 * B_KV_WIDTH + 2 * D_MODEL
EPS = 1e-6
NEG = -1e30
LOG2E = math.log2(math.e)

LANES = 128
VMEM_LIMIT_BYTES = 56 * 1024 * 1024

IN_TN = 1024
IN_QKV_TILES = (3 * A_WIDTH + B_WIDTH) // IN_TN
IN_GATE_TILES = (2 * D_MODEL) // IN_TN
IN_TILES = IN_QKV_TILES + IN_GATE_TILES + 1
IN_KV_COL = 3 * A_WIDTH + B_WIDTH
IN_GATE_COL = IN_KV_COL + 2 * B_KV_WIDTH
IN_TM = 1024
KVX_WIDTH = 8 * LANES

MOBA_PAIR = 2 * MOBA_BLOCK
MOBA_HEADS_PER_STEP = 4
SWA_TQ = 512
MERGE_TM = 512
MERGE_CH = 512
FFN_TM = 1024
FFN_TF = 512


def _dot(a, b):
    return jnp.dot(a, b, preferred_element_type=F32)


def _dot_nt(a, b):
    return lax.dot_general(a, b, (((1,), (1,)), ((), ())), preferred_element_type=F32)


def _sigmoid(v):
    return 0.5 * jnp.tanh(0.5 * v) + 0.5


def _t5_bucket_np(dist):
    n = np.maximum(dist, 0)
    nf = np.maximum(n, 1).astype(np.float32)
    large = MAX_EXACT + (np.log(nf / np.float32(MAX_EXACT))
                         / np.float32(math.log(MAX_DISTANCE / MAX_EXACT))
                         * np.float32(NUM_BUCKETS - MAX_EXACT)).astype(np.int32)
    large = np.minimum(large, NUM_BUCKETS - 1)
    return np.where(n < MAX_EXACT, n, large).astype(np.int32)


def _bucket_lookup(tab_ref, bm, h):
    acc = jnp.zeros(bm.shape, F32)
    for b in range(NUM_BUCKETS):
        acc = jnp.where(bm == b, tab_ref[b, h], acc)
    return acc


def _swa_bias_kernel(tab_ref, bucket_ref, valid_ref, o_ref):
    h = pl.program_id(0) + A_HEADS
    for t in range(bucket_ref.shape[0]):
        o_ref[t, 0, 0] = jnp.where(valid_ref[t] != 0,
                                   _bucket_lookup(tab_ref, bucket_ref[t], h) * LOG2E, NEG)


def _swa_bias(rel_bias, buckets, valid):
    n_tiles, r, c = buckets.shape
    per_kv = B_Q_HEADS // B_KV_HEADS
    return pl.pallas_call(
        _swa_bias_kernel,
        grid=(B_Q_HEADS,),
        in_specs=[
            pl.BlockSpec(memory_space=pltpu.SMEM),
            pl.BlockSpec((n_tiles, r, c), lambda h: (0, 0, 0)),
            pl.BlockSpec((n_tiles, r, c), lambda h: (0, 0, 0)),
        ],
        out_specs=pl.BlockSpec(
            (n_tiles, 1, 1, r, c),
            lambda h: (0, h // per_kv, h % 2, 0, (h % per_kv) // 2)),
        out_shape=jax.ShapeDtypeStruct((n_tiles, B_KV_HEADS, 2, r, c * per_kv // 2), F32),
        name="swa_bias",
    )(rel_bias, buckets, valid)


def _moba_bias_kernel(tab_ref, bucket_ref, o_ref, *, shift_bucket):
    h = pl.program_id(0)
    L = MOBA_BLOCK
    shift = tab_ref[shift_bucket, h]
    tiles = [(_bucket_lookup(tab_ref, bucket_ref[t], h) - shift) * LOG2E for t in range(2)]
    kk = lax.broadcasted_iota(jnp.int32, (L, L), 0)
    qq = lax.broadcasted_iota(jnp.int32, (L, L), 1)
    own = jnp.where(qq >= kk, tiles[0], NEG)
    o_ref[0, :L, :L] = own
    o_ref[0, L:, L:] = own
    o_ref[0, :L, L:] = tiles[1]
    o_ref[0, L:, :L] = jnp.full((L, L), NEG, F32)


def _moba_bias(rel_bias, buckets, shift_bucket):
    L = MOBA_BLOCK
    return pl.pallas_call(
        functools.partial(_moba_bias_kernel, shift_bucket=shift_bucket),
        grid=(A_HEADS,),
        in_specs=[
            pl.BlockSpec(memory_space=pltpu.SMEM),
            pl.BlockSpec((2, L, L), lambda h: (0, 0, 0)),
        ],
        out_specs=pl.BlockSpec((1, 2 * L, 2 * L), lambda h: (h, 0, 0)),
        out_shape=jax.ShapeDtypeStruct((A_HEADS, 2 * L, 2 * L), F32),
        name="moba_bias",
    )(rel_bias, buckets)


def _inproj_kernel(x_ref, g1_ref, w_ref, wkv_ref, cg_ref,
                   qkv_ref, gates_ref, kvx_ref, h_ref):
    j = pl.program_id(1)
    jt = j - 1
    cg = cg_ref[pl.ds(jnp.maximum(jt, 0), 1), :]

    def project(cols=slice(None), rows=slice(None)):
        return _dot(h_ref[rows, :], w_ref[:, cols].astype(BF16))

    def normed_tile(dh):
        n_chunks = 4
        rc = IN_TM // n_chunks
        for r in range(n_chunks):
            rows = slice(r * rc, (r + 1) * rc)
            qkv_ref[rows, :] = head_norm(project(rows=rows), dh, cg).astype(BF16)

    def head_norm(v, dh, gain):
        parts = []
        for k in range(v.shape[1] // LANES):
            vk = v[:, k * LANES:(k + 1) * LANES]
            sq = vk * vk
            if dh == LANES:
                r = lax.rsqrt(jnp.sum(sq, axis=-1, keepdims=True) * (1.0 / dh) + EPS)
            else:
                lo = lax.broadcasted_iota(jnp.int32, vk.shape, 1) < dh
                ss_lo = jnp.sum(jnp.where(lo, sq, 0.0), axis=-1, keepdims=True)
                ss_hi = jnp.sum(jnp.where(lo, 0.0, sq), axis=-1, keepdims=True)
                r = jnp.where(lo, lax.rsqrt(ss_lo * (1.0 / dh) + EPS),
                              lax.rsqrt(ss_hi * (1.0 / dh) + EPS))
            parts.append(vk * r)
        return jnp.concatenate(parts, axis=1) * gain

    n_a = 2 * A_WIDTH // IN_TN
    n_va = 3 * A_WIDTH // IN_TN

    @pl.when((jt >= 0) & (jt < n_a))
    def _():
        normed_tile(A_HEAD_DIM)

    @pl.when((jt >= n_a) & (jt < n_va))
    def _():
        qkv_ref[...] = project().astype(BF16)

    @pl.when((jt >= n_va) & (jt < IN_QKV_TILES))
    def _():
        normed_tile(B_HEAD_DIM)

    @pl.when(jt >= IN_QKV_TILES)
    def _():
        gates_ref[...] = project().astype(BF16)

    @pl.when(j == 0)
    def _():
        x = x_ref[...]
        ms = jnp.mean(x * x, axis=-1, keepdims=True)
        h_ref[...] = (x * lax.rsqrt(ms + EPS) * g1_ref[...]).astype(BF16)
        y = _dot(h_ref[...], wkv_ref[...].astype(BF16))
        yk = y[:, :LANES]
        br = lax.broadcasted_iota(jnp.int32, (LANES, LANES), 0) // B_HEAD_DIM
        bc = lax.broadcasted_iota(jnp.int32, (LANES, LANES), 1) // B_HEAD_DIM
        ss = _dot((yk * yk).astype(BF16), jnp.where(br == bc, 1.0, 0.0).astype(BF16))
        kn = (yk * lax.rsqrt(ss * (1.0 / B_HEAD_DIM) + EPS)
              * cg_ref[IN_TILES - 1:IN_TILES, :LANES])
        vv = y[:, LANES:2 * LANES]
        lo = lax.broadcasted_iota(jnp.int32, kn.shape, 1) < B_HEAD_DIM
        for base, t in ((0, kn), (4, vv)):
            e0 = jnp.where(lo, t, 0.0)
            o1 = jnp.where(lo, 0.0, t)
            o0 = pltpu.roll(e0, B_HEAD_DIM, 1)
            e1 = pltpu.roll(o1, B_HEAD_DIM, 1)
            for k, piece in enumerate((e0, o0, e1, o1)):
                kvx_ref[:, (base + k) * LANES:(base + k + 1) * LANES] = piece.astype(BF16)


def _inproj_w_col(j):
    t = IN_TN // LANES
    jt = jnp.where(j == 0, IN_TILES - 2, j - 1)
    lane_tile = jnp.where(jt < IN_QKV_TILES, jt * t,
                          IN_GATE_COL // LANES + (jt - IN_QKV_TILES) * t)
    return lane_tile * LANES


def _inproj_col_gains(q_norm_a, k_norm_a, q_norm_b, k_norm_b):
    ones = jnp.ones((IN_TN,), F32)
    rows = (
        [jnp.tile(q_norm_a * (A_HEAD_DIM ** -0.5 * LOG2E), IN_TN // A_HEAD_DIM)]
        * (A_WIDTH // IN_TN)
        + [jnp.tile(k_norm_a, IN_TN // A_HEAD_DIM)] * (A_WIDTH // IN_TN)
        + [ones] * (A_WIDTH // IN_TN)
        + [jnp.tile(q_norm_b * (B_HEAD_DIM ** -0.5 * LOG2E), IN_TN // B_HEAD_DIM)]
        * (B_WIDTH // IN_TN)
        + [ones] * IN_GATE_TILES
        + [jnp.concatenate([jnp.tile(k_norm_b, B_KV_HEADS),
                            jnp.ones((IN_TN - B_KV_WIDTH,), F32)])])
    assert len(rows) == IN_TILES
    return jnp.stack(rows + [ones] * (-IN_TILES % 8))


def _inproj(x2, g1, w, colgain):
    n = x2.shape[0]
    grid = (n // IN_TM, IN_TILES)
    return pl.pallas_call(
        _inproj_kernel,
        grid=grid,
        in_specs=[
            pl.BlockSpec((IN_TM, D_MODEL), lambda i, j: (i, 0)),
            pl.BlockSpec((1, D_MODEL), lambda i, j: (0, 0)),
            pl.BlockSpec((pl.Element(D_MODEL), pl.Element(IN_TN)),
                         lambda i, j: (0, _inproj_w_col(j))),
            pl.BlockSpec((pl.Element(D_MODEL), pl.Element(2 * B_KV_WIDTH)),
                         lambda i, j: (0, IN_KV_COL)),
            pl.BlockSpec(colgain.shape, lambda i, j: (0, 0)),
        ],
        out_specs=[
            pl.BlockSpec((IN_TM, IN_TN), lambda i, j: (i, jnp.clip(j - 1, 0, IN_QKV_TILES - 1))),
            pl.BlockSpec((IN_TM, IN_TN),
                         lambda i, j: (i, jnp.clip(j - 1 - IN_QKV_TILES, 0, IN_GATE_TILES - 1))),
            pl.BlockSpec((IN_TM, KVX_WIDTH), lambda i, j: (i, 0)),
        ],
        out_shape=[
            jax.ShapeDtypeStruct((n, IN_QKV_TILES * IN_TN), BF16),
            jax.ShapeDtypeStruct((n, IN_GATE_TILES * IN_TN), BF16),
            jax.ShapeDtypeStruct((n, KVX_WIDTH), BF16),
        ],
        scratch_shapes=[pltpu.VMEM((IN_TM, D_MODEL), BF16)],
        compiler_params=pltpu.CompilerParams(
            dimension_semantics=("arbitrary", "arbitrary"),
            vmem_limit_bytes=VMEM_LIMIT_BYTES),
        name="inproj",
    )(x2, g1, w, w, colgain)


def _moba_kernel(*refs, n_casts):
    q_ref, k_ref, v_ref, bd_ref, bc_ref = refs[:5]
    cast_src = refs[5:5 + n_casts]
    o_ref = refs[5 + n_casts]
    cast_dst = refs[6 + n_casts:6 + 2 * n_casts]
    kmh_ref, kml_ref, vt_ref, qa_ref, s_ref, m_ref, l_ref, acc_ref = refs[6 + 2 * n_casts:]

    t = pl.program_id(2)
    L = MOBA_BLOCK
    P = MOBA_PAIR
    Dh = A_HEAD_DIM
    NH = MOBA_HEADS_PER_STEP
    seq = k_ref.shape[1]
    nb = seq // L

    @pl.when(t == 0)
    def _():
        r = lax.broadcasted_iota(jnp.int32, (nb, seq), 0)
        c = lax.broadcasted_iota(jnp.int32, (nb, seq), 1)
        ind = jnp.where(lax.shift_right_logical(c, int(math.log2(L))) == r, 1.0, 0.0).astype(BF16)
        er = lax.broadcasted_iota(jnp.int32, (Dh, Dh), 0)
        ec = lax.broadcasted_iota(jnp.int32, (Dh, Dh), 1)
        eye = jnp.where(er == ec, 1.0, 0.0).astype(BF16)
        for hh in range(NH):
            km = _dot(ind, k_ref[0, :, hh * Dh:(hh + 1) * Dh]) * (1.0 / L)
            hi = km.astype(BF16)
            kmh_ref[hh] = hi
            kml_ref[hh] = (km - hi.astype(F32)).astype(BF16)
            for g in range(seq // P):
                vt_ref[hh, g] = _dot_nt(
                    eye, v_ref[0, g * P:(g + 1) * P, hh * Dh:(hh + 1) * Dh]).astype(BF16)

    lane = lax.broadcasted_iota(jnp.int32, (P, LANES), 1)
    row = lax.broadcasted_iota(jnp.int32, (P, LANES), 0)
    upper = (row >= L).astype(jnp.int32)

    def select(hh):
        q = q_ref[0, :, hh * Dh:(hh + 1) * Dh]
        g = _dot_nt(kmh_ref[hh], q) + _dot_nt(kml_ref[hh], q)
        blk = lax.broadcasted_iota(jnp.int32, g.shape, 0)
        blk_f = blk.astype(F32)
        qcol = lax.broadcasted_iota(jnp.int32, g.shape, 1)
        own_blk = 2 * t + (qcol >= L).astype(jnp.int32)
        past = blk < own_blk
        g = jnp.where(past, g, NEG)
        sel = jnp.full(g.shape, NEG, F32)
        for _ in range(MOBA_TOPK):
            mx = jnp.max(g, axis=0, keepdims=True)
            first = jnp.min(jnp.where(g == mx, blk_f, float(LANES)), axis=0, keepdims=True)
            hit = blk_f == first
            sel = jnp.where(hit & past, 0.0, sel)
            g = jnp.where(hit, -jnp.inf, g)
        sel = jnp.where(blk == own_blk, 0.0, sel)
        sel = jnp.concatenate([sel, jnp.zeros((LANES - nb, P), F32)], axis=0)
        qa_ref[hh, :, :Dh] = q
        qa_ref[hh, :, Dh:] = sel.T.astype(BF16)

    def scores(hh, pair):
        r0 = pl.multiple_of(pair * P, P)
        ind = jnp.where(lane == 2 * pair + upper, 1.0, 0.0).astype(BF16)
        k_aug = jnp.concatenate([k_ref[0, pl.ds(r0, P), hh * Dh:(hh + 1) * Dh], ind], axis=1)
        return _dot_nt(k_aug, qa_ref[hh])

    def accumulate(hh, pair, buf):
        m_old = m_ref[hh]
        m_new = jnp.maximum(m_old, jnp.max(s_ref[buf, hh], axis=0, keepdims=True))
        alpha = jnp.exp2(m_old - m_new)
        p = jnp.exp2(s_ref[buf, hh] - m_new)
        l_ref[hh] = alpha * l_ref[hh] + jnp.sum(p, axis=0, keepdims=True)
        acc_ref[hh] = alpha * acc_ref[hh] + _dot(vt_ref[hh, pair], p.astype(BF16))
        m_ref[hh] = m_new

    def phase(pair, buf, fetch=True):
        if fetch:
            for hh in range(NH):
                s_ref[1 - buf, hh] = scores(hh, jnp.maximum(pair - 1, 0))
        for hh in range(NH):
            accumulate(hh, pair, buf)

    for hh in range(NH):
        select(hh)
    for src, dst in zip(cast_src, cast_dst):
        dst[...] = src[...].astype(BF16)
    for hh in range(NH):
        s_ref[1, hh] = scores(hh, t)
    for hh in range(NH):
        s_ref[0, hh] = scores(hh, jnp.maximum(t - 1, 0))
        s_ref[0, hh, P - LANES:, :LANES] += bc_ref[hh]
    for hh in range(NH):
        s = s_ref[1, hh] + bd_ref[hh]
        m0 = jnp.max(s, axis=0, keepdims=True)
        p = jnp.exp2(s - m0)
        m_ref[hh] = m0
        l_ref[hh] = jnp.sum(p, axis=0, keepdims=True)
        acc_ref[hh] = _dot(vt_ref[hh, t], p.astype(BF16))

    def two_phases(i, carry):
        pair = t - 1 - 2 * i
        phase(pair, 0)
        phase(pair - 1, 1)
        return carry

    lax.fori_loop(0, t // 2, two_phases, 0)

    @pl.when(t % 2 == 1)
    def _():
        phase(0, 0, fetch=False)

    for hh in range(NH):
        o_ref[0, :, hh * Dh:(hh + 1) * Dh] = (acc_ref[hh] / l_ref[hh]).T.astype(BF16)


def _cast_slab_grid(rows, cols, n_steps):
    nc = 1
    while nc <= n_steps:
        nr = n_steps // nc
        if rows % (nr * 16) == 0 and cols % (nc * LANES) == 0:
            return nr, nc
        nc *= 2
    raise ValueError(f"no slab split of {(rows, cols)} for {n_steps} steps")


def _moba(qkv3, bias_a, weights, layer):
    b, s, _ = qkv3.shape
    P = MOBA_PAIR
    Dh = A_HEAD_DIM
    NH = MOBA_HEADS_PER_STEP
    G = A_HEADS // NH
    T = s // P
    nb = s // MOBA_BLOCK
    assert nb % 16 == 0 and nb <= LANES
    n_steps = b * G * T
    src_specs, dst_specs, dst_shapes = [], [], []
    for w in weights:
        _, r, c = w.shape
        nr, nc = _cast_slab_grid(r, c, n_steps)

        def slab(bi, h, t, nc=nc):
            step = (bi * G + h) * T + t
            return step // nc, step % nc

        src_specs.append(pl.BlockSpec((None, r // nr, c // nc),
                                      lambda bi, h, t, slab=slab: (layer, *slab(bi, h, t))))
        dst_specs.append(pl.BlockSpec((r // nr, c // nc), slab))
        dst_shapes.append(jax.ShapeDtypeStruct((r, c), BF16))
    outs = pl.pallas_call(
        functools.partial(_moba_kernel, n_casts=len(weights)),
        grid=(b, G, T),
        in_specs=[
            pl.BlockSpec((1, P, NH * Dh), lambda bi, h, t: (bi, t, h)),
            pl.BlockSpec((1, s, NH * Dh), lambda bi, h, t: (bi, 0, G + h)),
            pl.BlockSpec((1, s, NH * Dh), lambda bi, h, t: (bi, 0, 2 * G + h)),
            pl.BlockSpec((NH, P, P), lambda bi, h, t: (h, 0, 0)),
            pl.BlockSpec((NH, LANES, LANES), lambda bi, h, t: (h, 1, 2)),
        ] + src_specs,
        out_specs=[pl.BlockSpec((1, P, NH * Dh), lambda bi, h, t: (bi, t, h))] + dst_specs,
        out_shape=[jax.ShapeDtypeStruct((b, s, A_WIDTH), BF16)] + dst_shapes,
        scratch_shapes=[
            pltpu.VMEM((NH, nb, Dh), BF16),
            pltpu.VMEM((NH, nb, Dh), BF16),
            pltpu.VMEM((NH, s // P, Dh, P), BF16),
            pltpu.VMEM((NH, P, Dh + LANES), BF16),
            pltpu.VMEM((2, NH, P, P), F32),
            pltpu.VMEM((NH, 1, P), F32),
            pltpu.VMEM((NH, 1, P), F32),
            pltpu.VMEM((NH, Dh, P), F32),
        ],
        compiler_params=pltpu.CompilerParams(
            dimension_semantics=("arbitrary", "arbitrary", "arbitrary"),
            vmem_limit_bytes=VMEM_LIMIT_BYTES),
        name="moba",
    )(qkv3, qkv3, qkv3, bias_a, bias_a, *weights)
    return outs[0], outs[1:]


def _swa_kernel(q_ref, kvx_ref, bias_ref, sink_ref, o_ref, vt_ref):
    t = pl.program_id(1)
    W = WINDOW
    n_win = q_ref.shape[1] // W
    seq = kvx_ref.shape[1]
    pairs_per_kv = B_Q_HEADS // B_KV_HEADS // 2
    n_combo = 2 * B_KV_HEADS
    chunk = 4 * W

    @pl.when(t == 0)
    def _():
        er = lax.broadcasted_iota(jnp.int32, (LANES, LANES), 0)
        ec = lax.broadcasted_iota(jnp.int32, (LANES, LANES), 1)
        eye = jnp.where(er == ec, 1.0, 0.0).astype(BF16)
        for i in range(n_combo):
            col = (n_combo + i) * LANES
            for c in range(seq // chunk):
                vt = _dot_nt(eye, kvx_ref[0, c * chunk:(c + 1) * chunk, col:col + LANES])
                for k in range(chunk // W):
                    vt_ref[i, c * (chunk // W) + k] = vt[:, k * W:(k + 1) * W].astype(BF16)

    def band(w):
        n = t * n_win + w
        return jnp.maximum(n - 1, 0), (n == 0).astype(jnp.int32)

    def window_scores(w):
        sb, first = band(w)
        start = pl.multiple_of(sb * W, W)
        scores = []
        for kv in range(B_KV_HEADS):
            width = pairs_per_kv * LANES
            qg = q_ref[0, w * W:(w + 1) * W, kv * width:(kv + 1) * width]
            qs = jnp.concatenate(
                [qg[:, k * LANES:(k + 1) * LANES] for k in range(pairs_per_kv)], axis=0)
            for par in range(2):
                kcol = (2 * kv + par) * LANES
                kk = kvx_ref[0, pl.ds(start, 2 * W), kcol:kcol + LANES]
                scores.append(_dot_nt(kk, qs) + bias_ref[first, kv, par])
        return scores

    def window_output(w, scores):
        sb, _ = band(w)
        outs = []
        for i, s in enumerate(scores):
            sink = sink_ref[pl.ds(i, 1), :]
            m = jnp.maximum(jnp.max(s, axis=0, keepdims=True), sink)
            pe = jnp.exp2(s - m)
            l = jnp.sum(pe, axis=0, keepdims=True) + jnp.exp2(sink - m)
            vt = jnp.concatenate([vt_ref[i, sb], vt_ref[i, sb + 1]], axis=1)
            outs.append(_dot(vt, pe.astype(BF16)) / l)
        for kv in range(B_KV_HEADS):
            ot = outs[2 * kv] + outs[2 * kv + 1]
            for k in range(pairs_per_kv):
                p = kv * pairs_per_kv + k
                o_ref[0, w * W:(w + 1) * W, p * LANES:(p + 1) * LANES] = (
                    ot[:, k * W:(k + 1) * W].T.astype(BF16))

    scores = window_scores(0)
    for w in range(n_win):
        nxt = window_scores(w + 1) if w + 1 < n_win else None
        window_output(w, scores)
        scores = nxt


def _swa(qkv3, kvx3, bias_b, sink_rows):
    b, s, _ = qkv3.shape
    qb_blk = 3 * A_WIDTH // B_WIDTH
    return pl.pallas_call(
        _swa_kernel,
        grid=(b, s // SWA_TQ),
        in_specs=[
            pl.BlockSpec((1, SWA_TQ, B_WIDTH), lambda bi, t: (bi, t, qb_blk)),
            pl.BlockSpec((1, s, KVX_WIDTH), lambda bi, t: (bi, 0, 0)),
            pl.BlockSpec(bias_b.shape, lambda bi, t: (0, 0, 0, 0, 0)),
            pl.BlockSpec(sink_rows.shape, lambda bi, t: (0, 0)),
        ],
        out_specs=pl.BlockSpec((1, SWA_TQ, B_WIDTH), lambda bi, t: (bi, t, 0)),
        out_shape=jax.ShapeDtypeStruct((b, s, B_WIDTH), BF16),
        scratch_shapes=[pltpu.VMEM((2 * B_KV_HEADS, s // WINDOW, LANES, WINDOW), BF16)],
        compiler_params=pltpu.CompilerParams(
            dimension_semantics=("arbitrary", "arbitrary"),
            vmem_limit_bytes=VMEM_LIMIT_BYTES),
        name="swa",
    )(qkv3, kvx3, bias_b, sink_rows)


def _merge_kernel(ya_ref, yb_ref, gt_ref, x_ref, wa_ref, wb_ref, wo_ref, o_ref, mg_ref):
    ch = MERGE_CH
    for c in range(D_MODEL // ch):
        cols = slice(c * ch, (c + 1) * ch)
        ta = _dot(ya_ref[...], wa_ref[:, cols])
        tb = _dot(yb_ref[...], wb_ref[:, cols])
        ga = _sigmoid(gt_ref[:, cols].astype(F32))
        gb = _sigmoid(gt_ref[:, D_MODEL + c * ch:D_MODEL + (c + 1) * ch].astype(F32))
        mg_ref[:, cols] = (ga * ta + gb * tb).astype(BF16)
    for c in range(D_MODEL // ch):
        cols = slice(c * ch, (c + 1) * ch)
        o_ref[:, cols] = x_ref[:, cols] + _dot(mg_ref[...], wo_ref[:, cols])


def _merge(ya, yb, gates, x2, wa, wb, wo):
    n = x2.shape[0]
    tm = MERGE_TM
    resident = functools.partial(pl.BlockSpec, pipeline_mode=pl.Buffered(1))
    return pl.pallas_call(
        _merge_kernel,
        grid=(n // tm,),
        in_specs=[
            pl.BlockSpec((tm, A_WIDTH), lambda i: (i, 0)),
            pl.BlockSpec((tm, B_WIDTH), lambda i: (i, 0)),
            pl.BlockSpec((tm, 2 * D_MODEL), lambda i: (i, 0)),
            pl.BlockSpec((tm, D_MODEL), lambda i: (i, 0)),
            resident((A_WIDTH, D_MODEL), lambda i: (0, 0)),
            resident((B_WIDTH, D_MODEL), lambda i: (0, 0)),
            resident((D_MODEL, D_MODEL), lambda i: (0, 0)),
        ],
        out_specs=pl.BlockSpec((tm, D_MODEL), lambda i: (i, 0)),
        out_shape=jax.ShapeDtypeStruct((n, D_MODEL), F32),
        scratch_shapes=[pltpu.VMEM((tm, D_MODEL), BF16)],
        compiler_params=pltpu.CompilerParams(
            dimension_semantics=("arbitrary",),
            vmem_limit_bytes=VMEM_LIMIT_BYTES),
        name="merge",
    )(ya, yb, gates, x2, wa, wb, wo)


def _ffn_kernel(x_ref, g2_ref, wg_ref, wu_ref, wd_ref, o_ref, h_ref):
    f = pl.program_id(1)

    @pl.when(f == 0)
    def _():
        x = x_ref[...]
        ms = jnp.mean(x * x, axis=-1, keepdims=True)
        h_ref[...] = (x * lax.rsqrt(ms + EPS) * g2_ref[...]).astype(BF16)
        o_ref[...] = x

    h = h_ref[...]
    g = _dot(h, wg_ref[...])
    u = _dot(h, wu_ref[...])
    a = (g * _sigmoid(g) * u).astype(BF16)
    o_ref[...] += _dot(a, wd_ref[...])


def _ffn(x1, g2, wgu, wd):
    n = x1.shape[0]
    tm, tf = FFN_TM, FFN_TF
    nf = D_FF // tf
    return pl.pallas_call(
        _ffn_kernel,
        grid=(n // tm, nf),
        in_specs=[
            pl.BlockSpec((tm, D_MODEL), lambda i, f: (i, 0)),
            pl.BlockSpec((1, D_MODEL), lambda i, f: (0, 0)),
            pl.BlockSpec((D_MODEL, tf), lambda i, f: (0, f)),
            pl.BlockSpec((D_MODEL, tf), lambda i, f: (0, nf + f)),
            pl.BlockSpec((tf, D_MODEL), lambda i, f: (f, 0)),
        ],
        out_specs=pl.BlockSpec((tm, D_MODEL), lambda i, f: (i, 0)),
        out_shape=jax.ShapeDtypeStruct((n, D_MODEL), F32),
        scratch_shapes=[pltpu.VMEM((tm, D_MODEL), BF16)],
        compiler_params=pltpu.CompilerParams(
            dimension_semantics=("arbitrary", "arbitrary"),
            vmem_limit_bytes=VMEM_LIMIT_BYTES),
        name="ffn",
    )(x1, g2, wgu, wgu, wd)


def _moba_bucket_tables():
    L = MOBA_BLOCK
    kj = np.arange(L)[:, None]
    qi = np.arange(L)[None, :]
    return np.stack([_t5_bucket_np(qi - kj), _t5_bucket_np(L + qi - kj)])


def _swa_bias_tables():
    W = WINDOW
    qi = np.arange(W)[None, :]
    c = np.arange(2 * W)[:, None]
    d = W + qi - c
    v = (d >= 0) & (d < W)
    d0 = qi - c
    v0 = (d0 >= 0) & (d0 < W)
    buckets = np.stack([_t5_bucket_np(d), _t5_bucket_np(d0)])
    valid = np.stack([v, v0]).astype(np.int32)
    return buckets, valid


def kernel(x, norm1_g, w_in, q_norm_a, k_norm_a, q_norm_b, k_norm_b, rel_bias, sinks,
           w_branch_a, w_branch_b, w_out, norm2_g, w_gate_up, w_down):
    b, s, d = x.shape
    depth = w_in.shape[0]
    assert d == D_MODEL and s % MOBA_PAIR == 0 and s % SWA_TQ == 0
    assert (b * s) % IN_TM == 0 and (b * s) % MERGE_TM == 0 and (b * s) % FFN_TM == 0
    far = _t5_bucket_np(np.arange(LANES + 1, s + MOBA_PAIR))
    assert (far == far[0]).all()
    far_bucket = int(far[0])

    bias_a = _moba_bias(rel_bias, jnp.asarray(_moba_bucket_tables()), far_bucket)
    bk_b, va_b = _swa_bias_tables()
    bias_b = _swa_bias(rel_bias, jnp.asarray(bk_b), jnp.asarray(va_b))
    x2 = x.reshape(b * s, d)
    for l in range(depth):
        colgain = _inproj_col_gains(q_norm_a[l], k_norm_a[l], q_norm_b[l], k_norm_b[l])
        qkv, gates, kvx = _inproj(x2, norm1_g[l][None, :], w_in[l], colgain)
        qkv3 = qkv.reshape(b, s, -1)
        ya, (wa, wb, wo, wgu, wd) = _moba(
            qkv3, bias_a, [w_branch_a, w_branch_b, w_out, w_gate_up, w_down], l)
        per_kv = B_Q_HEADS // B_KV_HEADS
        sink_rows = jnp.repeat(
            sinks[l].reshape(B_KV_HEADS, per_kv // 2, 2).transpose(0, 2, 1).reshape(2 * B_KV_HEADS, -1),
            WINDOW, axis=1) * LOG2E
        sink_rows = jnp.pad(sink_rows, ((0, 8 - 2 * B_KV_HEADS), (0, 0)))
        yb = _swa(qkv3, kvx.reshape(b, s, -1), bias_b, sink_rows)
        x1 = _merge(ya.reshape(b * s, -1), yb.reshape(b * s, -1), gates, x2, wa, wb, wo)
        x2 = _ffn(x1, norm2_g[l][None, :], wgu, wd)
    return x2.reshape(b, s, d)
```

```python
import functools
import math

import jax
import jax.numpy as jnp
import numpy as np
from jax import lax
from jax.experimental import pallas as pl
from jax.experimental.pallas import tpu as pltpu

F32 = jnp.float32
BF16 = jnp.bfloat16

D_MODEL = 2048
A_HEADS = 8
A_HEAD_DIM = 128
A_WIDTH = A_HEADS * A_HEAD_DIM
MOBA_BLOCK = 256
MOBA_TOPK = 3
B_Q_HEADS = 16
B_KV_HEADS = 2
B_HEAD_DIM = 64
B_WIDTH = B_Q_HEADS * B_HEAD_DIM
B_KV_WIDTH = B_KV_HEADS * B_HEAD_DIM
WINDOW = 128
NUM_BUCKETS = 32
MAX_DISTANCE = 128
MAX_EXACT = NUM_BUCKETS // 2
D_FF = -(-(8 * D_MODEL) // (3 * 256)) * 256
IN_WIDTH = 3 * A_WIDTH + B_WIDTH + 2 * B_KV_WIDTH + 2 * D_MODEL
EPS = 1e-6
NEG = -1e30
LOG2E = math.log2(math.e)

LANES = 128
VMEM_LIMIT_BYTES = 60 * 1024 * 1024

IN_TN = 1024
IN_QKV_TILES = (3 * A_WIDTH + B_WIDTH) // IN_TN
IN_GATE_TILES = (2 * D_MODEL) // IN_TN
IN_TILES = IN_QKV_TILES + IN_GATE_TILES + 1
IN_KV_COL = 3 * A_WIDTH + B_WIDTH
IN_GATE_COL = IN_KV_COL + 2 * B_KV_WIDTH
IN_TM = 1024
KVX_WIDTH = 8 * LANES

MOBA_PAIR = 2 * MOBA_BLOCK
MOBA_HEADS_PER_STEP = 4
SWA_TQ = 512
MERGE_TM = 512
MERGE_CH = 512
FFN_TM = 1024
FFN_TF = 512


def _dot(a, b):
    return jnp.dot(a, b, preferred_element_type=F32)


def _dot_nt(a, b):
    return lax.dot_general(a, b, (((1,), (1,)), ((), ())), preferred_element_type=F32)


def _sigmoid(v):
    return 0.5 * jnp.tanh(0.5 * v) + 0.5


def _t5_bucket_np(dist):
    n = np.maximum(dist, 0)
    nf = np.maximum(n, 1).astype(np.float32)
    large = MAX_EXACT + (np.log(nf / np.float32(MAX_EXACT))
                         / np.float32(math.log(MAX_DISTANCE / MAX_EXACT))
                         * np.float32(NUM_BUCKETS - MAX_EXACT)).astype(np.int32)
    large = np.minimum(large, NUM_BUCKETS - 1)
    return np.where(n < MAX_EXACT, n, large).astype(np.int32)


def _bucket_lookup(tab_ref, bm, h):
    acc = jnp.zeros(bm.shape, F32)
    for b in range(NUM_BUCKETS):
        acc = jnp.where(bm == b, tab_ref[b, h], acc)
    return acc


def _swa_bias_kernel(tab_ref, bucket_ref, valid_ref, o_ref):
    h = pl.program_id(0) + A_HEADS
    for t in range(bucket_ref.shape[0]):
        o_ref[t, 0, 0] = jnp.where(valid_ref[t] != 0,
                                   _bucket_lookup(tab_ref, bucket_ref[t], h) * LOG2E, NEG)


def _swa_bias(rel_bias, buckets, valid):
    n_tiles, r, c = buckets.shape
    per_kv = B_Q_HEADS // B_KV_HEADS
    return pl.pallas_call(
        _swa_bias_kernel,
        grid=(B_Q_HEADS,),
        in_specs=[
            pl.BlockSpec(memory_space=pltpu.SMEM),
            pl.BlockSpec((n_tiles, r, c), lambda h: (0, 0, 0)),
            pl.BlockSpec((n_tiles, r, c), lambda h: (0, 0, 0)),
        ],
        out_specs=pl.BlockSpec(
            (n_tiles, 1, 1, r, c),
            lambda h: (0, h // per_kv, h % 2, 0, (h % per_kv) // 2)),
        out_shape=jax.ShapeDtypeStruct((n_tiles, B_KV_HEADS, 2, r, c * per_kv // 2), F32),
        name="swa_bias",
    )(rel_bias, buckets, valid)


def _moba_bias_kernel(tab_ref, bucket_ref, o_ref, *, shift_bucket):
    h = pl.program_id(0)
    L = MOBA_BLOCK
    shift = tab_ref[shift_bucket, h]
    tiles = [(_bucket_lookup(tab_ref, bucket_ref[t], h) - shift) * LOG2E for t in range(2)]
    kk = lax.broadcasted_iota(jnp.int32, (L, L), 0)
    qq = lax.broadcasted_iota(jnp.int32, (L, L), 1)
    own = jnp.where(qq >= kk, tiles[0], NEG)
    o_ref[0, :L, :L] = own
    o_ref[0, L:, L:] = own
    o_ref[0, :L, L:] = tiles[1]
    o_ref[0, L:, :L] = jnp.full((L, L), NEG, F32)


def _moba_bias(rel_bias, buckets, shift_bucket):
    L = MOBA_BLOCK
    return pl.pallas_call(
        functools.partial(_moba_bias_kernel, shift_bucket=shift_bucket),
        grid=(A_HEADS,),
        in_specs=[
            pl.BlockSpec(memory_space=pltpu.SMEM),
            pl.BlockSpec((2, L, L), lambda h: (0, 0, 0)),
        ],
        out_specs=pl.BlockSpec((1, 2 * L, 2 * L), lambda h: (h, 0, 0)),
        out_shape=jax.ShapeDtypeStruct((A_HEADS, 2 * L, 2 * L), F32),
        name="moba_bias",
    )(rel_bias, buckets)


def _inproj_kernel(x_ref, g1_ref, w_ref, wkv_ref, cg_ref,
                   qkv_ref, gates_ref, kvx_ref, h_ref):
    j = pl.program_id(1)
    jt = j - 1
    cg = cg_ref[pl.ds(jnp.maximum(jt, 0), 1), :]

    def project(cols=slice(None), rows=slice(None)):
        return _dot(h_ref[rows, :], w_ref[:, cols].astype(BF16))

    def normed_tile(dh):
        n_chunks = 4
        rc = IN_TM // n_chunks
        for r in range(n_chunks):
            rows = slice(r * rc, (r + 1) * rc)
            qkv_ref[rows, :] = head_norm(project(rows=rows), dh, cg).astype(BF16)

    def head_norm(v, dh, gain):
        parts = []
        for k in range(v.shape[1] // LANES):
            vk = v[:, k * LANES:(k + 1) * LANES]
            sq = vk * vk
            if dh == LANES:
                r = lax.rsqrt(jnp.sum(sq, axis=-1, keepdims=True) * (1.0 / dh) + EPS)
            else:
                lo = lax.broadcasted_iota(jnp.int32, vk.shape, 1) < dh
                ss_lo = jnp.sum(jnp.where(lo, sq, 0.0), axis=-1, keepdims=True)
                ss_hi = jnp.sum(jnp.where(lo, 0.0, sq), axis=-1, keepdims=True)
                r = jnp.where(lo, lax.rsqrt(ss_lo * (1.0 / dh) + EPS),
                              lax.rsqrt(ss_hi * (1.0 / dh) + EPS))
            parts.append(vk * r)
        return jnp.concatenate(parts, axis=1) * gain

    n_a = 2 * A_WIDTH // IN_TN
    n_va = 3 * A_WIDTH // IN_TN

    @pl.when((jt >= 0) & (jt < n_a))
    def _():
        normed_tile(A_HEAD_DIM)

    @pl.when((jt >= n_a) & (jt < n_va))
    def _():
        qkv_ref[...] = project().astype(BF16)

    @pl.when((jt >= n_va) & (jt < IN_QKV_TILES))
    def _():
        normed_tile(B_HEAD_DIM)

    @pl.when(jt >= IN_QKV_TILES)
    def _():
        gates_ref[...] = project().astype(BF16)

    @pl.when(j == 0)
    def _():
        x = x_ref[...]
        ms = jnp.mean(x * x, axis=-1, keepdims=True)
        h_ref[...] = (x * lax.rsqrt(ms + EPS) * g1_ref[...]).astype(BF16)
        y = _dot(h_ref[...], wkv_ref[...].astype(BF16))
        yk = y[:, :LANES]
        br = lax.broadcasted_iota(jnp.int32, (LANES, LANES), 0) // B_HEAD_DIM
        bc = lax.broadcasted_iota(jnp.int32, (LANES, LANES), 1) // B_HEAD_DIM
        ss = _dot((yk * yk).astype(BF16), jnp.where(br == bc, 1.0, 0.0).astype(BF16))
        kn = (yk * lax.rsqrt(ss * (1.0 / B_HEAD_DIM) + EPS)
              * cg_ref[IN_TILES - 1:IN_TILES, :LANES])
        vv = y[:, LANES:2 * LANES]
        lo = lax.broadcasted_iota(jnp.int32, kn.shape, 1) < B_HEAD_DIM
        for base, t in ((0, kn), (4, vv)):
            e0 = jnp.where(lo, t, 0.0)
            o1 = jnp.where(lo, 0.0, t)
            o0 = pltpu.roll(e0, B_HEAD_DIM, 1)
            e1 = pltpu.roll(o1, B_HEAD_DIM, 1)
            for k, piece in enumerate((e0, o0, e1, o1)):
                kvx_ref[:, (base + k) * LANES:(base + k + 1) * LANES] = piece.astype(BF16)


def _inproj_w_col(j):
    t = IN_TN // LANES
    jt = jnp.where(j == 0, IN_TILES - 2, j - 1)
    lane_tile = jnp.where(jt < IN_QKV_TILES, jt * t,
                          IN_GATE_COL // LANES + (jt - IN_QKV_TILES) * t)
    return lane_tile * LANES


def _inproj_col_gains(q_norm_a, k_norm_a, q_norm_b, k_norm_b):
    ones = jnp.ones((IN_TN,), F32)
    rows = (
        [jnp.tile(q_norm_a * (A_HEAD_DIM ** -0.5 * LOG2E), IN_TN // A_HEAD_DIM)]
        * (A_WIDTH // IN_TN)
        + [jnp.tile(k_norm_a, IN_TN // A_HEAD_DIM)] * (A_WIDTH // IN_TN)
        + [ones] * (A_WIDTH // IN_TN)
        + [jnp.tile(q_norm_b * (B_HEAD_DIM ** -0.5 * LOG2E), IN_TN // B_HEAD_DIM)]
        * (B_WIDTH // IN_TN)
        + [ones] * IN_GATE_TILES
        + [jnp.concatenate([jnp.tile(k_norm_b, B_KV_HEADS),
                            jnp.ones((IN_TN - B_KV_WIDTH,), F32)])])
    assert len(rows) == IN_TILES
    return jnp.stack(rows + [ones] * (-IN_TILES % 8))


def _inproj(x2, g1, w, colgain):
    n = x2.shape[0]
    grid = (n // IN_TM, IN_TILES)
    return pl.pallas_call(
        _inproj_kernel,
        grid=grid,
        in_specs=[
            pl.BlockSpec((IN_TM, D_MODEL), lambda i, j: (i, 0)),
            pl.BlockSpec((1, D_MODEL), lambda i, j: (0, 0)),
            pl.BlockSpec((pl.Element(D_MODEL), pl.Element(IN_TN)),
                         lambda i, j: (0, _inproj_w_col(j))),
            pl.BlockSpec((pl.Element(D_MODEL), pl.Element(2 * B_KV_WIDTH)),
                         lambda i, j: (0, IN_KV_COL)),
            pl.BlockSpec(colgain.shape, lambda i, j: (0, 0)),
        ],
        out_specs=[
            pl.BlockSpec((IN_TM, IN_TN), lambda i, j: (i, jnp.clip(j - 1, 0, IN_QKV_TILES - 1))),
            pl.BlockSpec((IN_TM, IN_TN),
                         lambda i, j: (i, jnp.clip(j - 1 - IN_QKV_TILES, 0, IN_GATE_TILES - 1))),
            pl.BlockSpec((IN_TM, KVX_WIDTH), lambda i, j: (i, 0)),
        ],
        out_shape=[
            jax.ShapeDtypeStruct((n, IN_QKV_TILES * IN_TN), BF16),
            jax.ShapeDtypeStruct((n, IN_GATE_TILES * IN_TN), BF16),
            jax.ShapeDtypeStruct((n, KVX_WIDTH), BF16),
        ],
        scratch_shapes=[pltpu.VMEM((IN_TM, D_MODEL), BF16)],
        compiler_params=pltpu.CompilerParams(
            dimension_semantics=("arbitrary", "arbitrary"),
            vmem_limit_bytes=VMEM_LIMIT_BYTES),
        name="inproj",
    )(x2, g1, w, w, colgain)


def _moba_kernel(*refs, n_casts):
    q_ref, k_ref, v_ref, bd_ref, bc_ref = refs[:5]
    cast_src = refs[5:5 + n_casts]
    o_ref = refs[5 + n_casts]
    cast_dst = refs[6 + n_casts:6 + 2 * n_casts]
    kmh_ref, kml_ref, vt_ref, qa_ref, s_ref, m_ref, l_ref, acc_ref = refs[6 + 2 * n_casts:]

    t = pl.program_id(2)
    L = MOBA_BLOCK
    P = MOBA_PAIR
    Dh = A_HEAD_DIM
    NH = MOBA_HEADS_PER_STEP
    seq = k_ref.shape[1]
    nb = seq // L

    @pl.when(t == 0)
    def _():
        r = lax.broadcasted_iota(jnp.int32, (nb, seq), 0)
        c = lax.broadcasted_iota(jnp.int32, (nb, seq), 1)
        ind = jnp.where(lax.shift_right_logical(c, int(math.log2(L))) == r, 1.0, 0.0).astype(BF16)
        er = lax.broadcasted_iota(jnp.int32, (Dh, Dh), 0)
        ec = lax.broadcasted_iota(jnp.int32, (Dh, Dh), 1)
        eye = jnp.where(er == ec, 1.0, 0.0).astype(BF16)
        for hh in range(NH):
            km = _dot(ind, k_ref[0, :, hh * Dh:(hh + 1) * Dh]) * (1.0 / L)
            hi = km.astype(BF16)
            kmh_ref[hh] = hi
            kml_ref[hh] = (km - hi.astype(F32)).astype(BF16)
            for g in range(seq // P):
                vt_ref[hh, g] = _dot_nt(
                    eye, v_ref[0, g * P:(g + 1) * P, hh * Dh:(hh + 1) * Dh]).astype(BF16)

    lane = lax.broadcasted_iota(jnp.int32, (P, LANES), 1)
    row = lax.broadcasted_iota(jnp.int32, (P, LANES), 0)
    upper = (row >= L).astype(jnp.int32)

    def select(hh):
        q = q_ref[0, :, hh * Dh:(hh + 1) * Dh]
        g = _dot_nt(kmh_ref[hh], q) + _dot_nt(kml_ref[hh], q)
        blk = lax.broadcasted_iota(jnp.int32, g.shape, 0)
        blk_f = blk.astype(F32)
        qcol = lax.broadcasted_iota(jnp.int32, g.shape, 1)
        own_blk = 2 * t + (qcol >= L).astype(jnp.int32)
        past = blk < own_blk
        g = jnp.where(past, g, NEG)
        sel = jnp.full(g.shape, NEG, F32)
        for _ in range(MOBA_TOPK):
            mx = jnp.max(g, axis=0, keepdims=True)
            first = jnp.min(jnp.where(g == mx, blk_f, float(LANES)), axis=0, keepdims=True)
            hit = blk_f == first
            sel = jnp.where(hit & past, 0.0, sel)
            g = jnp.where(hit, -jnp.inf, g)
        sel = jnp.where(blk == own_blk, 0.0, sel)
        sel = jnp.concatenate([sel, jnp.zeros((LANES - nb, P), F32)], axis=0)
        qa_ref[hh, :, :Dh] = q
        qa_ref[hh, :, Dh:] = sel.T.astype(BF16)

    def scores(hh, pair):
        r0 = pl.multiple_of(pair * P, P)
        ind = jnp.where(lane == 2 * pair + upper, 1.0, 0.0).astype(BF16)
        k_aug = jnp.concatenate([k_ref[0, pl.ds(r0, P), hh * Dh:(hh + 1) * Dh], ind], axis=1)
        return _dot_nt(k_aug, qa_ref[hh])

    def accumulate(hh, pair, buf):
        m_old = m_ref[hh]
        m_new = jnp.maximum(m_old, jnp.max(s_ref[buf, hh], axis=0, keepdims=True))
        alpha = jnp.exp2(m_old - m_new)
        p = jnp.exp2(s_ref[buf, hh] - m_new)
        l_ref[hh] = alpha * l_ref[hh] + jnp.sum(p, axis=0, keepdims=True)
        acc_ref[hh] = alpha * acc_ref[hh] + _dot(vt_ref[hh, pair], p.astype(BF16))
        m_ref[hh] = m_new

    def phase(pair, buf, fetch=True):
        if fetch:
            for hh in range(NH):
                s_ref[1 - buf, hh] = scores(hh, pair - 1)
        for hh in range(NH):
            accumulate(hh, pair, buf)

    for hh in range(NH):
        select(hh)
    for src, dst in zip(cast_src, cast_dst):
        dst[...] = src[...].astype(BF16)
    for hh in range(NH):
        s_ref[1, hh] = scores(hh, t)
    for hh in range(NH):
        s_ref[0, hh] = scores(hh, jnp.maximum(t - 1, 0))
        s_ref[0, hh, P - LANES:, :LANES] += bc_ref[hh]
    for hh in range(NH):
        s = s_ref[1, hh] + bd_ref[hh]
        m0 = jnp.max(s, axis=0, keepdims=True)
        p = jnp.exp2(s - m0)
        m_ref[hh] = m0
        l_ref[hh] = jnp.sum(p, axis=0, keepdims=True)
        acc_ref[hh] = _dot(vt_ref[hh, t], p.astype(BF16))

    def two_phases(i, carry):
        pair = t - 1 - 2 * i
        phase(pair, 0)
        phase(pair - 1, 1)
        return carry

    lax.fori_loop(0, jnp.maximum(t - 1, 0) // 2, two_phases, 0)

    @pl.when(t % 2 == 1)
    def _():
        phase(0, 0, fetch=False)

    @pl.when((t % 2 == 0) & (t >= 2))
    def _():
        phase(1, 0)
        phase(0, 1, fetch=False)

    for hh in range(NH):
        o_ref[0, :, hh * Dh:(hh + 1) * Dh] = (acc_ref[hh] / l_ref[hh]).T.astype(BF16)


def _cast_slab_grid(rows, cols, n_steps):
    nc = 1
    while nc <= n_steps:
        nr = n_steps // nc
        if rows % (nr * 16) == 0 and cols % (nc * LANES) == 0:
            return nr, nc
        nc *= 2
    raise ValueError(f"no slab split of {(rows, cols)} for {n_steps} steps")


def _moba(qkv3, bias_a, weights, layer):
    b, s, _ = qkv3.shape
    P = MOBA_PAIR
    Dh = A_HEAD_DIM
    NH = MOBA_HEADS_PER_STEP
    G = A_HEADS // NH
    T = s // P
    nb = s // MOBA_BLOCK
    assert nb % 16 == 0 and nb <= LANES
    n_steps = b * G * T
    src_specs, dst_specs, dst_shapes = [], [], []
    for w in weights:
        _, r, c = w.shape
        nr, nc = _cast_slab_grid(r, c, n_steps)

        def slab(bi, h, t, nc=nc):
            step = (bi * G + h) * T + t
            return step // nc, step % nc

        src_specs.append(pl.BlockSpec((None, r // nr, c // nc),
                                      lambda bi, h, t, slab=slab: (layer, *slab(bi, h, t))))
        dst_specs.append(pl.BlockSpec((r // nr, c // nc), slab))
        dst_shapes.append(jax.ShapeDtypeStruct((r, c), BF16))
    outs = pl.pallas_call(
        functools.partial(_moba_kernel, n_casts=len(weights)),
        grid=(b, G, T),
        in_specs=[
            pl.BlockSpec((1, P, NH * Dh), lambda bi, h, t: (bi, t, h)),
            pl.BlockSpec((1, s, NH * Dh), lambda bi, h, t: (bi, 0, G + h)),
            pl.BlockSpec((1, s, NH * Dh), lambda bi, h, t: (bi, 0, 2 * G + h)),
            pl.BlockSpec((NH, P, P), lambda bi, h, t: (h, 0, 0)),
            pl.BlockSpec((NH, LANES, LANES), lambda bi, h, t: (h, 1, 2)),
        ] + src_specs,
        out_specs=[pl.BlockSpec((1, P, NH * Dh), lambda bi, h, t: (bi, t, h))] + dst_specs,
        out_shape=[jax.ShapeDtypeStruct((b, s, A_WIDTH), BF16)] + dst_shapes,
        scratch_shapes=[
            pltpu.VMEM((NH, nb, Dh), BF16),
            pltpu.VMEM((NH, nb, Dh), BF16),
            pltpu.VMEM((NH, s // P, Dh, P), BF16),
            pltpu.VMEM((NH, P, Dh + LANES), BF16),
            pltpu.VMEM((2, NH, P, P), F32),
            pltpu.VMEM((NH, 1, P), F32),
            pltpu.VMEM((NH, 1, P), F32),
            pltpu.VMEM((NH, Dh, P), F32),
        ],
        compiler_params=pltpu.CompilerParams(
            dimension_semantics=("arbitrary", "arbitrary", "arbitrary"),
            vmem_limit_bytes=VMEM_LIMIT_BYTES),
        name="moba",
    )(qkv3, qkv3, qkv3, bias_a, bias_a, *weights)
    return outs[0], outs[1:]


def _swa_kernel(q_ref, kvx_ref, bias_ref, sink_ref, o_ref, vt_ref):
    t = pl.program_id(1)
    W = WINDOW
    n_win = q_ref.shape[1] // W
    seq = kvx_ref.shape[1]
    pairs_per_kv = B_Q_HEADS // B_KV_HEADS // 2
    n_combo = 2 * B_KV_HEADS
    chunk = 4 * W

    @pl.when(t == 0)
    def _():
        er = lax.broadcasted_iota(jnp.int32, (LANES, LANES), 0)
        ec = lax.broadcasted_iota(jnp.int32, (LANES, LANES), 1)
        eye = jnp.where(er == ec, 1.0, 0.0).astype(BF16)
        for i in range(n_combo):
            col = (n_combo + i) * LANES
            for c in range(seq // chunk):
                vt = _dot_nt(eye, kvx_ref[0, c * chunk:(c + 1) * chunk, col:col + LANES])
                for k in range(chunk // W):
                    vt_ref[i, c * (chunk // W) + k] = vt[:, k * W:(k + 1) * W].astype(BF16)

    def band(w):
        n = t * n_win + w
        return jnp.maximum(n - 1, 0), (n == 0).astype(jnp.int32)

    def window_scores(w):
        sb, first = band(w)
        start = pl.multiple_of(sb * W, W)
        scores = []
        for kv in range(B_KV_HEADS):
            width = pairs_per_kv * LANES
            qg = q_ref[0, w * W:(w + 1) * W, kv * width:(kv + 1) * width]
            qs = jnp.concatenate(
                [qg[:, k * LANES:(k + 1) * LANES] for k in range(pairs_per_kv)], axis=0)
            for par in range(2):
                kcol = (2 * kv + par) * LANES
                kk = kvx_ref[0, pl.ds(start, 2 * W), kcol:kcol + LANES]
                scores.append(_dot_nt(kk, qs) + bias_ref[first, kv, par])
        return scores

    def window_output(w, scores):
        sb, _ = band(w)
        outs = []
        for i, s in enumerate(scores):
            sink = sink_ref[pl.ds(i, 1), :]
            m = jnp.maximum(jnp.max(s, axis=0, keepdims=True), sink)
            pe = jnp.exp2(s - m)
            l = jnp.sum(pe, axis=0, keepdims=True) + jnp.exp2(sink - m)
            vt = jnp.concatenate([vt_ref[i, sb], vt_ref[i, sb + 1]], axis=1)
            outs.append(_dot(vt, pe.astype(BF16)) / l)
        for kv in range(B_KV_HEADS):
            ot = outs[2 * kv] + outs[2 * kv + 1]
            for k in range(pairs_per_kv):
                p = kv * pairs_per_kv + k
                o_ref[0, w * W:(w + 1) * W, p * LANES:(p + 1) * LANES] = (
                    ot[:, k * W:(k + 1) * W].T.astype(BF16))

    scores = window_scores(0)
    for w in range(n_win):
        nxt = window_scores(w + 1) if w + 1 < n_win else None
        window_output(w, scores)
        scores = nxt


def _swa(qkv3, kvx3, bias_b, sink_rows):
    b, s, _ = qkv3.shape
    qb_blk = 3 * A_WIDTH // B_WIDTH
    return pl.pallas_call(
        _swa_kernel,
        grid=(b, s // SWA_TQ),
        in_specs=[
            pl.BlockSpec((1, SWA_TQ, B_WIDTH), lambda bi, t: (bi, t, qb_blk)),
            pl.BlockSpec((1, s, KVX_WIDTH), lambda bi, t: (bi, 0, 0)),
            pl.BlockSpec(bias_b.shape, lambda bi, t: (0, 0, 0, 0, 0)),
            pl.BlockSpec(sink_rows.shape, lambda bi, t: (0, 0)),
        ],
        out_specs=pl.BlockSpec((1, SWA_TQ, B_WIDTH), lambda bi, t: (bi, t, 0)),
        out_shape=jax.ShapeDtypeStruct((b, s, B_WIDTH), BF16),
        scratch_shapes=[pltpu.VMEM((2 * B_KV_HEADS, s // WINDOW, LANES, WINDOW), BF16)],
        compiler_params=pltpu.CompilerParams(
            dimension_semantics=("arbitrary", "arbitrary"),
            vmem_limit_bytes=VMEM_LIMIT_BYTES),
        name="swa",
    )(qkv3, kvx3, bias_b, sink_rows)


def _merge_kernel(ya_ref, yb_ref, gt_ref, x_ref, wa_ref, wb_ref, wo_ref, g2_ref,
                  o_ref, h2_ref, mg_ref):
    ch = MERGE_CH
    for c in range(D_MODEL // ch):
        cols = slice(c * ch, (c + 1) * ch)
        ta = _dot(ya_ref[...], wa_ref[:, cols])
        tb = _dot(yb_ref[...], wb_ref[:, cols])
        ga = _sigmoid(gt_ref[:, cols].astype(F32))
        gb = _sigmoid(gt_ref[:, D_MODEL + c * ch:D_MODEL + (c + 1) * ch].astype(F32))
        mg_ref[:, cols] = (ga * ta + gb * tb).astype(BF16)
    for c in range(D_MODEL // ch):
        cols = slice(c * ch, (c + 1) * ch)
        o_ref[:, cols] = x_ref[:, cols] + _dot(mg_ref[...], wo_ref[:, cols])
    x1 = o_ref[...]
    ms = jnp.mean(x1 * x1, axis=-1, keepdims=True)
    h2_ref[...] = (x1 * lax.rsqrt(ms + EPS) * g2_ref[...]).astype(BF16)


def _merge(ya, yb, gates, x2, wa, wb, wo, g2):
    n = x2.shape[0]
    tm = MERGE_TM
    resident = functools.partial(pl.BlockSpec, pipeline_mode=pl.Buffered(1))
    return pl.pallas_call(
        _merge_kernel,
        grid=(n // tm,),
        in_specs=[
            pl.BlockSpec((tm, A_WIDTH), lambda i: (i, 0)),
            pl.BlockSpec((tm, B_WIDTH), lambda i: (i, 0)),
            pl.BlockSpec((tm, 2 * D_MODEL), lambda i: (i, 0)),
            pl.BlockSpec((tm, D_MODEL), lambda i: (i, 0)),
            resident((A_WIDTH, D_MODEL), lambda i: (0, 0)),
            resident((B_WIDTH, D_MODEL), lambda i: (0, 0)),
            resident((D_MODEL, D_MODEL), lambda i: (0, 0)),
            pl.BlockSpec((1, D_MODEL), lambda i: (0, 0)),
        ],
        out_specs=[pl.BlockSpec((tm, D_MODEL), lambda i: (i, 0)),
                   pl.BlockSpec((tm, D_MODEL), lambda i: (i, 0))],
        out_shape=[jax.ShapeDtypeStruct((n, D_MODEL), F32),
                   jax.ShapeDtypeStruct((n, D_MODEL), BF16)],
        scratch_shapes=[pltpu.VMEM((tm, D_MODEL), BF16)],
        compiler_params=pltpu.CompilerParams(
            dimension_semantics=("arbitrary",),
            vmem_limit_bytes=VMEM_LIMIT_BYTES),
        name="merge",
    )(ya, yb, gates, x2, wa, wb, wo, g2)


def _ffn_kernel(h_ref, x_ref, wg_ref, wu_ref, wd_ref, o_ref):
    f = pl.program_id(1)
    last = pl.num_programs(1) - 1

    @pl.when(f == 0)
    def _():
        o_ref[...] = jnp.zeros(o_ref.shape, F32)

    h = h_ref[...]
    g = _dot(h, wg_ref[...])
    u = _dot(h, wu_ref[...])
    a = (g * _sigmoid(g) * u).astype(BF16)
    o_ref[...] += _dot(a, wd_ref[...]) + jnp.where(f == last, x_ref[...], 0.0)


def _ffn(x1, h2, wgu, wd):
    n = x1.shape[0]
    tm, tf = FFN_TM, FFN_TF
    nf = D_FF // tf
    return pl.pallas_call(
        _ffn_kernel,
        grid=(n // tm, nf),
        in_specs=[
            pl.BlockSpec((tm, D_MODEL), lambda i, f: (i, 0)),
            pl.BlockSpec((tm, D_MODEL),
                         lambda i, f: (jnp.where(f == nf - 1, i, jnp.maximum(i - 1, 0)), 0)),
            pl.BlockSpec((D_MODEL, tf), lambda i, f: (0, f)),
            pl.BlockSpec((D_MODEL, tf), lambda i, f: (0, nf + f)),
            pl.BlockSpec((tf, D_MODEL), lambda i, f: (f, 0)),
        ],
        out_specs=pl.BlockSpec((tm, D_MODEL), lambda i, f: (i, 0)),
        out_shape=jax.ShapeDtypeStruct((n, D_MODEL), F32),
        compiler_params=pltpu.CompilerParams(
            dimension_semantics=("arbitrary", "arbitrary"),
            vmem_limit_bytes=VMEM_LIMIT_BYTES),
        name="ffn",
    )(h2, x1, wgu, wgu, wd)


def _moba_bucket_tables():
    L = MOBA_BLOCK
    kj = np.arange(L)[:, None]
    qi = np.arange(L)[None, :]
    return np.stack([_t5_bucket_np(qi - kj), _t5_bucket_np(L + qi - kj)])


def _swa_bias_tables():
    W = WINDOW
    qi = np.arange(W)[None, :]
    c = np.arange(2 * W)[:, None]
    d = W + qi - c
    v = (d >= 0) & (d < W)
    d0 = qi - c
    v0 = (d0 >= 0) & (d0 < W)
    buckets = np.stack([_t5_bucket_np(d), _t5_bucket_np(d0)])
    valid = np.stack([v, v0]).astype(np.int32)
    return buckets, valid


def kernel(x, norm1_g, w_in, q_norm_a, k_norm_a, q_norm_b, k_norm_b, rel_bias, sinks,
           w_branch_a, w_branch_b, w_out, norm2_g, w_gate_up, w_down):
    b, s, d = x.shape
    depth = w_in.shape[0]
    assert d == D_MODEL and s % MOBA_PAIR == 0 and s % SWA_TQ == 0
    assert (b * s) % IN_TM == 0 and (b * s) % MERGE_TM == 0 and (b * s) % FFN_TM == 0
    far = _t5_bucket_np(np.arange(LANES + 1, s + MOBA_PAIR))
    assert (far == far[0]).all()
    far_bucket = int(far[0])

    bias_a = _moba_bias(rel_bias, jnp.asarray(_moba_bucket_tables()), far_bucket)
    bk_b, va_b = _swa_bias_tables()
    bias_b = _swa_bias(rel_bias, jnp.asarray(bk_b), jnp.asarray(va_b))
    x2 = x.reshape(b * s, d)
    for l in range(depth):
        colgain = _inproj_col_gains(q_norm_a[l], k_norm_a[l], q_norm_b[l], k_norm_b[l])
        qkv, gates, kvx = _inproj(x2, norm1_g[l][None, :], w_in[l], colgain)
        qkv3 = qkv.reshape(b, s, -1)
        ya, (wa, wb, wo, wgu, wd) = _moba(
            qkv3, bias_a, [w_branch_a, w_branch_b, w_out, w_gate_up, w_down], l)
        per_kv = B_Q_HEADS // B_KV_HEADS
        sink_rows = jnp.repeat(
            sinks[l].reshape(B_KV_HEADS, per_kv // 2, 2).transpose(0, 2, 1).reshape(2 * B_KV_HEADS, -1),
            WINDOW, axis=1) * LOG2E
        sink_rows = jnp.pad(sink_rows, ((0, 8 - 2 * B_KV_HEADS), (0, 0)))
        yb = _swa(qkv3, kvx.reshape(b, s, -1), bias_b, sink_rows)
        x1, h2 = _merge(ya.reshape(b * s, -1), yb.reshape(b * s, -1), gates, x2, wa, wb, wo,
                        norm2_g[l][None, :])
        x2 = _ffn(x1, h2, wgu, wd)
    return x2.reshape(b, s, d)
```

```python
import functools
import math

import jax
import jax.numpy as jnp
import numpy as np
from jax import lax
from jax.experimental import pallas as pl
from jax.experimental.pallas import tpu as pltpu

F32 = jnp.float32
BF16 = jnp.bfloat16

D_MODEL = 2048
A_HEADS = 8
A_HEAD_DIM = 128
A_WIDTH = A_HEADS * A_HEAD_DIM
MOBA_BLOCK = 256
MOBA_TOPK = 3
B_Q_HEADS = 16
B_KV_HEADS = 2
B_HEAD_DIM = 64
B_WIDTH = B_Q_HEADS * B_HEAD_DIM
B_KV_WIDTH = B_KV_HEADS * B_HEAD_DIM
WINDOW = 128
NUM_BUCKETS = 32
MAX_DISTANCE = 128
MAX_EXACT = NUM_BUCKETS // 2
D_FF = -(-(8 * D_MODEL) // (3 * 256)) * 256
IN_WIDTH = 3 * A_WIDTH + B_WIDTH + 2 * B_KV_WIDTH + 2 * D_MODEL
EPS = 1e-6
NEG = -1e30
LOG2E = math.log2(math.e)

LANES = 128
SUBLANES_F32 = 8
SUBLANES_BF16 = 16
VMEM_LIMIT_BYTES = 56 * 1024 * 1024

IN_TN = 1024
IN_QKV_TILES = (3 * A_WIDTH + B_WIDTH) // IN_TN
IN_GATE_TILES = (2 * D_MODEL) // IN_TN
IN_TILES = IN_QKV_TILES + IN_GATE_TILES + 1
IN_KV_COL = 3 * A_WIDTH + B_WIDTH
IN_GATE_COL = IN_KV_COL + 2 * B_KV_WIDTH
IN_TM = 1024
KVX_WIDTH = 8 * LANES

MOBA_PAIR = 2 * MOBA_BLOCK
MOBA_HEADS_PER_STEP = 4
SWA_TQ = 512
MERGE_TM = 512
MERGE_CH = 512
FFN_TM = 1024
FFN_TF = 512


def _dot(a, b):
    return jnp.dot(a, b, preferred_element_type=F32)


def _dot_nt(a, b):
    return lax.dot_general(a, b, (((1,), (1,)), ((), ())), preferred_element_type=F32)


def _sigmoid(v):
    return 0.5 * jnp.tanh(0.5 * v) + 0.5


def _t5_bucket_np(dist):
    n = np.maximum(dist, 0)
    nf = np.maximum(n, 1).astype(np.float32)
    large = MAX_EXACT + (np.log(nf / np.float32(MAX_EXACT))
                         / np.float32(math.log(MAX_DISTANCE / MAX_EXACT))
                         * np.float32(NUM_BUCKETS - MAX_EXACT)).astype(np.int32)
    large = np.minimum(large, NUM_BUCKETS - 1)
    return np.where(n < MAX_EXACT, n, large).astype(np.int32)


def _bucket_lookup(tab_ref, bm, h):
    acc = jnp.zeros(bm.shape, F32)
    for b in range(NUM_BUCKETS):
        acc = jnp.where(bm == b, tab_ref[b, h], acc)
    return acc


def _swa_bias_kernel(tab_ref, bucket_ref, valid_ref, o_ref):
    h = pl.program_id(0) + A_HEADS
    for t in range(bucket_ref.shape[0]):
        o_ref[t, 0, 0] = jnp.where(valid_ref[t] != 0,
                                   _bucket_lookup(tab_ref, bucket_ref[t], h) * LOG2E, NEG)


def _swa_bias(rel_bias, buckets, valid):
    n_tiles, r, c = buckets.shape
    per_kv = B_Q_HEADS // B_KV_HEADS
    return pl.pallas_call(
        _swa_bias_kernel,
        grid=(B_Q_HEADS,),
        in_specs=[
            pl.BlockSpec(memory_space=pltpu.SMEM),
            pl.BlockSpec((n_tiles, r, c), lambda h: (0, 0, 0)),
            pl.BlockSpec((n_tiles, r, c), lambda h: (0, 0, 0)),
        ],
        out_specs=pl.BlockSpec(
            (n_tiles, 1, 1, r, c),
            lambda h: (0, h // per_kv, h % 2, 0, (h % per_kv) // 2)),
        out_shape=jax.ShapeDtypeStruct((n_tiles, B_KV_HEADS, 2, r, c * per_kv // 2), F32),
        name="swa_bias",
    )(rel_bias, buckets, valid)


def _moba_bias_kernel(tab_ref, bucket_ref, o_ref, *, shift_bucket):
    h = pl.program_id(0)
    L = MOBA_BLOCK
    shift = tab_ref[shift_bucket, h]
    tiles = [(_bucket_lookup(tab_ref, bucket_ref[t], h) - shift) * LOG2E for t in range(2)]
    kk = lax.broadcasted_iota(jnp.int32, (L, L), 0)
    qq = lax.broadcasted_iota(jnp.int32, (L, L), 1)
    own = jnp.where(qq >= kk, tiles[0], NEG)
    o_ref[0, :L, :L] = own
    o_ref[0, L:, L:] = own
    o_ref[0, :L, L:] = tiles[1]
    o_ref[0, L:, :L] = jnp.full((L, L), NEG, F32)


def _moba_bias(rel_bias, buckets, shift_bucket):
    L = MOBA_BLOCK
    return pl.pallas_call(
        functools.partial(_moba_bias_kernel, shift_bucket=shift_bucket),
        grid=(A_HEADS,),
        in_specs=[
            pl.BlockSpec(memory_space=pltpu.SMEM),
            pl.BlockSpec((2, L, L), lambda h: (0, 0, 0)),
        ],
        out_specs=pl.BlockSpec((1, 2 * L, 2 * L), lambda h: (h, 0, 0)),
        out_shape=jax.ShapeDtypeStruct((A_HEADS, 2 * L, 2 * L), F32),
        name="moba_bias",
    )(rel_bias, buckets)


def _inproj_kernel(x_ref, g1_ref, w_ref, wkv_ref, cg_ref,
                   qkv_ref, gates_ref, kvx_ref, h_ref):
    j = pl.program_id(1)
    jt = j - 1
    cg = cg_ref[pl.ds(jnp.maximum(jt, 0), 1), :]

    def project(cols=slice(None), rows=slice(None)):
        return _dot(h_ref[rows, :], w_ref[:, cols].astype(BF16))

    def normed_tile(dh):
        n_chunks = 4
        rc = IN_TM // n_chunks
        for r in range(n_chunks):
            rows = slice(r * rc, (r + 1) * rc)
            qkv_ref[rows, :] = head_norm(project(rows=rows), dh, cg).astype(BF16)

    def head_norm(v, dh, gain):
        parts = []
        for k in range(v.shape[1] // LANES):
            vk = v[:, k * LANES:(k + 1) * LANES]
            sq = vk * vk
            if dh == LANES:
                r = lax.rsqrt(jnp.sum(sq, axis=-1, keepdims=True) * (1.0 / dh) + EPS)
            else:
                lo = lax.broadcasted_iota(jnp.int32, vk.shape, 1) < dh
                ss_lo = jnp.sum(jnp.where(lo, sq, 0.0), axis=-1, keepdims=True)
                ss_hi = jnp.sum(jnp.where(lo, 0.0, sq), axis=-1, keepdims=True)
                r = jnp.where(lo, lax.rsqrt(ss_lo * (1.0 / dh) + EPS),
                              lax.rsqrt(ss_hi * (1.0 / dh) + EPS))
            parts.append(vk * r)
        return jnp.concatenate(parts, axis=1) * gain

    n_a = 2 * A_WIDTH // IN_TN
    n_va = 3 * A_WIDTH // IN_TN

    @pl.when((jt >= 0) & (jt < n_a))
    def _():
        normed_tile(A_HEAD_DIM)

    @pl.when((jt >= n_a) & (jt < n_va))
    def _():
        qkv_ref[...] = project().astype(BF16)

    @pl.when((jt >= n_va) & (jt < IN_QKV_TILES))
    def _():
        normed_tile(B_HEAD_DIM)

    @pl.when(jt >= IN_QKV_TILES)
    def _():
        gates_ref[...] = project().astype(BF16)

    @pl.when(j == 0)
    def _():
        x = x_ref[...]
        ms = jnp.mean(x * x, axis=-1, keepdims=True)
        h_ref[...] = (x * lax.rsqrt(ms + EPS) * g1_ref[...]).astype(BF16)
        y = _dot(h_ref[...], wkv_ref[...].astype(BF16))
        yk = y[:, :LANES]
        br = lax.broadcasted_iota(jnp.int32, (LANES, LANES), 0) // B_HEAD_DIM
        bc = lax.broadcasted_iota(jnp.int32, (LANES, LANES), 1) // B_HEAD_DIM
        ss = _dot((yk * yk).astype(BF16), jnp.where(br == bc, 1.0, 0.0).astype(BF16))
        kn = (yk * lax.rsqrt(ss * (1.0 / B_HEAD_DIM) + EPS)
              * cg_ref[IN_TILES - 1:IN_TILES, :LANES])
        vv = y[:, LANES:2 * LANES]
        lo = lax.broadcasted_iota(jnp.int32, kn.shape, 1) < B_HEAD_DIM
        for base, t in ((0, kn), (4, vv)):
            e0 = jnp.where(lo, t, 0.0)
            o1 = jnp.where(lo, 0.0, t)
            o0 = pltpu.roll(e0, B_HEAD_DIM, 1)
            e1 = pltpu.roll(o1, B_HEAD_DIM, 1)
            for k, piece in enumerate((e0, o0, e1, o1)):
                kvx_ref[:, (base + k) * LANES:(base + k + 1) * LANES] = piece.astype(BF16)


def _inproj_w_col(j):
    t = IN_TN // LANES
    jt = jnp.where(j == 0, IN_TILES - 2, j - 1)
    lane_tile = jnp.where(jt < IN_QKV_TILES, jt * t,
                          IN_GATE_COL // LANES + (jt - IN_QKV_TILES) * t)
    return lane_tile * LANES


def _inproj_col_gains(q_norm_a, k_norm_a, q_norm_b, k_norm_b):
    ones = jnp.ones((IN_TN,), F32)
    rows = (
        [jnp.tile(q_norm_a * (A_HEAD_DIM ** -0.5 * LOG2E), IN_TN // A_HEAD_DIM)]
        * (A_WIDTH // IN_TN)
        + [jnp.tile(k_norm_a, IN_TN // A_HEAD_DIM)] * (A_WIDTH // IN_TN)
        + [ones] * (A_WIDTH // IN_TN)
        + [jnp.tile(q_norm_b * (B_HEAD_DIM ** -0.5 * LOG2E), IN_TN // B_HEAD_DIM)]
        * (B_WIDTH // IN_TN)
        + [ones] * IN_GATE_TILES
        + [jnp.concatenate([jnp.tile(k_norm_b, B_KV_HEADS),
                            jnp.ones((IN_TN - B_KV_WIDTH,), F32)])])
    assert len(rows) == IN_TILES
    return jnp.stack(rows + [ones] * (-IN_TILES % SUBLANES_F32))


def _inproj(x2, g1, w, colgain):
    n = x2.shape[0]
    grid = (n // IN_TM, IN_TILES)
    return pl.pallas_call(
        _inproj_kernel,
        grid=grid,
        in_specs=[
            pl.BlockSpec((IN_TM, D_MODEL), lambda i, j: (i, 0)),
            pl.BlockSpec((1, D_MODEL), lambda i, j: (0, 0)),
            pl.BlockSpec((pl.Element(D_MODEL), pl.Element(IN_TN)),
                         lambda i, j: (0, _inproj_w_col(j))),
            pl.BlockSpec((pl.Element(D_MODEL), pl.Element(2 * B_KV_WIDTH)),
                         lambda i, j: (0, IN_KV_COL)),
            pl.BlockSpec(colgain.shape, lambda i, j: (0, 0)),
        ],
        out_specs=[
            pl.BlockSpec((IN_TM, IN_TN), lambda i, j: (i, jnp.clip(j - 1, 0, IN_QKV_TILES - 1))),
            pl.BlockSpec((IN_TM, IN_TN),
                         lambda i, j: (i, jnp.clip(j - 1 - IN_QKV_TILES, 0, IN_GATE_TILES - 1))),
            pl.BlockSpec((IN_TM, KVX_WIDTH), lambda i, j: (i, 0)),
        ],
        out_shape=[
            jax.ShapeDtypeStruct((n, IN_QKV_TILES * IN_TN), BF16),
            jax.ShapeDtypeStruct((n, IN_GATE_TILES * IN_TN), BF16),
            jax.ShapeDtypeStruct((n, KVX_WIDTH), BF16),
        ],
        scratch_shapes=[pltpu.VMEM((IN_TM, D_MODEL), BF16)],
        compiler_params=pltpu.CompilerParams(
            dimension_semantics=("arbitrary", "arbitrary"),
            vmem_limit_bytes=VMEM_LIMIT_BYTES),
        name="inproj",
    )(x2, g1, w, w, colgain)


def _moba_kernel(*refs, n_casts):
    q_ref, k_ref, v_ref, bd_ref, bc_ref = refs[:5]
    cast_src = refs[5:5 + n_casts]
    o_ref = refs[5 + n_casts]
    cast_dst = refs[6 + n_casts:6 + 2 * n_casts]
    (kmh_ref, kml_ref, vt_ref, qa_ref, s_ref, bm_ref, m_ref, l_ref,
     acc_ref) = refs[6 + 2 * n_casts:]

    t = pl.program_id(2)
    L = MOBA_BLOCK
    P = MOBA_PAIR
    Dh = A_HEAD_DIM
    NH = MOBA_HEADS_PER_STEP
    seq = k_ref.shape[1]
    nb = seq // L

    @pl.when(t == 0)
    def _():
        r = lax.broadcasted_iota(jnp.int32, (nb, seq), 0)
        c = lax.broadcasted_iota(jnp.int32, (nb, seq), 1)
        ind = jnp.where(lax.shift_right_logical(c, int(math.log2(L))) == r, 1.0, 0.0).astype(BF16)
        er = lax.broadcasted_iota(jnp.int32, (Dh, Dh), 0)
        ec = lax.broadcasted_iota(jnp.int32, (Dh, Dh), 1)
        eye = jnp.where(er == ec, 1.0, 0.0).astype(BF16)
        for hh in range(NH):
            km = _dot(ind, k_ref[0, :, hh * Dh:(hh + 1) * Dh]) * (1.0 / L)
            hi = km.astype(BF16)
            kmh_ref[hh] = hi
            kml_ref[hh] = (km - hi.astype(F32)).astype(BF16)
            for g in range(seq // P):
                vt_ref[hh, g] = _dot_nt(
                    eye, v_ref[0, g * P:(g + 1) * P, hh * Dh:(hh + 1) * Dh]).astype(BF16)

    lane = lax.broadcasted_iota(jnp.int32, (P, LANES), 1)
    row = lax.broadcasted_iota(jnp.int32, (P, LANES), 0)
    upper = (row >= L).astype(jnp.int32)

    def select(hh):
        q = q_ref[0, :, hh * Dh:(hh + 1) * Dh]
        g = _dot_nt(kmh_ref[hh], q) + _dot_nt(kml_ref[hh], q)
        blk = lax.broadcasted_iota(jnp.int32, g.shape, 0)
        blk_f = blk.astype(F32)
        qcol = lax.broadcasted_iota(jnp.int32, g.shape, 1)
        own_blk = 2 * t + (qcol >= L).astype(jnp.int32)
        past = blk < own_blk
        g = jnp.where(past, g, NEG)
        sel = jnp.full(g.shape, NEG, F32)
        for _ in range(MOBA_TOPK):
            mx = jnp.max(g, axis=0, keepdims=True)
            first = jnp.min(jnp.where(g == mx, blk_f, float(LANES)), axis=0, keepdims=True)
            hit = blk_f == first
            sel = jnp.where(hit & past, 0.0, sel)
            g = jnp.where(hit, -jnp.inf, g)
        sel = jnp.where(blk == own_blk, 0.0, sel)
        sel = jnp.concatenate([sel, jnp.zeros((LANES - nb, P), F32)], axis=0)
        qa_ref[hh, :, :Dh] = q
        qa_ref[hh, :, Dh:] = sel.T.astype(BF16)

    def scores(hh, pair):
        r0 = pl.multiple_of(pair * P, P)
        ind = jnp.where(lane == 2 * pair + upper, 1.0, 0.0).astype(BF16)
        k_aug = jnp.concatenate([k_ref[0, pl.ds(r0, P), hh * Dh:(hh + 1) * Dh], ind], axis=1)
        return _dot_nt(k_aug, qa_ref[hh])

    def accumulate(hh, pair, buf):
        m_old = m_ref[hh]
        m_new = jnp.maximum(m_old, bm_ref[buf, hh])
        alpha = jnp.exp2(m_old - m_new)
        p = jnp.exp2(s_ref[buf, hh] - m_new)
        l_ref[hh] = alpha * l_ref[hh] + jnp.sum(p, axis=0, keepdims=True)
        acc_ref[hh] = alpha * acc_ref[hh] + _dot(vt_ref[hh, pair], p.astype(BF16))
        m_ref[hh] = m_new

    def phase(pair, buf, fetch=True):
        if fetch:
            for hh in range(NH):
                sc = scores(hh, jnp.maximum(pair - 1, 0))
                s_ref[1 - buf, hh] = sc
                bm_ref[1 - buf, hh] = jnp.max(sc, axis=0, keepdims=True)
        for hh in range(NH):
            accumulate(hh, pair, buf)

    for hh in range(NH):
        select(hh)
    for src, dst in zip(cast_src, cast_dst):
        dst[...] = src[...].astype(BF16)
    for hh in range(NH):
        s_ref[1, hh] = scores(hh, t)
    for hh in range(NH):
        s_ref[0, hh] = scores(hh, jnp.maximum(t - 1, 0))
        s_ref[0, hh, P - LANES:, :LANES] += bc_ref[hh]
        bm_ref[0, hh] = jnp.max(s_ref[0, hh], axis=0, keepdims=True)
    for hh in range(NH):
        s = s_ref[1, hh] + bd_ref[hh]
        m0 = jnp.max(s, axis=0, keepdims=True)
        p = jnp.exp2(s - m0)
        m_ref[hh] = m0
        l_ref[hh] = jnp.sum(p, axis=0, keepdims=True)
        acc_ref[hh] = _dot(vt_ref[hh, t], p.astype(BF16))

    def two_phases(i, carry):
        pair = t - 1 - 2 * i
        phase(pair, 0)
        phase(pair - 1, 1)
        return carry

    lax.fori_loop(0, t // 2, two_phases, 0)

    @pl.when(t % 2 == 1)
    def _():
        phase(0, 0, fetch=False)

    for hh in range(NH):
        o_ref[0, :, hh * Dh:(hh + 1) * Dh] = (acc_ref[hh] / l_ref[hh]).T.astype(BF16)


def _cast_slab_grid(rows, cols, n_steps):
    nc = 1
    while nc <= n_steps:
        nr = n_steps // nc
        if rows % (nr * SUBLANES_BF16) == 0 and cols % (nc * LANES) == 0:
            return nr, nc
        nc *= 2
    raise ValueError(f"no slab split of {(rows, cols)} for {n_steps} steps")


def _moba(qkv3, bias_a, weights, layer):
    b, s, _ = qkv3.shape
    P = MOBA_PAIR
    Dh = A_HEAD_DIM
    NH = MOBA_HEADS_PER_STEP
    G = A_HEADS // NH
    T = s // P
    nb = s // MOBA_BLOCK
    assert nb % SUBLANES_BF16 == 0 and nb <= LANES
    n_steps = b * G * T
    src_specs, dst_specs, dst_shapes = [], [], []
    for w in weights:
        _, r, c = w.shape
        nr, nc = _cast_slab_grid(r, c, n_steps)

        def slab(bi, h, t, nc=nc):
            step = (bi * G + h) * T + t
            return step // nc, step % nc

        src_specs.append(pl.BlockSpec((None, r // nr, c // nc),
                                      lambda bi, h, t, slab=slab: (layer, *slab(bi, h, t))))
        dst_specs.append(pl.BlockSpec((r // nr, c // nc), slab))
        dst_shapes.append(jax.ShapeDtypeStruct((r, c), BF16))
    outs = pl.pallas_call(
        functools.partial(_moba_kernel, n_casts=len(weights)),
        grid=(b, G, T),
        in_specs=[
            pl.BlockSpec((1, P, NH * Dh), lambda bi, h, t: (bi, t, h)),
            pl.BlockSpec((1, s, NH * Dh), lambda bi, h, t: (bi, 0, G + h)),
            pl.BlockSpec((1, s, NH * Dh), lambda bi, h, t: (bi, 0, 2 * G + h)),
            pl.BlockSpec((NH, P, P), lambda bi, h, t: (h, 0, 0)),
            pl.BlockSpec((NH, LANES, LANES), lambda bi, h, t: (h, 1, 2)),
        ] + src_specs,
        out_specs=[pl.BlockSpec((1, P, NH * Dh), lambda bi, h, t: (bi, t, h))] + dst_specs,
        out_shape=[jax.ShapeDtypeStruct((b, s, A_WIDTH), BF16)] + dst_shapes,
        scratch_shapes=[
            pltpu.VMEM((NH, nb, Dh), BF16),
            pltpu.VMEM((NH, nb, Dh), BF16),
            pltpu.VMEM((NH, s // P, Dh, P), BF16),
            pltpu.VMEM((NH, P, Dh + LANES), BF16),
            pltpu.VMEM((2, NH, P, P), F32),
            pltpu.VMEM((2, NH, 1, P), F32),
            pltpu.VMEM((NH, 1, P), F32),
            pltpu.VMEM((NH, 1, P), F32),
            pltpu.VMEM((NH, Dh, P), F32),
        ],
        compiler_params=pltpu.CompilerParams(
            dimension_semantics=("arbitrary", "arbitrary", "arbitrary"),
            vmem_limit_bytes=VMEM_LIMIT_BYTES),
        name="moba",
    )(qkv3, qkv3, qkv3, bias_a, bias_a, *weights)
    return outs[0], outs[1:]


def _swa_kernel(q_ref, kvx_ref, bias_ref, sink_ref, o_ref, vt_ref):
    t = pl.program_id(1)
    W = WINDOW
    n_win = q_ref.shape[1] // W
    seq = kvx_ref.shape[1]
    pairs_per_kv = B_Q_HEADS // B_KV_HEADS // 2
    n_combo = 2 * B_KV_HEADS
    chunk = 4 * W

    @pl.when(t == 0)
    def _():
        er = lax.broadcasted_iota(jnp.int32, (LANES, LANES), 0)
        ec = lax.broadcasted_iota(jnp.int32, (LANES, LANES), 1)
        eye = jnp.where(er == ec, 1.0, 0.0).astype(BF16)
        for i in range(n_combo):
            col = (n_combo + i) * LANES
            for c in range(seq // chunk):
                vt = _dot_nt(eye, kvx_ref[0, c * chunk:(c + 1) * chunk, col:col + LANES])
                for k in range(chunk // W):
                    vt_ref[i, c * (chunk // W) + k] = vt[:, k * W:(k + 1) * W].astype(BF16)

    def band(w):
        n = t * n_win + w
        return jnp.maximum(n - 1, 0), (n == 0).astype(jnp.int32)

    def window_scores(w):
        sb, first = band(w)
        start = pl.multiple_of(sb * W, W)
        scores = []
        for kv in range(B_KV_HEADS):
            width = pairs_per_kv * LANES
            qg = q_ref[0, w * W:(w + 1) * W, kv * width:(kv + 1) * width]
            qs = jnp.concatenate(
                [qg[:, k * LANES:(k + 1) * LANES] for k in range(pairs_per_kv)], axis=0)
            for par in range(2):
                kcol = (2 * kv + par) * LANES
                kk = kvx_ref[0, pl.ds(start, 2 * W), kcol:kcol + LANES]
                scores.append(_dot_nt(kk, qs) + bias_ref[first, kv, par])
        return scores

    def window_output(w, scores):
        sb, _ = band(w)
        outs = []
        for i, s in enumerate(scores):
            sink = sink_ref[pl.ds(i, 1), :]
            m = jnp.maximum(jnp.max(s, axis=0, keepdims=True), sink)
            pe = jnp.exp2(s - m)
            l = jnp.sum(pe, axis=0, keepdims=True) + jnp.exp2(sink - m)
            vt = jnp.concatenate([vt_ref[i, sb], vt_ref[i, sb + 1]], axis=1)
            outs.append(_dot(vt, pe.astype(BF16)) / l)
        for kv in range(B_KV_HEADS):
            ot = outs[2 * kv] + outs[2 * kv + 1]
            for k in range(pairs_per_kv):
                p = kv * pairs_per_kv + k
                o_ref[0, w * W:(w + 1) * W, p * LANES:(p + 1) * LANES] = (
                    ot[:, k * W:(k + 1) * W].T.astype(BF16))

    scores = window_scores(0)
    for w in range(n_win):
        nxt = window_scores(w + 1) if w + 1 < n_win else None
        window_output(w, scores)
        scores = nxt


def _swa(qkv3, kvx3, bias_b, sink_rows):
    b, s, _ = qkv3.shape
    qb_blk = 3 * A_WIDTH // B_WIDTH
    return pl.pallas_call(
        _swa_kernel,
        grid=(b, s // SWA_TQ),
        in_specs=[
            pl.BlockSpec((1, SWA_TQ, B_WIDTH), lambda bi, t: (bi, t, qb_blk)),
            pl.BlockSpec((1, s, KVX_WIDTH), lambda bi, t: (bi, 0, 0)),
            pl.BlockSpec(bias_b.shape, lambda bi, t: (0, 0, 0, 0, 0)),
            pl.BlockSpec(sink_rows.shape, lambda bi, t: (0, 0)),
        ],
        out_specs=pl.BlockSpec((1, SWA_TQ, B_WIDTH), lambda bi, t: (bi, t, 0)),
        out_shape=jax.ShapeDtypeStruct((b, s, B_WIDTH), BF16),
        scratch_shapes=[pltpu.VMEM((2 * B_KV_HEADS, s // WINDOW, LANES, WINDOW), BF16)],
        compiler_params=pltpu.CompilerParams(
            dimension_semantics=("arbitrary", "arbitrary"),
            vmem_limit_bytes=VMEM_LIMIT_BYTES),
        name="swa",
    )(qkv3, kvx3, bias_b, sink_rows)


def _merge_kernel(ya_ref, yb_ref, gt_ref, x_ref, wa_ref, wb_ref, wo_ref, o_ref, mg_ref):
    ch = MERGE_CH
    for c in range(D_MODEL // ch):
        cols = slice(c * ch, (c + 1) * ch)
        ta = _dot(ya_ref[...], wa_ref[:, cols])
        tb = _dot(yb_ref[...], wb_ref[:, cols])
        ga = _sigmoid(gt_ref[:, cols].astype(F32))
        gb = _sigmoid(gt_ref[:, D_MODEL + c * ch:D_MODEL + (c + 1) * ch].astype(F32))
        mg_ref[:, cols] = (ga * ta + gb * tb).astype(BF16)
    for c in range(D_MODEL // ch):
        cols = slice(c * ch, (c + 1) * ch)
        o_ref[:, cols] = x_ref[:, cols] + _dot(mg_ref[...], wo_ref[:, cols])


def _merge(ya, yb, gates, x2, wa, wb, wo):
    n = x2.shape[0]
    tm = MERGE_TM
    resident = functools.partial(pl.BlockSpec, pipeline_mode=pl.Buffered(1))
    return pl.pallas_call(
        _merge_kernel,
        grid=(n // tm,),
        in_specs=[
            pl.BlockSpec((tm, A_WIDTH), lambda i: (i, 0)),
            pl.BlockSpec((tm, B_WIDTH), lambda i: (i, 0)),
            pl.BlockSpec((tm, 2 * D_MODEL), lambda i: (i, 0)),
            pl.BlockSpec((tm, D_MODEL), lambda i: (i, 0)),
            resident((A_WIDTH, D_MODEL), lambda i: (0, 0)),
            resident((B_WIDTH, D_MODEL), lambda i: (0, 0)),
            resident((D_MODEL, D_MODEL), lambda i: (0, 0)),
        ],
        out_specs=pl.BlockSpec((tm, D_MODEL), lambda i: (i, 0)),
        out_shape=jax.ShapeDtypeStruct((n, D_MODEL), F32),
        scratch_shapes=[pltpu.VMEM((tm, D_MODEL), BF16)],
        compiler_params=pltpu.CompilerParams(
            dimension_semantics=("arbitrary",),
            vmem_limit_bytes=VMEM_LIMIT_BYTES),
        name="merge",
    )(ya, yb, gates, x2, wa, wb, wo)


def _ffn_kernel(x_ref, g2_ref, wg_ref, wu_ref, wd_ref, o_ref, h_ref):
    f = pl.program_id(1)

    @pl.when(f == 0)
    def _():
        x = x_ref[...]
        ms = jnp.mean(x * x, axis=-1, keepdims=True)
        h_ref[...] = (x * lax.rsqrt(ms + EPS) * g2_ref[...]).astype(BF16)
        o_ref[...] = x

    h = h_ref[...]
    g = _dot(h, wg_ref[...])
    u = _dot(h, wu_ref[...])
    a = (g * _sigmoid(g) * u).astype(BF16)
    o_ref[...] += _dot(a, wd_ref[...])


def _ffn(x1, g2, wgu, wd):
    n = x1.shape[0]
    tm, tf = FFN_TM, FFN_TF
    nf = D_FF // tf
    return pl.pallas_call(
        _ffn_kernel,
        grid=(n // tm, nf),
        in_specs=[
            pl.BlockSpec((tm, D_MODEL), lambda i, f: (i, 0)),
            pl.BlockSpec((1, D_MODEL), lambda i, f: (0, 0)),
            pl.BlockSpec((D_MODEL, tf), lambda i, f: (0, f)),
            pl.BlockSpec((D_MODEL, tf), lambda i, f: (0, nf + f)),
            pl.BlockSpec((tf, D_MODEL), lambda i, f: (f, 0)),
        ],
        out_specs=pl.BlockSpec((tm, D_MODEL), lambda i, f: (i, 0)),
        out_shape=jax.ShapeDtypeStruct((n, D_MODEL), F32),
        scratch_shapes=[pltpu.VMEM((tm, D_MODEL), BF16)],
        compiler_params=pltpu.CompilerParams(
            dimension_semantics=("arbitrary", "arbitrary"),
            vmem_limit_bytes=VMEM_LIMIT_BYTES),
        name="ffn",
    )(x1, g2, wgu, wgu, wd)


def _moba_bucket_tables():
    L = MOBA_BLOCK
    kj = np.arange(L)[:, None]
    qi = np.arange(L)[None, :]
    return np.stack([_t5_bucket_np(qi - kj), _t5_bucket_np(L + qi - kj)])


def _swa_bias_tables():
    W = WINDOW
    qi = np.arange(W)[None, :]
    c = np.arange(2 * W)[:, None]
    d = W + qi - c
    v = (d >= 0) & (d < W)
    d0 = qi - c
    v0 = (d0 >= 0) & (d0 < W)
    buckets = np.stack([_t5_bucket_np(d), _t5_bucket_np(d0)])
    valid = np.stack([v, v0]).astype(np.int32)
    return buckets, valid


def kernel(x, norm1_g, w_in, q_norm_a, k_norm_a, q_norm_b, k_norm_b, rel_bias, sinks,
           w_branch_a, w_branch_b, w_out, norm2_g, w_gate_up, w_down):
    b, s, d = x.shape
    depth = w_in.shape[0]
    assert d == D_MODEL and s % MOBA_PAIR == 0 and s % SWA_TQ == 0
    assert (b * s) % IN_TM == 0 and (b * s) % MERGE_TM == 0 and (b * s) % FFN_TM == 0
    far = _t5_bucket_np(np.arange(LANES + 1, s + MOBA_PAIR))
    assert (far == far[0]).all()
    far_bucket = int(far[0])

    bias_a = _moba_bias(rel_bias, jnp.asarray(_moba_bucket_tables()), far_bucket)
    bk_b, va_b = _swa_bias_tables()
    bias_b = _swa_bias(rel_bias, jnp.asarray(bk_b), jnp.asarray(va_b))
    x2 = x.reshape(b * s, d)
    for l in range(depth):
        colgain = _inproj_col_gains(q_norm_a[l], k_norm_a[l], q_norm_b[l], k_norm_b[l])
        qkv, gates, kvx = _inproj(x2, norm1_g[l][None, :], w_in[l], colgain)
        qkv3 = qkv.reshape(b, s, -1)
        ya, (wa, wb, wo, wgu, wd) = _moba(
            qkv3, bias_a, [w_branch_a, w_branch_b, w_out, w_gate_up, w_down], l)
        per_kv = B_Q_HEADS // B_KV_HEADS
        sink_rows = jnp.repeat(
            sinks[l].reshape(B_KV_HEADS, per_kv // 2, 2).transpose(0, 2, 1).reshape(2 * B_KV_HEADS, -1),
            WINDOW, axis=1) * LOG2E
        sink_rows = jnp.pad(sink_rows, ((0, -2 * B_KV_HEADS % SUBLANES_F32), (0, 0)))
        yb = _swa(qkv3, kvx.reshape(b, s, -1), bias_b, sink_rows)
        x1 = _merge(ya.reshape(b * s, -1), yb.reshape(b * s, -1), gates, x2, wa, wb, wo)
        x2 = _ffn(x1, norm2_g[l][None, :], wgu, wd)
    return x2.reshape(b, s, d)
```

```python
import functools
import math

import jax
import jax.numpy as jnp
import numpy as np
from jax import lax
from jax.experimental import pallas as pl
from jax.experimental.pallas import tpu as pltpu

F32 = jnp.float32
BF16 = jnp.bfloat16

D_MODEL = 2048
A_HEADS = 8
A_HEAD_DIM = 128
A_WIDTH = A_HEADS * A_HEAD_DIM
MOBA_BLOCK = 256
MOBA_TOPK = 3
B_Q_HEADS = 16
B_KV_HEADS = 2
B_HEAD_DIM = 64
B_WIDTH = B_Q_HEADS * B_HEAD_DIM
B_KV_WIDTH = B_KV_HEADS * B_HEAD_DIM
WINDOW = 128
NUM_BUCKETS = 32
MAX_DISTANCE = 128
MAX_EXACT = NUM_BUCKETS // 2
D_FF = -(-(8 * D_MODEL) // (3 * 256)) * 256
IN_WIDTH = 3 * A_WIDTH + B_WIDTH + 2 * B_KV_WIDTH + 2 * D_MODEL
EPS = 1e-6
NEG = -1e30
LOG2E = math.log2(math.e)

LANES = 128
SUBLANES_F32 = 8
SUBLANES_BF16 = 16
VMEM_LIMIT_BYTES = 56 * 1024 * 1024

IN_TN = 1024
IN_QKV_TILES = (3 * A_WIDTH + B_WIDTH) // IN_TN
IN_GATE_TILES = (2 * D_MODEL) // IN_TN
IN_TILES = IN_QKV_TILES + IN_GATE_TILES + 1
IN_KV_COL = 3 * A_WIDTH + B_WIDTH
IN_GATE_COL = IN_KV_COL + 2 * B_KV_WIDTH
IN_TM = 1024
KVX_WIDTH = 8 * LANES

MOBA_PAIR = 2 * MOBA_BLOCK
MOBA_HEADS_PER_STEP = 4
SWA_TQ = 512
MERGE_TM = 512
MERGE_CH = 512
FFN_TM = 1024
FFN_TF = 512


def _dot(a, b):
    return jnp.dot(a, b, preferred_element_type=F32)


def _dot_nt(a, b):
    return lax.dot_general(a, b, (((1,), (1,)), ((), ())), preferred_element_type=F32)


def _sigmoid(v):
    return 0.5 * jnp.tanh(0.5 * v) + 0.5


def _t5_bucket_np(dist):
    n = np.maximum(dist, 0)
    nf = np.maximum(n, 1).astype(np.float32)
    large = MAX_EXACT + (np.log(nf / np.float32(MAX_EXACT))
                         / np.float32(math.log(MAX_DISTANCE / MAX_EXACT))
                         * np.float32(NUM_BUCKETS - MAX_EXACT)).astype(np.int32)
    large = np.minimum(large, NUM_BUCKETS - 1)
    return np.where(n < MAX_EXACT, n, large).astype(np.int32)


def _bucket_lookup(tab_ref, bm, h):
    acc = jnp.zeros(bm.shape, F32)
    for b in range(NUM_BUCKETS):
        acc = jnp.where(bm == b, tab_ref[b, h], acc)
    return acc


def _swa_bias_kernel(tab_ref, bucket_ref, valid_ref, o_ref):
    h = pl.program_id(0) + A_HEADS
    for t in range(bucket_ref.shape[0]):
        o_ref[t, 0, 0] = jnp.where(valid_ref[t] != 0,
                                   _bucket_lookup(tab_ref, bucket_ref[t], h) * LOG2E, NEG)


def _swa_bias(rel_bias, buckets, valid):
    n_tiles, r, c = buckets.shape
    per_kv = B_Q_HEADS // B_KV_HEADS
    return pl.pallas_call(
        _swa_bias_kernel,
        grid=(B_Q_HEADS,),
        in_specs=[
            pl.BlockSpec(memory_space=pltpu.SMEM),
            pl.BlockSpec((n_tiles, r, c), lambda h: (0, 0, 0)),
            pl.BlockSpec((n_tiles, r, c), lambda h: (0, 0, 0)),
        ],
        out_specs=pl.BlockSpec(
            (n_tiles, 1, 1, r, c),
            lambda h: (0, h // per_kv, h % 2, 0, (h % per_kv) // 2)),
        out_shape=jax.ShapeDtypeStruct((n_tiles, B_KV_HEADS, 2, r, c * per_kv // 2), F32),
        name="swa_bias",
    )(rel_bias, buckets, valid)


def _moba_bias_kernel(tab_ref, bucket_ref, o_ref, *, shift_bucket):
    h = pl.program_id(0)
    L = MOBA_BLOCK
    shift = tab_ref[shift_bucket, h]
    tiles = [(_bucket_lookup(tab_ref, bucket_ref[t], h) - shift) * LOG2E for t in range(2)]
    kk = lax.broadcasted_iota(jnp.int32, (L, L), 0)
    qq = lax.broadcasted_iota(jnp.int32, (L, L), 1)
    own = jnp.where(qq >= kk, tiles[0], NEG)
    o_ref[0, :L, :L] = own
    o_ref[0, L:, L:] = own
    o_ref[0, :L, L:] = tiles[1]
    o_ref[0, L:, :L] = jnp.full((L, L), NEG, F32)


def _moba_bias(rel_bias, buckets, shift_bucket):
    L = MOBA_BLOCK
    return pl.pallas_call(
        functools.partial(_moba_bias_kernel, shift_bucket=shift_bucket),
        grid=(A_HEADS,),
        in_specs=[
            pl.BlockSpec(memory_space=pltpu.SMEM),
            pl.BlockSpec((2, L, L), lambda h: (0, 0, 0)),
        ],
        out_specs=pl.BlockSpec((1, 2 * L, 2 * L), lambda h: (h, 0, 0)),
        out_shape=jax.ShapeDtypeStruct((A_HEADS, 2 * L, 2 * L), F32),
        name="moba_bias",
    )(rel_bias, buckets)


def _inproj_kernel(x_ref, g1_ref, w_ref, wkv_ref, cg_ref,
                   qkv_ref, gates_ref, kvx_ref, h_ref):
    j = pl.program_id(1)
    jt = j - 1
    cg = cg_ref[pl.ds(jnp.maximum(jt, 0), 1), :]

    def project(cols=slice(None), rows=slice(None)):
        return _dot(h_ref[rows, :], w_ref[:, cols].astype(BF16))

    def normed_tile(dh):
        n_chunks = 4
        rc = IN_TM // n_chunks
        for r in range(n_chunks):
            rows = slice(r * rc, (r + 1) * rc)
            qkv_ref[rows, :] = head_norm(project(rows=rows), dh, cg).astype(BF16)

    def head_norm(v, dh, gain):
        parts = []
        for k in range(v.shape[1] // LANES):
            vk = v[:, k * LANES:(k + 1) * LANES]
            sq = vk * vk
            if dh == LANES:
                r = lax.rsqrt(jnp.sum(sq, axis=-1, keepdims=True) * (1.0 / dh) + EPS)
            else:
                lo = lax.broadcasted_iota(jnp.int32, vk.shape, 1) < dh
                ss_lo = jnp.sum(jnp.where(lo, sq, 0.0), axis=-1, keepdims=True)
                ss_hi = jnp.sum(jnp.where(lo, 0.0, sq), axis=-1, keepdims=True)
                r = jnp.where(lo, lax.rsqrt(ss_lo * (1.0 / dh) + EPS),
                              lax.rsqrt(ss_hi * (1.0 / dh) + EPS))
            parts.append(vk * r)
        return jnp.concatenate(parts, axis=1) * gain

    n_a = 2 * A_WIDTH // IN_TN
    n_va = 3 * A_WIDTH // IN_TN

    @pl.when((jt >= 0) & (jt < n_a))
    def _():
        normed_tile(A_HEAD_DIM)

    @pl.when((jt >= n_a) & (jt < n_va))
    def _():
        qkv_ref[...] = project().astype(BF16)

    @pl.when((jt >= n_va) & (jt < IN_QKV_TILES))
    def _():
        normed_tile(B_HEAD_DIM)

    @pl.when(jt >= IN_QKV_TILES)
    def _():
        gates_ref[...] = project().astype(BF16)

    @pl.when(j == 0)
    def _():
        x = x_ref[...]
        ms = jnp.mean(x * x, axis=-1, keepdims=True)
        h_ref[...] = (x * lax.rsqrt(ms + EPS) * g1_ref[...]).astype(BF16)
        y = _dot(h_ref[...], wkv_ref[...].astype(BF16))
        yk = y[:, :LANES]
        br = lax.broadcasted_iota(jnp.int32, (LANES, LANES), 0) // B_HEAD_DIM
        bc = lax.broadcasted_iota(jnp.int32, (LANES, LANES), 1) // B_HEAD_DIM
        ss = _dot((yk * yk).astype(BF16), jnp.where(br == bc, 1.0, 0.0).astype(BF16))
        kn = (yk * lax.rsqrt(ss * (1.0 / B_HEAD_DIM) + EPS)
              * cg_ref[IN_TILES - 1:IN_TILES, :LANES])
        vv = y[:, LANES:2 * LANES]
        lo = lax.broadcasted_iota(jnp.int32, kn.shape, 1) < B_HEAD_DIM
        for base, t in ((0, kn), (4, vv)):
            e0 = jnp.where(lo, t, 0.0)
            o1 = jnp.where(lo, 0.0, t)
            o0 = pltpu.roll(e0, B_HEAD_DIM, 1)
            e1 = pltpu.roll(o1, B_HEAD_DIM, 1)
            for k, piece in enumerate((e0, o0, e1, o1)):
                kvx_ref[:, (base + k) * LANES:(base + k + 1) * LANES] = piece.astype(BF16)


def _inproj_w_col(j):
    t = IN_TN // LANES
    jt = jnp.where(j == 0, IN_TILES - 2, j - 1)
    lane_tile = jnp.where(jt < IN_QKV_TILES, jt * t,
                          IN_GATE_COL // LANES + (jt - IN_QKV_TILES) * t)
    return lane_tile * LANES


def _inproj_col_gains(q_norm_a, k_norm_a, q_norm_b, k_norm_b):
    ones = jnp.ones((IN_TN,), F32)
    rows = (
        [jnp.tile(q_norm_a * (A_HEAD_DIM ** -0.5 * LOG2E), IN_TN // A_HEAD_DIM)]
        * (A_WIDTH // IN_TN)
        + [jnp.tile(k_norm_a, IN_TN // A_HEAD_DIM)] * (A_WIDTH // IN_TN)
        + [ones] * (A_WIDTH // IN_TN)
        + [jnp.tile(q_norm_b * (B_HEAD_DIM ** -0.5 * LOG2E), IN_TN // B_HEAD_DIM)]
        * (B_WIDTH // IN_TN)
        + [ones] * IN_GATE_TILES
        + [jnp.concatenate([jnp.tile(k_norm_b, B_KV_HEADS),
                            jnp.ones((IN_TN - B_KV_WIDTH,), F32)])])
    assert len(rows) == IN_TILES
    return jnp.stack(rows + [ones] * (-IN_TILES % SUBLANES_F32))


def _inproj(x2, g1, w, colgain):
    n = x2.shape[0]
    grid = (n // IN_TM, IN_TILES)
    return pl.pallas_call(
        _inproj_kernel,
        grid=grid,
        in_specs=[
            pl.BlockSpec((IN_TM, D_MODEL), lambda i, j: (i, 0)),
            pl.BlockSpec((1, D_MODEL), lambda i, j: (0, 0)),
            pl.BlockSpec((pl.Element(D_MODEL), pl.Element(IN_TN)),
                         lambda i, j: (0, _inproj_w_col(j))),
            pl.BlockSpec((pl.Element(D_MODEL), pl.Element(2 * B_KV_WIDTH)),
                         lambda i, j: (0, IN_KV_COL)),
            pl.BlockSpec(colgain.shape, lambda i, j: (0, 0)),
        ],
        out_specs=[
            pl.BlockSpec((IN_TM, IN_TN), lambda i, j: (i, jnp.clip(j - 1, 0, IN_QKV_TILES - 1))),
            pl.BlockSpec((IN_TM, IN_TN),
                         lambda i, j: (i, jnp.clip(j - 1 - IN_QKV_TILES, 0, IN_GATE_TILES - 1))),
            pl.BlockSpec((IN_TM, KVX_WIDTH), lambda i, j: (i, 0)),
        ],
        out_shape=[
            jax.ShapeDtypeStruct((n, IN_QKV_TILES * IN_TN), BF16),
            jax.ShapeDtypeStruct((n, IN_GATE_TILES * IN_TN), BF16),
            jax.ShapeDtypeStruct((n, KVX_WIDTH), BF16),
        ],
        scratch_shapes=[pltpu.VMEM((IN_TM, D_MODEL), BF16)],
        compiler_params=pltpu.CompilerParams(
            dimension_semantics=("arbitrary", "arbitrary"),
            vmem_limit_bytes=VMEM_LIMIT_BYTES),
        name="inproj",
    )(x2, g1, w, w, colgain)


def _moba_kernel(*refs, n_casts):
    q_ref, k_ref, v_ref, bd_ref, bc_ref = refs[:5]
    cast_src = refs[5:5 + n_casts]
    o_ref = refs[5 + n_casts]
    cast_dst = refs[6 + n_casts:6 + 2 * n_casts]
    (kmh_ref, kml_ref, vt_ref, qa_ref, s_ref, bm_ref, m_ref, l_ref,
     acc_ref) = refs[6 + 2 * n_casts:]

    t = pl.program_id(2)
    L = MOBA_BLOCK
    P = MOBA_PAIR
    Dh = A_HEAD_DIM
    NH = MOBA_HEADS_PER_STEP
    seq = k_ref.shape[1]
    nb = seq // L

    @pl.when(t == 0)
    def _():
        r = lax.broadcasted_iota(jnp.int32, (nb, seq), 0)
        c = lax.broadcasted_iota(jnp.int32, (nb, seq), 1)
        ind = jnp.where(lax.shift_right_logical(c, int(math.log2(L))) == r, 1.0, 0.0).astype(BF16)
        er = lax.broadcasted_iota(jnp.int32, (Dh, Dh), 0)
        ec = lax.broadcasted_iota(jnp.int32, (Dh, Dh), 1)
        eye = jnp.where(er == ec, 1.0, 0.0).astype(BF16)
        for hh in range(NH):
            km = _dot(ind, k_ref[0, :, hh * Dh:(hh + 1) * Dh]) * (1.0 / L)
            hi = km.astype(BF16)
            kmh_ref[hh] = hi
            kml_ref[hh] = (km - hi.astype(F32)).astype(BF16)
            for g in range(seq // P):
                vt_ref[hh, g] = _dot_nt(
                    eye, v_ref[0, g * P:(g + 1) * P, hh * Dh:(hh + 1) * Dh]).astype(BF16)

    lane = lax.broadcasted_iota(jnp.int32, (P, LANES), 1)
    row = lax.broadcasted_iota(jnp.int32, (P, LANES), 0)
    upper = (row >= L).astype(jnp.int32)

    def select(hh):
        q = q_ref[0, :, hh * Dh:(hh + 1) * Dh]
        g = _dot_nt(kmh_ref[hh], q) + _dot_nt(kml_ref[hh], q)
        blk = lax.broadcasted_iota(jnp.int32, g.shape, 0)
        blk_f = blk.astype(F32)
        qcol = lax.broadcasted_iota(jnp.int32, g.shape, 1)
        own_blk = 2 * t + (qcol >= L).astype(jnp.int32)
        past = blk < own_blk
        g = jnp.where(past, g, NEG)
        sel = jnp.full(g.shape, NEG, F32)
        for _ in range(MOBA_TOPK):
            mx = jnp.max(g, axis=0, keepdims=True)
            first = jnp.min(jnp.where(g == mx, blk_f, float(LANES)), axis=0, keepdims=True)
            hit = blk_f == first
            sel = jnp.where(hit & past, 0.0, sel)
            g = jnp.where(hit, -jnp.inf, g)
        sel = jnp.where(blk == own_blk, 0.0, sel)
        sel = jnp.concatenate([sel, jnp.zeros((LANES - nb, P), F32)], axis=0)
        qa_ref[hh, :, :Dh] = q
        qa_ref[hh, :, Dh:] = sel.T.astype(BF16)

    def scores(hh, pair):
        r0 = pl.multiple_of(pair * P, P)
        ind = jnp.where(lane == 2 * pair + upper, 1.0, 0.0).astype(BF16)
        k_aug = jnp.concatenate([k_ref[0, pl.ds(r0, P), hh * Dh:(hh + 1) * Dh], ind], axis=1)
        return _dot_nt(k_aug, qa_ref[hh])

    def accumulate(hh, pair, buf):
        m_old = m_ref[hh]
        m_new = jnp.maximum(m_old, bm_ref[buf, hh])
        alpha = jnp.exp2(m_old - m_new)
        p = jnp.exp2(s_ref[buf, hh] - m_new)
        l_ref[hh] = alpha * l_ref[hh] + jnp.sum(p, axis=0, keepdims=True)
        acc_ref[hh] = alpha * acc_ref[hh] + _dot(vt_ref[hh, pair], p.astype(BF16))
        m_ref[hh] = m_new

    def phase(pair, buf, fetch=True):
        if fetch:
            for hh in range(NH):
                sc = scores(hh, jnp.maximum(pair - 1, 0))
                s_ref[1 - buf, hh] = sc
                bm_ref[1 - buf, hh] = jnp.max(sc, axis=0, keepdims=True)
        for hh in range(NH):
            accumulate(hh, pair, buf)

    for hh in range(NH):
        select(hh)
    for src, dst in zip(cast_src, cast_dst):
        dst[...] = src[...].astype(BF16)
    for hh in range(NH):
        sd = scores(hh, t) + bd_ref[hh]
        s_ref[1, hh] = sd
        bm_ref[1, hh] = jnp.max(sd, axis=0, keepdims=True)
    for hh in range(NH):
        sc = scores(hh, jnp.maximum(t - 1, 0))
        corner = sc[P - LANES:, :LANES] + bc_ref[hh]
        s_ref[0, hh] = sc
        s_ref[0, hh, P - LANES:, :LANES] = corner
        bm = jnp.max(sc, axis=0, keepdims=True)
        cm = jnp.maximum(bm[:, :LANES], jnp.max(corner, axis=0, keepdims=True))
        bm_ref[0, hh] = jnp.concatenate([cm, bm[:, LANES:]], axis=1)
    for hh in range(NH):
        m0 = bm_ref[1, hh]
        p = jnp.exp2(s_ref[1, hh] - m0)
        m_ref[hh] = m0
        l_ref[hh] = jnp.sum(p, axis=0, keepdims=True)
        acc_ref[hh] = _dot(vt_ref[hh, t], p.astype(BF16))

    def two_phases(i, carry):
        pair = t - 1 - 2 * i
        phase(pair, 0)
        phase(pair - 1, 1)
        return carry

    lax.fori_loop(0, t // 2, two_phases, 0)

    @pl.when(t % 2 == 1)
    def _():
        phase(0, 0, fetch=False)

    for hh in range(NH):
        o_ref[0, :, hh * Dh:(hh + 1) * Dh] = (acc_ref[hh] / l_ref[hh]).T.astype(BF16)


def _cast_slab_grid(rows, cols, n_steps):
    nc = 1
    while nc <= n_steps:
        nr = n_steps // nc
        if rows % (nr * SUBLANES_BF16) == 0 and cols % (nc * LANES) == 0:
            return nr, nc
        nc *= 2
    raise ValueError(f"no slab split of {(rows, cols)} for {n_steps} steps")


def _moba(qkv3, bias_a, weights, layer):
    b, s, _ = qkv3.shape
    P = MOBA_PAIR
    Dh = A_HEAD_DIM
    NH = MOBA_HEADS_PER_STEP
    G = A_HEADS // NH
    T = s // P
    nb = s // MOBA_BLOCK
    assert nb % SUBLANES_BF16 == 0 and nb <= LANES
    n_steps = b * G * T
    src_specs, dst_specs, dst_shapes = [], [], []
    for w in weights:
        _, r, c = w.shape
        nr, nc = _cast_slab_grid(r, c, n_steps)

        def slab(bi, h, t, nc=nc):
            step = (bi * G + h) * T + t
            return step // nc, step % nc

        src_specs.append(pl.BlockSpec((None, r // nr, c // nc),
                                      lambda bi, h, t, slab=slab: (layer, *slab(bi, h, t))))
        dst_specs.append(pl.BlockSpec((r // nr, c // nc), slab))
        dst_shapes.append(jax.ShapeDtypeStruct((r, c), BF16))
    outs = pl.pallas_call(
        functools.partial(_moba_kernel, n_casts=len(weights)),
        grid=(b, G, T),
        in_specs=[
            pl.BlockSpec((1, P, NH * Dh), lambda bi, h, t: (bi, t, h)),
            pl.BlockSpec((1, s, NH * Dh), lambda bi, h, t: (bi, 0, G + h)),
            pl.BlockSpec((1, s, NH * Dh), lambda bi, h, t: (bi, 0, 2 * G + h)),
            pl.BlockSpec((NH, P, P), lambda bi, h, t: (h, 0, 0)),
            pl.BlockSpec((NH, LANES, LANES), lambda bi, h, t: (h, 1, 2)),
        ] + src_specs,
        out_specs=[pl.BlockSpec((1, P, NH * Dh), lambda bi, h, t: (bi, t, h))] + dst_specs,
        out_shape=[jax.ShapeDtypeStruct((b, s, A_WIDTH), BF16)] + dst_shapes,
        scratch_shapes=[
            pltpu.VMEM((NH, nb, Dh), BF16),
            pltpu.VMEM((NH, nb, Dh), BF16),
            pltpu.VMEM((NH, s // P, Dh, P), BF16),
            pltpu.VMEM((NH, P, Dh + LANES), BF16),
            pltpu.VMEM((2, NH, P, P), F32),
            pltpu.VMEM((2, NH, 1, P), F32),
            pltpu.VMEM((NH, 1, P), F32),
            pltpu.VMEM((NH, 1, P), F32),
            pltpu.VMEM((NH, Dh, P), F32),
        ],
        compiler_params=pltpu.CompilerParams(
            dimension_semantics=("arbitrary", "arbitrary", "arbitrary"),
            vmem_limit_bytes=VMEM_LIMIT_BYTES),
        name="moba",
    )(qkv3, qkv3, qkv3, bias_a, bias_a, *weights)
    return outs[0], outs[1:]


def _swa_kernel(q_ref, kvx_ref, bias_ref, sink_ref, o_ref, vt_ref, ss_ref):
    t = pl.program_id(1)
    W = WINDOW
    n_win = q_ref.shape[1] // W
    seq = kvx_ref.shape[1]
    pairs_per_kv = B_Q_HEADS // B_KV_HEADS // 2
    n_combo = 2 * B_KV_HEADS
    chunk = 4 * W

    @pl.when(t == 0)
    def _():
        er = lax.broadcasted_iota(jnp.int32, (LANES, LANES), 0)
        ec = lax.broadcasted_iota(jnp.int32, (LANES, LANES), 1)
        eye = jnp.where(er == ec, 1.0, 0.0).astype(BF16)
        for i in range(n_combo):
            col = (n_combo + i) * LANES
            for c in range(seq // chunk):
                vt = _dot_nt(eye, kvx_ref[0, c * chunk:(c + 1) * chunk, col:col + LANES])
                for k in range(chunk // W):
                    vt_ref[i, c * (chunk // W) + k] = vt[:, k * W:(k + 1) * W].astype(BF16)

    def band(w):
        n = t * n_win + w
        return jnp.maximum(n - 1, 0), (n == 0).astype(jnp.int32)

    def window_scores(w):
        sb, first = band(w)
        start = pl.multiple_of(sb * W, W)
        maxima = []
        for kv in range(B_KV_HEADS):
            width = pairs_per_kv * LANES
            qg = q_ref[0, w * W:(w + 1) * W, kv * width:(kv + 1) * width]
            qs = jnp.concatenate(
                [qg[:, k * LANES:(k + 1) * LANES] for k in range(pairs_per_kv)], axis=0)
            for par in range(2):
                kcol = (2 * kv + par) * LANES
                kk = kvx_ref[0, pl.ds(start, 2 * W), kcol:kcol + LANES]
                sc = _dot_nt(kk, qs) + bias_ref[first, kv, par]
                ss_ref[w % 2, 2 * kv + par] = sc
                maxima.append(jnp.max(sc, axis=0, keepdims=True))
        return maxima

    def window_output(w, maxima):
        sb, _ = band(w)
        outs = []
        for i, bm in enumerate(maxima):
            sink = sink_ref[pl.ds(i, 1), :]
            m = jnp.maximum(bm, sink)
            pe = jnp.exp2(ss_ref[w % 2, i] - m)
            l = jnp.sum(pe, axis=0, keepdims=True) + jnp.exp2(sink - m)
            vt = jnp.concatenate([vt_ref[i, sb], vt_ref[i, sb + 1]], axis=1)
            outs.append(_dot(vt, pe.astype(BF16)) / l)
        for kv in range(B_KV_HEADS):
            ot = outs[2 * kv] + outs[2 * kv + 1]
            for k in range(pairs_per_kv):
                p = kv * pairs_per_kv + k
                o_ref[0, w * W:(w + 1) * W, p * LANES:(p + 1) * LANES] = (
                    ot[:, k * W:(k + 1) * W].T.astype(BF16))

    maxima = window_scores(0)
    for w in range(n_win):
        nxt = window_scores(w + 1) if w + 1 < n_win else None
        window_output(w, maxima)
        maxima = nxt


def _swa(qkv3, kvx3, bias_b, sink_rows):
    b, s, _ = qkv3.shape
    qb_blk = 3 * A_WIDTH // B_WIDTH
    return pl.pallas_call(
        _swa_kernel,
        grid=(b, s // SWA_TQ),
        in_specs=[
            pl.BlockSpec((1, SWA_TQ, B_WIDTH), lambda bi, t: (bi, t, qb_blk)),
            pl.BlockSpec((1, s, KVX_WIDTH), lambda bi, t: (bi, 0, 0)),
            pl.BlockSpec(bias_b.shape, lambda bi, t: (0, 0, 0, 0, 0)),
            pl.BlockSpec(sink_rows.shape, lambda bi, t: (0, 0)),
        ],
        out_specs=pl.BlockSpec((1, SWA_TQ, B_WIDTH), lambda bi, t: (bi, t, 0)),
        out_shape=jax.ShapeDtypeStruct((b, s, B_WIDTH), BF16),
        scratch_shapes=[
            pltpu.VMEM((2 * B_KV_HEADS, s // WINDOW, LANES, WINDOW), BF16),
            pltpu.VMEM((2, 2 * B_KV_HEADS, 2 * WINDOW, B_Q_HEADS // B_KV_HEADS // 2 * WINDOW), F32),
        ],
        compiler_params=pltpu.CompilerParams(
            dimension_semantics=("arbitrary", "arbitrary"),
            vmem_limit_bytes=VMEM_LIMIT_BYTES),
        name="swa",
    )(qkv3, kvx3, bias_b, sink_rows)


def _merge_kernel(ya_ref, yb_ref, gt_ref, x_ref, wa_ref, wb_ref, wo_ref, o_ref, mg_ref):
    ch = MERGE_CH
    for c in range(D_MODEL // ch):
        cols = slice(c * ch, (c + 1) * ch)
        ta = _dot(ya_ref[...], wa_ref[:, cols])
        tb = _dot(yb_ref[...], wb_ref[:, cols])
        ga = _sigmoid(gt_ref[:, cols].astype(F32))
        gb = _sigmoid(gt_ref[:, D_MODEL + c * ch:D_MODEL + (c + 1) * ch].astype(F32))
        mg_ref[:, cols] = (ga * ta + gb * tb).astype(BF16)
    for c in range(D_MODEL // ch):
        cols = slice(c * ch, (c + 1) * ch)
        o_ref[:, cols] = x_ref[:, cols] + _dot(mg_ref[...], wo_ref[:, cols])


def _merge(ya, yb, gates, x2, wa, wb, wo):
    n = x2.shape[0]
    tm = MERGE_TM
    resident = functools.partial(pl.BlockSpec, pipeline_mode=pl.Buffered(1))
    return pl.pallas_call(
        _merge_kernel,
        grid=(n // tm,),
        in_specs=[
            pl.BlockSpec((tm, A_WIDTH), lambda i: (i, 0)),
            pl.BlockSpec((tm, B_WIDTH), lambda i: (i, 0)),
            pl.BlockSpec((tm, 2 * D_MODEL), lambda i: (i, 0)),
            pl.BlockSpec((tm, D_MODEL), lambda i: (i, 0)),
            resident((A_WIDTH, D_MODEL), lambda i: (0, 0)),
            resident((B_WIDTH, D_MODEL), lambda i: (0, 0)),
            resident((D_MODEL, D_MODEL), lambda i: (0, 0)),
        ],
        out_specs=pl.BlockSpec((tm, D_MODEL), lambda i: (i, 0)),
        out_shape=jax.ShapeDtypeStruct((n, D_MODEL), F32),
        scratch_shapes=[pltpu.VMEM((tm, D_MODEL), BF16)],
        compiler_params=pltpu.CompilerParams(
            dimension_semantics=("arbitrary",),
            vmem_limit_bytes=VMEM_LIMIT_BYTES),
        name="merge",
    )(ya, yb, gates, x2, wa, wb, wo)


def _ffn_kernel(x_ref, g2_ref, wg_ref, wu_ref, wd_ref, o_ref, h_ref):
    f = pl.program_id(1)

    @pl.when(f == 0)
    def _():
        x = x_ref[...]
        ms = jnp.mean(x * x, axis=-1, keepdims=True)
        h_ref[...] = (x * lax.rsqrt(ms + EPS) * g2_ref[...]).astype(BF16)
        o_ref[...] = x

    h = h_ref[...]
    g = _dot(h, wg_ref[...])
    u = _dot(h, wu_ref[...])
    a = (g * _sigmoid(g) * u).astype(BF16)
    o_ref[...] += _dot(a, wd_ref[...])


def _ffn(x1, g2, wgu, wd):
    n = x1.shape[0]
    tm, tf = FFN_TM, FFN_TF
    nf = D_FF // tf
    return pl.pallas_call(
        _ffn_kernel,
        grid=(n // tm, nf),
        in_specs=[
            pl.BlockSpec((tm, D_MODEL), lambda i, f: (i, 0)),
            pl.BlockSpec((1, D_MODEL), lambda i, f: (0, 0)),
            pl.BlockSpec((D_MODEL, tf), lambda i, f: (0, f)),
            pl.BlockSpec((D_MODEL, tf), lambda i, f: (0, nf + f)),
            pl.BlockSpec((tf, D_MODEL), lambda i, f: (f, 0)),
        ],
        out_specs=pl.BlockSpec((tm, D_MODEL), lambda i, f: (i, 0)),
        out_shape=jax.ShapeDtypeStruct((n, D_MODEL), F32),
        scratch_shapes=[pltpu.VMEM((tm, D_MODEL), BF16)],
        compiler_params=pltpu.CompilerParams(
            dimension_semantics=("arbitrary", "arbitrary"),
            vmem_limit_bytes=VMEM_LIMIT_BYTES),
        name="ffn",
    )(x1, g2, wgu, wgu, wd)


def _moba_bucket_tables():
    L = MOBA_BLOCK
    kj = np.arange(L)[:, None]
    qi = np.arange(L)[None, :]
    return np.stack([_t5_bucket_np(qi - kj), _t5_bucket_np(L + qi - kj)])


def _swa_bias_tables():
    W = WINDOW
    qi = np.arange(W)[None, :]
    c = np.arange(2 * W)[:, None]
    d = W + qi - c
    v = (d >= 0) & (d < W)
    d0 = qi - c
    v0 = (d0 >= 0) & (d0 < W)
    buckets = np.stack([_t5_bucket_np(d), _t5_bucket_np(d0)])
    valid = np.stack([v, v0]).astype(np.int32)
    return buckets, valid


def kernel(x, norm1_g, w_in, q_norm_a, k_norm_a, q_norm_b, k_norm_b, rel_bias, sinks,
           w_branch_a, w_branch_b, w_out, norm2_g, w_gate_up, w_down):
    b, s, d = x.shape
    depth = w_in.shape[0]
    assert d == D_MODEL and s % MOBA_PAIR == 0 and s % SWA_TQ == 0
    assert (b * s) % IN_TM == 0 and (b * s) % MERGE_TM == 0 and (b * s) % FFN_TM == 0
    far = _t5_bucket_np(np.arange(LANES + 1, s + MOBA_PAIR))
    assert (far == far[0]).all()
    far_bucket = int(far[0])

    bias_a = _moba_bias(rel_bias, jnp.asarray(_moba_bucket_tables()), far_bucket)
    bk_b, va_b = _swa_bias_tables()
    bias_b = _swa_bias(rel_bias, jnp.asarray(bk_b), jnp.asarray(va_b))
    x2 = x.reshape(b * s, d)
    for l in range(depth):
        colgain = _inproj_col_gains(q_norm_a[l], k_norm_a[l], q_norm_b[l], k_norm_b[l])
        qkv, gates, kvx = _inproj(x2, norm1_g[l][None, :], w_in[l], colgain)
        qkv3 = qkv.reshape(b, s, -1)
        ya, (wa, wb, wo, wgu, wd) = _moba(
            qkv3, bias_a, [w_branch_a, w_branch_b, w_out, w_gate_up, w_down], l)
        per_kv = B_Q_HEADS // B_KV_HEADS
        sink_rows = jnp.repeat(
            sinks[l].reshape(B_KV_HEADS, per_kv // 2, 2).transpose(0, 2, 1).reshape(2 * B_KV_HEADS, -1),
            WINDOW, axis=1) * LOG2E
        sink_rows = jnp.pad(sink_rows, ((0, -2 * B_KV_HEADS % SUBLANES_F32), (0, 0)))
        yb = _swa(qkv3, kvx.reshape(b, s, -1), bias_b, sink_rows)
        x1 = _merge(ya.reshape(b * s, -1), yb.reshape(b * s, -1), gates, x2, wa, wb, wo)
        x2 = _ffn(x1, norm2_g[l][None, :], wgu, wd)
    return x2.reshape(b, s, d)
```

```python
import functools
import math

import jax
import jax.numpy as jnp
import numpy as np
from jax import lax
from jax.experimental import pallas as pl
from jax.experimental.pallas import tpu as pltpu

F32 = jnp.float32
BF16 = jnp.bfloat16

D_MODEL = 2048
A_HEADS = 8
A_HEAD_DIM = 128
A_WIDTH = A_HEADS * A_HEAD_DIM
MOBA_BLOCK = 256
MOBA_TOPK = 3
B_Q_HEADS = 16
B_KV_HEADS = 2
B_HEAD_DIM = 64
B_WIDTH = B_Q_HEADS * B_HEAD_DIM
B_KV_WIDTH = B_KV_HEADS * B_HEAD_DIM
WINDOW = 128
NUM_BUCKETS = 32
MAX_DISTANCE = 128
MAX_EXACT = NUM_BUCKETS // 2
D_FF = -(-(8 * D_MODEL) // (3 * 256)) * 256
IN_WIDTH = 3 * A_WIDTH + B_WIDTH + 2 * B_KV_WIDTH + 2 * D_MODEL
EPS = 1e-6
NEG = -1e30
LOG2E = math.log2(math.e)

LANES = 128
SUBLANES_F32 = 8
SUBLANES_BF16 = 16
VMEM_LIMIT_BYTES = 56 * 1024 * 1024

IN_TN = 1024
IN_QKV_TILES = (3 * A_WIDTH + B_WIDTH) // IN_TN
IN_GATE_TILES = (2 * D_MODEL) // IN_TN
IN_TILES = IN_QKV_TILES + IN_GATE_TILES + 1
IN_KV_COL = 3 * A_WIDTH + B_WIDTH
IN_GATE_COL = IN_KV_COL + 2 * B_KV_WIDTH
IN_TM = 1024
KVX_WIDTH = 8 * LANES

MOBA_PAIR = 2 * MOBA_BLOCK
MOBA_HEADS_PER_STEP = 4
SWA_TQ = 1024
MERGE_TM = 512
MERGE_CH = 512
FFN_TM = 1024
FFN_TF = 512


def _dot(a, b):
    return jnp.dot(a, b, preferred_element_type=F32)


def _dot_nt(a, b):
    return lax.dot_general(a, b, (((1,), (1,)), ((), ())), preferred_element_type=F32)


def _sigmoid(v):
    return 0.5 * jnp.tanh(0.5 * v) + 0.5


def _t5_bucket_np(dist):
    n = np.maximum(dist, 0)
    nf = np.maximum(n, 1).astype(np.float32)
    large = MAX_EXACT + (np.log(nf / np.float32(MAX_EXACT))
                         / np.float32(math.log(MAX_DISTANCE / MAX_EXACT))
                         * np.float32(NUM_BUCKETS - MAX_EXACT)).astype(np.int32)
    large = np.minimum(large, NUM_BUCKETS - 1)
    return np.where(n < MAX_EXACT, n, large).astype(np.int32)


def _bucket_lookup(tab_ref, bm, h):
    acc = jnp.zeros(bm.shape, F32)
    for b in range(NUM_BUCKETS):
        acc = jnp.where(bm == b, tab_ref[b, h], acc)
    return acc


def _swa_bias_kernel(tab_ref, bucket_ref, valid_ref, o_ref):
    per_kv = B_Q_HEADS // B_KV_HEADS
    c = bucket_ref.shape[2]
    for hl in range(per_kv):
        h = A_HEADS + pl.program_id(0) * per_kv + hl
        for t in range(bucket_ref.shape[0]):
            o_ref[t, 0, hl % 2, :, (hl // 2) * c:(hl // 2 + 1) * c] = jnp.where(
                valid_ref[t] != 0, _bucket_lookup(tab_ref, bucket_ref[t], h) * LOG2E, NEG)


def _swa_bias(rel_bias, buckets, valid):
    n_tiles, r, c = buckets.shape
    per_kv = B_Q_HEADS // B_KV_HEADS
    return pl.pallas_call(
        _swa_bias_kernel,
        grid=(B_KV_HEADS,),
        in_specs=[
            pl.BlockSpec(memory_space=pltpu.SMEM),
            pl.BlockSpec((n_tiles, r, c), lambda kv: (0, 0, 0)),
            pl.BlockSpec((n_tiles, r, c), lambda kv: (0, 0, 0)),
        ],
        out_specs=pl.BlockSpec((n_tiles, 1, 2, r, c * per_kv // 2), lambda kv: (0, kv, 0, 0, 0)),
        out_shape=jax.ShapeDtypeStruct((n_tiles, B_KV_HEADS, 2, r, c * per_kv // 2), F32),
        name="swa_bias",
    )(rel_bias, buckets, valid)


def _moba_bias_kernel(tab_ref, bucket_ref, o_ref, *, shift_bucket):
    L = MOBA_BLOCK
    kk = lax.broadcasted_iota(jnp.int32, (L, L), 0)
    qq = lax.broadcasted_iota(jnp.int32, (L, L), 1)
    for hl in range(o_ref.shape[0]):
        h = pl.program_id(0) * o_ref.shape[0] + hl
        shift = tab_ref[shift_bucket, h]
        tiles = [(_bucket_lookup(tab_ref, bucket_ref[t], h) - shift) * LOG2E for t in range(2)]
        own = jnp.where(qq >= kk, tiles[0], NEG)
        o_ref[hl, :L, :L] = own
        o_ref[hl, L:, L:] = own
        o_ref[hl, :L, L:] = tiles[1]
        o_ref[hl, L:, :L] = jnp.full((L, L), NEG, F32)


def _moba_bias(rel_bias, buckets, shift_bucket):
    L = MOBA_BLOCK
    heads_per_step = MOBA_HEADS_PER_STEP
    return pl.pallas_call(
        functools.partial(_moba_bias_kernel, shift_bucket=shift_bucket),
        grid=(A_HEADS // heads_per_step,),
        in_specs=[
            pl.BlockSpec(memory_space=pltpu.SMEM),
            pl.BlockSpec((2, L, L), lambda h: (0, 0, 0)),
        ],
        out_specs=pl.BlockSpec((heads_per_step, 2 * L, 2 * L), lambda h: (h, 0, 0)),
        out_shape=jax.ShapeDtypeStruct((A_HEADS, 2 * L, 2 * L), F32),
        name="moba_bias",
    )(rel_bias, buckets)


def _inproj_kernel(x_ref, g1_ref, w_ref, wkv_ref, cg_ref,
                   qkv_ref, gates_ref, kvx_ref, h_ref):
    j = pl.program_id(1)
    jt = j - 1
    cg = cg_ref[pl.ds(jnp.maximum(jt, 0), 1), :]

    def project(cols=slice(None), rows=slice(None)):
        return _dot(h_ref[rows, :], w_ref[:, cols].astype(BF16))

    def normed_tile(dh):
        n_chunks = 4
        rc = IN_TM // n_chunks
        for r in range(n_chunks):
            rows = slice(r * rc, (r + 1) * rc)
            qkv_ref[rows, :] = head_norm(project(rows=rows), dh, cg).astype(BF16)

    def head_norm(v, dh, gain):
        parts = []
        for k in range(v.shape[1] // LANES):
            vk = v[:, k * LANES:(k + 1) * LANES]
            sq = vk * vk
            if dh == LANES:
                r = lax.rsqrt(jnp.sum(sq, axis=-1, keepdims=True) * (1.0 / dh) + EPS)
            else:
                lo = lax.broadcasted_iota(jnp.int32, vk.shape, 1) < dh
                ss_lo = jnp.sum(jnp.where(lo, sq, 0.0), axis=-1, keepdims=True)
                ss_hi = jnp.sum(jnp.where(lo, 0.0, sq), axis=-1, keepdims=True)
                r = jnp.where(lo, lax.rsqrt(ss_lo * (1.0 / dh) + EPS),
                              lax.rsqrt(ss_hi * (1.0 / dh) + EPS))
            parts.append(vk * r)
        return jnp.concatenate(parts, axis=1) * gain

    n_a = 2 * A_WIDTH // IN_TN
    n_va = 3 * A_WIDTH // IN_TN

    @pl.when((jt >= 0) & (jt < n_a))
    def _():
        normed_tile(A_HEAD_DIM)

    @pl.when((jt >= n_a) & (jt < n_va))
    def _():
        qkv_ref[...] = project().astype(BF16)

    @pl.when((jt >= n_va) & (jt < IN_QKV_TILES))
    def _():
        normed_tile(B_HEAD_DIM)

    @pl.when(jt >= IN_QKV_TILES)
    def _():
        gates_ref[...] = project().astype(BF16)

    @pl.when(j == 0)
    def _():
        x = x_ref[...]
        ms = jnp.mean(x * x, axis=-1, keepdims=True)
        h_ref[...] = (x * lax.rsqrt(ms + EPS) * g1_ref[...]).astype(BF16)
        y = _dot(h_ref[...], wkv_ref[...].astype(BF16))
        yk = y[:, :LANES]
        br = lax.broadcasted_iota(jnp.int32, (LANES, LANES), 0) // B_HEAD_DIM
        bc = lax.broadcasted_iota(jnp.int32, (LANES, LANES), 1) // B_HEAD_DIM
        ss = _dot((yk * yk).astype(BF16), jnp.where(br == bc, 1.0, 0.0).astype(BF16))
        kn = (yk * lax.rsqrt(ss * (1.0 / B_HEAD_DIM) + EPS)
              * cg_ref[IN_TILES - 1:IN_TILES, :LANES])
        vv = y[:, LANES:2 * LANES]
        lo = lax.broadcasted_iota(jnp.int32, kn.shape, 1) < B_HEAD_DIM
        for base, t in ((0, kn), (4, vv)):
            e0 = jnp.where(lo, t, 0.0)
            o1 = jnp.where(lo, 0.0, t)
            o0 = pltpu.roll(e0, B_HEAD_DIM, 1)
            e1 = pltpu.roll(o1, B_HEAD_DIM, 1)
            for k, piece in enumerate((e0, o0, e1, o1)):
                kvx_ref[:, (base + k) * LANES:(base + k + 1) * LANES] = piece.astype(BF16)


def _inproj_w_col(j):
    t = IN_TN // LANES
    jt = jnp.where(j == 0, IN_TILES - 2, j - 1)
    lane_tile = jnp.where(jt < IN_QKV_TILES, jt * t,
                          IN_GATE_COL // LANES + (jt - IN_QKV_TILES) * t)
    return lane_tile * LANES


def _inproj_col_gains(q_norm_a, k_norm_a, q_norm_b, k_norm_b):
    ones = jnp.ones((IN_TN,), F32)
    rows = (
        [jnp.tile(q_norm_a * (A_HEAD_DIM ** -0.5 * LOG2E), IN_TN // A_HEAD_DIM)]
        * (A_WIDTH // IN_TN)
        + [jnp.tile(k_norm_a, IN_TN // A_HEAD_DIM)] * (A_WIDTH // IN_TN)
        + [ones] * (A_WIDTH // IN_TN)
        + [jnp.tile(q_norm_b * (B_HEAD_DIM ** -0.5 * LOG2E), IN_TN // B_HEAD_DIM)]
        * (B_WIDTH // IN_TN)
        + [ones] * IN_GATE_TILES
        + [jnp.concatenate([jnp.tile(k_norm_b, B_KV_HEADS),
                            jnp.ones((IN_TN - B_KV_WIDTH,), F32)])])
    assert len(rows) == IN_TILES
    return jnp.stack(rows + [ones] * (-IN_TILES % SUBLANES_F32))


def _inproj(x2, g1, w, colgain):
    n = x2.shape[0]
    grid = (n // IN_TM, IN_TILES)
    return pl.pallas_call(
        _inproj_kernel,
        grid=grid,
        in_specs=[
            pl.BlockSpec((IN_TM, D_MODEL), lambda i, j: (i, 0)),
            pl.BlockSpec((1, D_MODEL), lambda i, j: (0, 0)),
            pl.BlockSpec((pl.Element(D_MODEL), pl.Element(IN_TN)),
                         lambda i, j: (0, _inproj_w_col(j))),
            pl.BlockSpec((pl.Element(D_MODEL), pl.Element(2 * B_KV_WIDTH)),
                         lambda i, j: (0, IN_KV_COL)),
            pl.BlockSpec(colgain.shape, lambda i, j: (0, 0)),
        ],
        out_specs=[
            pl.BlockSpec((IN_TM, IN_TN), lambda i, j: (i, jnp.clip(j - 1, 0, IN_QKV_TILES - 1))),
            pl.BlockSpec((IN_TM, IN_TN),
                         lambda i, j: (i, jnp.clip(j - 1 - IN_QKV_TILES, 0, IN_GATE_TILES - 1))),
            pl.BlockSpec((IN_TM, KVX_WIDTH), lambda i, j: (i, 0)),
        ],
        out_shape=[
            jax.ShapeDtypeStruct((n, IN_QKV_TILES * IN_TN), BF16),
            jax.ShapeDtypeStruct((n, IN_GATE_TILES * IN_TN), BF16),
            jax.ShapeDtypeStruct((n, KVX_WIDTH), BF16),
        ],
        scratch_shapes=[pltpu.VMEM((IN_TM, D_MODEL), BF16)],
        compiler_params=pltpu.CompilerParams(
            dimension_semantics=("arbitrary", "arbitrary"),
            vmem_limit_bytes=VMEM_LIMIT_BYTES),
        name="inproj",
    )(x2, g1, w, w, colgain)


def _moba_kernel(*refs, n_casts):
    q_ref, k_ref, v_ref, bd_ref, bc_ref = refs[:5]
    cast_src = refs[5:5 + n_casts]
    o_ref = refs[5 + n_casts]
    cast_dst = refs[6 + n_casts:6 + 2 * n_casts]
    (kmh_ref, kml_ref, vt_ref, qa_ref, s_ref, bm_ref, m_ref, l_ref,
     acc_ref) = refs[6 + 2 * n_casts:]

    t = pl.program_id(2)
    L = MOBA_BLOCK
    P = MOBA_PAIR
    Dh = A_HEAD_DIM
    NH = MOBA_HEADS_PER_STEP
    seq = k_ref.shape[1]
    nb = seq // L

    @pl.when(t == 0)
    def _():
        r = lax.broadcasted_iota(jnp.int32, (nb, seq), 0)
        c = lax.broadcasted_iota(jnp.int32, (nb, seq), 1)
        ind = jnp.where(lax.shift_right_logical(c, int(math.log2(L))) == r, 1.0, 0.0).astype(BF16)
        for hh in range(NH):
            km = _dot(ind, k_ref[0, :, hh * Dh:(hh + 1) * Dh]) * (1.0 / L)
            hi = km.astype(BF16)
            kmh_ref[hh] = hi
            kml_ref[hh] = (km - hi.astype(F32)).astype(BF16)

    er = lax.broadcasted_iota(jnp.int32, (Dh, Dh), 0)
    ec = lax.broadcasted_iota(jnp.int32, (Dh, Dh), 1)
    eye = jnp.where(er == ec, 1.0, 0.0).astype(BF16)
    diag0 = pl.multiple_of(t * P, P)
    for hh in range(NH):
        vt_ref[hh, t] = _dot_nt(
            eye, v_ref[0, pl.ds(diag0, P), hh * Dh:(hh + 1) * Dh]).astype(BF16)

    lane = lax.broadcasted_iota(jnp.int32, (P, LANES), 1)
    row = lax.broadcasted_iota(jnp.int32, (P, LANES), 0)
    upper = (row >= L).astype(jnp.int32)

    def select(hh):
        q = q_ref[0, :, hh * Dh:(hh + 1) * Dh]
        g = _dot_nt(kmh_ref[hh], q) + _dot_nt(kml_ref[hh], q)
        blk = lax.broadcasted_iota(jnp.int32, g.shape, 0)
        blk_f = blk.astype(F32)
        qcol = lax.broadcasted_iota(jnp.int32, g.shape, 1)
        own_blk = 2 * t + (qcol >= L).astype(jnp.int32)
        past = blk < own_blk
        g = jnp.where(past, g, NEG)
        sel = jnp.full(g.shape, NEG, F32)
        for _ in range(MOBA_TOPK):
            mx = jnp.max(g, axis=0, keepdims=True)
            first = jnp.min(jnp.where(g == mx, blk_f, float(LANES)), axis=0, keepdims=True)
            hit = blk_f == first
            sel = jnp.where(hit & past, 0.0, sel)
            g = jnp.where(hit, -jnp.inf, g)
        sel = jnp.where(blk == own_blk, 0.0, sel)
        sel = jnp.concatenate([sel, jnp.zeros((LANES - nb, P), F32)], axis=0)
        qa_ref[hh, :, :Dh] = q
        qa_ref[hh, :, Dh:] = sel.T.astype(BF16)

    def scores(hh, pair):
        r0 = pl.multiple_of(pair * P, P)
        ind = jnp.where(lane == 2 * pair + upper, 1.0, 0.0).astype(BF16)
        k_aug = jnp.concatenate([k_ref[0, pl.ds(r0, P), hh * Dh:(hh + 1) * Dh], ind], axis=1)
        return _dot_nt(k_aug, qa_ref[hh])

    def accumulate(hh, pair, buf):
        m_old = m_ref[hh]
        m_new = jnp.maximum(m_old, bm_ref[buf, hh])
        alpha = jnp.exp2(m_old - m_new)
        p = jnp.exp2(s_ref[buf, hh] - m_new)
        l_ref[hh] = alpha * l_ref[hh] + jnp.sum(p, axis=0, keepdims=True)
        acc_ref[hh] = alpha * acc_ref[hh] + _dot(vt_ref[hh, pair], p.astype(BF16))
        m_ref[hh] = m_new

    def phase(pair, buf, fetch=True):
        if fetch:
            for hh in range(NH):
                sc = scores(hh, jnp.maximum(pair - 1, 0))
                s_ref[1 - buf, hh] = sc
                bm_ref[1 - buf, hh] = jnp.max(sc, axis=0, keepdims=True)
        for hh in range(NH):
            accumulate(hh, pair, buf)

    for hh in range(NH):
        select(hh)
    for src, dst in zip(cast_src, cast_dst):
        dst[...] = src[...].astype(BF16)
    for hh in range(NH):
        sd = scores(hh, t) + bd_ref[hh]
        s_ref[1, hh] = sd
        bm_ref[1, hh] = jnp.max(sd, axis=0, keepdims=True)
    for hh in range(NH):
        sc = scores(hh, jnp.maximum(t - 1, 0))
        corner = sc[P - LANES:, :LANES] + bc_ref[hh]
        s_ref[0, hh] = sc
        s_ref[0, hh, P - LANES:, :LANES] = corner
        bm = jnp.max(sc, axis=0, keepdims=True)
        cm = jnp.maximum(bm[:, :LANES], jnp.max(corner, axis=0, keepdims=True))
        bm_ref[0, hh] = jnp.concatenate([cm, bm[:, LANES:]], axis=1)
    for hh in range(NH):
        m0 = bm_ref[1, hh]
        p = jnp.exp2(s_ref[1, hh] - m0)
        m_ref[hh] = m0
        l_ref[hh] = jnp.sum(p, axis=0, keepdims=True)
        acc_ref[hh] = _dot(vt_ref[hh, t], p.astype(BF16))

    def two_phases(i, carry):
        pair = t - 1 - 2 * i
        phase(pair, 0)
        phase(pair - 1, 1)
        return carry

    lax.fori_loop(0, t // 2, two_phases, 0)

    @pl.when(t % 2 == 1)
    def _():
        phase(0, 0, fetch=False)

    for hh in range(NH):
        o_ref[0, :, hh * Dh:(hh + 1) * Dh] = (acc_ref[hh] / l_ref[hh]).T.astype(BF16)


def _cast_slab_grid(rows, cols, n_steps):
    nc = 1
    while nc <= n_steps:
        nr = n_steps // nc
        if rows % (nr * SUBLANES_BF16) == 0 and cols % (nc * LANES) == 0:
            return nr, nc
        nc *= 2
    raise ValueError(f"no slab split of {(rows, cols)} for {n_steps} steps")


def _moba(qkv3, bias_a, weights, layer):
    b, s, _ = qkv3.shape
    P = MOBA_PAIR
    Dh = A_HEAD_DIM
    NH = MOBA_HEADS_PER_STEP
    G = A_HEADS // NH
    T = s // P
    nb = s // MOBA_BLOCK
    assert nb % SUBLANES_BF16 == 0 and nb <= LANES
    n_steps = b * G * T
    src_specs, dst_specs, dst_shapes = [], [], []
    for w in weights:
        _, r, c = w.shape
        nr, nc = _cast_slab_grid(r, c, n_steps)

        def slab(bi, h, t, nc=nc):
            step = (bi * G + h) * T + t
            return step // nc, step % nc

        src_specs.append(pl.BlockSpec((None, r // nr, c // nc),
                                      lambda bi, h, t, slab=slab: (layer, *slab(bi, h, t))))
        dst_specs.append(pl.BlockSpec((r // nr, c // nc), slab))
        dst_shapes.append(jax.ShapeDtypeStruct((r, c), BF16))
    outs = pl.pallas_call(
        functools.partial(_moba_kernel, n_casts=len(weights)),
        grid=(b, G, T),
        in_specs=[
            pl.BlockSpec((1, P, NH * Dh), lambda bi, h, t: (bi, t, h)),
            pl.BlockSpec((1, s, NH * Dh), lambda bi, h, t: (bi, 0, G + h)),
            pl.BlockSpec((1, s, NH * Dh), lambda bi, h, t: (bi, 0, 2 * G + h)),
            pl.BlockSpec((NH, P, P), lambda bi, h, t: (h, 0, 0)),
            pl.BlockSpec((NH, LANES, LANES), lambda bi, h, t: (h, 1, 2)),
        ] + src_specs,
        out_specs=[pl.BlockSpec((1, P, NH * Dh), lambda bi, h, t: (bi, t, h))] + dst_specs,
        out_shape=[jax.ShapeDtypeStruct((b, s, A_WIDTH), BF16)] + dst_shapes,
        scratch_shapes=[
            pltpu.VMEM((NH, nb, Dh), BF16),
            pltpu.VMEM((NH, nb, Dh), BF16),
            pltpu.VMEM((NH, s // P, Dh, P), BF16),
            pltpu.VMEM((NH, P, Dh + LANES), BF16),
            pltpu.VMEM((2, NH, P, P), F32),
            pltpu.VMEM((2, NH, 1, P), F32),
            pltpu.VMEM((NH, 1, P), F32),
            pltpu.VMEM((NH, 1, P), F32),
            pltpu.VMEM((NH, Dh, P), F32),
        ],
        compiler_params=pltpu.CompilerParams(
            dimension_semantics=("arbitrary", "arbitrary", "arbitrary"),
            vmem_limit_bytes=VMEM_LIMIT_BYTES),
        name="moba",
    )(qkv3, qkv3, qkv3, bias_a, bias_a, *weights)
    return outs[0], outs[1:]


def _swa_kernel(q_ref, kvx_ref, bias_ref, sink_ref, o_ref, vt_ref):
    t = pl.program_id(1)
    W = WINDOW
    chunk = q_ref.shape[1]
    n_win = chunk // W
    pairs_per_kv = B_Q_HEADS // B_KV_HEADS // 2
    n_combo = 2 * B_KV_HEADS

    er = lax.broadcasted_iota(jnp.int32, (LANES, LANES), 0)
    ec = lax.broadcasted_iota(jnp.int32, (LANES, LANES), 1)
    eye = jnp.where(er == ec, 1.0, 0.0).astype(BF16)
    row0 = pl.multiple_of(t * chunk, chunk)
    for i in range(n_combo):
        col = (n_combo + i) * LANES
        vt = _dot_nt(eye, kvx_ref[0, pl.ds(row0, chunk), col:col + LANES])
        for k in range(n_win):
            vt_ref[i, t * n_win + k] = vt[:, k * W:(k + 1) * W].astype(BF16)

    def band(w):
        n = t * n_win + w
        return jnp.maximum(n - 1, 0), (n == 0).astype(jnp.int32)

    def window_scores(w):
        sb, first = band(w)
        start = pl.multiple_of(sb * W, W)
        scores = []
        for kv in range(B_KV_HEADS):
            width = pairs_per_kv * LANES
            qg = q_ref[0, w * W:(w + 1) * W, kv * width:(kv + 1) * width]
            qs = jnp.concatenate(
                [qg[:, k * LANES:(k + 1) * LANES] for k in range(pairs_per_kv)], axis=0)
            for par in range(2):
                kcol = (2 * kv + par) * LANES
                kk = kvx_ref[0, pl.ds(start, 2 * W), kcol:kcol + LANES]
                scores.append(_dot_nt(kk, qs) + bias_ref[first, kv, par])
        return scores

    def window_output(w, scores):
        sb, _ = band(w)
        outs = []
        for i, s in enumerate(scores):
            sink = sink_ref[pl.ds(i, 1), :]
            m = jnp.maximum(jnp.max(s, axis=0, keepdims=True), sink)
            pe = jnp.exp2(s - m)
            l = jnp.sum(pe, axis=0, keepdims=True) + jnp.exp2(sink - m)
            vt = jnp.concatenate([vt_ref[i, sb], vt_ref[i, sb + 1]], axis=1)
            outs.append(_dot(vt, pe.astype(BF16)) / l)
        for kv in range(B_KV_HEADS):
            ot = outs[2 * kv] + outs[2 * kv + 1]
            for k in range(pairs_per_kv):
                p = kv * pairs_per_kv + k
                o_ref[0, w * W:(w + 1) * W, p * LANES:(p + 1) * LANES] = (
                    ot[:, k * W:(k + 1) * W].T.astype(BF16))

    scores = window_scores(0)
    for w in range(n_win):
        nxt = window_scores(w + 1) if w + 1 < n_win else None
        window_output(w, scores)
        scores = nxt


def _swa(qkv3, kvx3, bias_b, sink_rows):
    b, s, _ = qkv3.shape
    qb_blk = 3 * A_WIDTH // B_WIDTH
    return pl.pallas_call(
        _swa_kernel,
        grid=(b, s // SWA_TQ),
        in_specs=[
            pl.BlockSpec((1, SWA_TQ, B_WIDTH), lambda bi, t: (bi, t, qb_blk)),
            pl.BlockSpec((1, s, KVX_WIDTH), lambda bi, t: (bi, 0, 0)),
            pl.BlockSpec(bias_b.shape, lambda bi, t: (0, 0, 0, 0, 0)),
            pl.BlockSpec(sink_rows.shape, lambda bi, t: (0, 0)),
        ],
        out_specs=pl.BlockSpec((1, SWA_TQ, B_WIDTH), lambda bi, t: (bi, t, 0)),
        out_shape=jax.ShapeDtypeStruct((b, s, B_WIDTH), BF16),
        scratch_shapes=[pltpu.VMEM((2 * B_KV_HEADS, s // WINDOW, LANES, WINDOW), BF16)],
        compiler_params=pltpu.CompilerParams(
            dimension_semantics=("arbitrary", "arbitrary"),
            vmem_limit_bytes=VMEM_LIMIT_BYTES),
        name="swa",
    )(qkv3, kvx3, bias_b, sink_rows)


def _merge_kernel(ya_ref, yb_ref, gt_ref, x_ref, wa_ref, wb_ref, wo_ref, o_ref, mg_ref):
    ch = MERGE_CH
    for c in range(D_MODEL // ch):
        cols = slice(c * ch, (c + 1) * ch)
        ta = _dot(ya_ref[...], wa_ref[:, cols])
        tb = _dot(yb_ref[...], wb_ref[:, cols])
        ga = _sigmoid(gt_ref[:, cols].astype(F32))
        gb = _sigmoid(gt_ref[:, D_MODEL + c * ch:D_MODEL + (c + 1) * ch].astype(F32))
        mg_ref[:, cols] = (ga * ta + gb * tb).astype(BF16)
    for c in range(D_MODEL // ch):
        cols = slice(c * ch, (c + 1) * ch)
        o_ref[:, cols] = x_ref[:, cols] + _dot(mg_ref[...], wo_ref[:, cols])


def _merge(ya, yb, gates, x2, wa, wb, wo):
    n = x2.shape[0]
    tm = MERGE_TM
    resident = functools.partial(pl.BlockSpec, pipeline_mode=pl.Buffered(1))
    return pl.pallas_call(
        _merge_kernel,
        grid=(n // tm,),
        in_specs=[
            pl.BlockSpec((tm, A_WIDTH), lambda i: (i, 0)),
            pl.BlockSpec((tm, B_WIDTH), lambda i: (i, 0)),
            pl.BlockSpec((tm, 2 * D_MODEL), lambda i: (i, 0)),
            pl.BlockSpec((tm, D_MODEL), lambda i: (i, 0)),
            resident((A_WIDTH, D_MODEL), lambda i: (0, 0)),
            resident((B_WIDTH, D_MODEL), lambda i: (0, 0)),
            resident((D_MODEL, D_MODEL), lambda i: (0, 0)),
        ],
        out_specs=pl.BlockSpec((tm, D_MODEL), lambda i: (i, 0)),
        out_shape=jax.ShapeDtypeStruct((n, D_MODEL), F32),
        scratch_shapes=[pltpu.VMEM((tm, D_MODEL), BF16)],
        compiler_params=pltpu.CompilerParams(
            dimension_semantics=("arbitrary",),
            vmem_limit_bytes=VMEM_LIMIT_BYTES),
        name="merge",
    )(ya, yb, gates, x2, wa, wb, wo)


def _ffn_kernel(x_ref, g2_ref, wg_ref, wu_ref, wd_ref, o_ref, h_ref):
    f = pl.program_id(1)

    @pl.when(f == 0)
    def _():
        x = x_ref[...]
        ms = jnp.mean(x * x, axis=-1, keepdims=True)
        h_ref[...] = (x * lax.rsqrt(ms + EPS) * g2_ref[...]).astype(BF16)
        o_ref[...] = x

    h = h_ref[...]
    g = _dot(h, wg_ref[...])
    u = _dot(h, wu_ref[...])
    a = (g * _sigmoid(g) * u).astype(BF16)
    o_ref[...] += _dot(a, wd_ref[...])


def _ffn(x1, g2, wgu, wd):
    n = x1.shape[0]
    tm, tf = FFN_TM, FFN_TF
    nf = D_FF // tf
    return pl.pallas_call(
        _ffn_kernel,
        grid=(n // tm, nf),
        in_specs=[
            pl.BlockSpec((tm, D_MODEL), lambda i, f: (i, 0)),
            pl.BlockSpec((1, D_MODEL), lambda i, f: (0, 0)),
            pl.BlockSpec((D_MODEL, tf), lambda i, f: (0, f)),
            pl.BlockSpec((D_MODEL, tf), lambda i, f: (0, nf + f)),
            pl.BlockSpec((tf, D_MODEL), lambda i, f: (f, 0)),
        ],
        out_specs=pl.BlockSpec((tm, D_MODEL), lambda i, f: (i, 0)),
        out_shape=jax.ShapeDtypeStruct((n, D_MODEL), F32),
        scratch_shapes=[pltpu.VMEM((tm, D_MODEL), BF16)],
        compiler_params=pltpu.CompilerParams(
            dimension_semantics=("arbitrary", "arbitrary"),
            vmem_limit_bytes=VMEM_LIMIT_BYTES),
        name="ffn",
    )(x1, g2, wgu, wgu, wd)


def _moba_bucket_tables():
    L = MOBA_BLOCK
    kj = np.arange(L)[:, None]
    qi = np.arange(L)[None, :]
    return np.stack([_t5_bucket_np(qi - kj), _t5_bucket_np(L + qi - kj)])


def _swa_bias_tables():
    W = WINDOW
    qi = np.arange(W)[None, :]
    c = np.arange(2 * W)[:, None]
    d = W + qi - c
    v = (d >= 0) & (d < W)
    d0 = qi - c
    v0 = (d0 >= 0) & (d0 < W)
    buckets = np.stack([_t5_bucket_np(d), _t5_bucket_np(d0)])
    valid = np.stack([v, v0]).astype(np.int32)
    return buckets, valid


def kernel(x, norm1_g, w_in, q_norm_a, k_norm_a, q_norm_b, k_norm_b, rel_bias, sinks,
           w_branch_a, w_branch_b, w_out, norm2_g, w_gate_up, w_down):
    b, s, d = x.shape
    depth = w_in.shape[0]
    assert d == D_MODEL and s % MOBA_PAIR == 0 and s % SWA_TQ == 0
    assert (b * s) % IN_TM == 0 and (b * s) % MERGE_TM == 0 and (b * s) % FFN_TM == 0
    far = _t5_bucket_np(np.arange(LANES + 1, s + MOBA_PAIR))
    assert (far == far[0]).all()
    far_bucket = int(far[0])

    bias_a = _moba_bias(rel_bias, jnp.asarray(_moba_bucket_tables()), far_bucket)
    bk_b, va_b = _swa_bias_tables()
    bias_b = _swa_bias(rel_bias, jnp.asarray(bk_b), jnp.asarray(va_b))
    x2 = x.reshape(b * s, d)
    for l in range(depth):
        colgain = _inproj_col_gains(q_norm_a[l], k_norm_a[l], q_norm_b[l], k_norm_b[l])
        qkv, gates, kvx = _inproj(x2, norm1_g[l][None, :], w_in[l], colgain)
        qkv3 = qkv.reshape(b, s, -1)
        ya, (wa, wb, wo, wgu, wd) = _moba(
            qkv3, bias_a, [w_branch_a, w_branch_b, w_out, w_gate_up, w_down], l)
        per_kv = B_Q_HEADS // B_KV_HEADS
        sink_rows = jnp.repeat(
            sinks[l].reshape(B_KV_HEADS, per_kv // 2, 2).transpose(0, 2, 1).reshape(2 * B_KV_HEADS, -1),
            WINDOW, axis=1) * LOG2E
        sink_rows = jnp.pad(sink_rows, ((0, -2 * B_KV_HEADS % SUBLANES_F32), (0, 0)))
        yb = _swa(qkv3, kvx.reshape(b, s, -1), bias_b, sink_rows)
        x1 = _merge(ya.reshape(b * s, -1), yb.reshape(b * s, -1), gates, x2, wa, wb, wo)
        x2 = _ffn(x1, norm2_g[l][None, :], wgu, wd)
    return x2.reshape(b, s, d)
```
